```python
import jax, jax.numpy as jnp
from jax import lax
import numpy as np

D_MODEL = 1024
BATCH = 16
SEQ = 2048
DEPTH = 2

CHUNK = 64
Q_BLOCK = 128
EPS = 1e-6

A_HEADS = 8
A_HEAD_DIM = 64
IDX_HEADS = 4
IDX_DIM = 64
TOPK_MAX = 256

B_HEADS = 8
B_HEAD_DIM = 64

C_HEADS = 4
C_KEY_DIM = 64
C_VAL_DIM = 128
C_GATE_RANK = 16
C_GATE_TAU = 16.0

PEER_HEADS = 8
PEER_NKEYS = 128
PEER_EXPERTS = PEER_NKEYS * PEER_NKEYS
PEER_QDIM = 256
PEER_TOPK = 16
TOKEN_BLOCK = 128

N_BRANCH = 3
BRANCH_WIDTH = 512
COL_WIDTHS = (
    A_HEADS * A_HEAD_DIM,
    A_HEAD_DIM,
    A_HEAD_DIM,
    IDX_HEADS * IDX_DIM,
    IDX_DIM,
    IDX_HEADS,
    B_HEADS * B_HEAD_DIM,
    B_HEADS * B_HEAD_DIM,
    B_HEADS * B_HEAD_DIM,
    C_HEADS * C_KEY_DIM,
    C_HEADS * C_KEY_DIM,
    C_HEADS * C_VAL_DIM,
    C_HEADS * C_VAL_DIM,
    C_GATE_RANK,
    N_BRANCH * D_MODEL,
)
IN_COLS = sum(COL_WIDTHS)

kernel_name = "hybrid_dsa_stickbreak_gla_peer_trunk"


def rms_norm(x, g):
    xf = x.astype(jnp.float32)
    y = xf * lax.rsqrt(jnp.mean(xf * xf, axis=-1, keepdims=True) + EPS)
    return y.astype(x.dtype) * g


def to_blocks(x, nb):
    b, t = x.shape[:2]
    return jnp.moveaxis(x.reshape(b, nb, t // nb, *x.shape[2:]), 1, 0)


def from_blocks(y):
    y = jnp.moveaxis(y, 0, 1)
    return y.reshape(y.shape[0], y.shape[1] * y.shape[2], *y.shape[3:])


def dsa_attention(q, k, v, q_idx, k_idx, w_idx, q_gain, k_gain):
    bsz, t_len = q.shape[:2]
    n_sel = min(TOPK_MAX, t_len // 4)
    nb = t_len // Q_BLOCK
    q = rms_norm(q, q_gain)
    k = rms_norm(k, k_gain)
    scale = A_HEAD_DIM ** -0.5
    gather = jax.vmap(lambda table, idx: table[idx])
    keys_pos = jnp.arange(t_len)

    def block(args):
        i, qb, qib, wb = args
        t = i * Q_BLOCK + jnp.arange(Q_BLOCK)
        limit = (t // CHUNK + 1) * CHUNK
        admissible = keys_pos[None, :] < limit[:, None]
        rel = jax.nn.relu(jnp.einsum('bqhd,bsd->bqhs', qib, k_idx).astype(jnp.float32))
        score = jnp.einsum('bqh,bqhs->bqs', wb.astype(jnp.float32), rel)
        score = jnp.where(admissible[None], score, -jnp.inf)
        _, sel = lax.top_k(score, n_sel)
        valid = sel < limit[None, :, None]
        k_sel = gather(k, sel)
        v_sel = gather(v, sel)
        logits = jnp.einsum('bqhd,bqkd->bqhk', qb, k_sel).astype(jnp.float32) * scale
        logits = jnp.where(valid[:, :, None, :], logits, -jnp.inf)
        p = jax.nn.softmax(logits, axis=-1).astype(v.dtype)
        return jnp.einsum('bqhk,bqkd->bqhd', p, v_sel)

    out = lax.map(block, (jnp.arange(nb), to_blocks(q, nb), to_blocks(q_idx, nb), to_blocks(w_idx, nb)))
    return from_blocks(out).reshape(bsz, t_len, -1)


def stick_breaking_attention(q, k, v, q_gain, k_gain):
    bsz, t_len = q.shape[:2]
    nb = t_len // Q_BLOCK
    q = rms_norm(q, q_gain)
    k = rms_norm(k, k_gain)
    scale = B_HEAD_DIM ** -0.5
    keys_pos = jnp.arange(t_len)

    def block(args):
        i, qb = args
        t = i * Q_BLOCK + jnp.arange(Q_BLOCK)
        strict = keys_pos[None, :] < t[:, None]
        z = jnp.einsum('bqhd,bshd->bhqs', qb, k).astype(jnp.float32) * scale
        log_beta = jax.nn.log_sigmoid(z)
        log_keep = jnp.where(strict, jax.nn.log_sigmoid(-z), 0.0)
        later = lax.cumsum(log_keep, axis=3, reverse=True) - log_keep
        a = jnp.where(strict, jnp.exp(log_beta + later), 0.0).astype(v.dtype)
        return jnp.einsum('bhqs,bshd->bqhd', a, v)

    out = lax.map(block, (jnp.arange(nb), to_blocks(q, nb)))
    return from_blocks(out).reshape(bsz, t_len, -1)


def gla_attention(q, k, v, r, a_low, w_alpha, b_alpha, out_gain):
    bsz, t_len = q.shape[:2]
    nc = t_len // CHUNK
    f32 = jnp.float32
    log_alpha = jax.nn.log_sigmoid((a_low @ w_alpha + b_alpha).astype(f32)) / C_GATE_TAU
    log_alpha = log_alpha.reshape(bsz, t_len, C_HEADS, C_KEY_DIM)
    qs = q.astype(f32) * (C_KEY_DIM ** -0.5)

    def chunks(x):
        return jnp.swapaxes(to_blocks(x, nc), 2, 3)

    causal = jnp.tril(jnp.ones((CHUNK, CHUNK), dtype=bool))

    def step(state, inp):
        qc, kc, vc, lac = inp
        b = jnp.cumsum(lac, axis=2)
        o_inter = jnp.einsum('bhtk,bhkv->bhtv', qc * jnp.exp(b), state)
        diff = b[:, :, :, None, :] - b[:, :, None, :, :]
        decay = jnp.exp(jnp.where(causal[:, :, None], diff, -jnp.inf))
        att = jnp.einsum('bhtk,bhsk,bhtsk->bhts', qc, kc, decay)
        o_intra = jnp.einsum('bhts,bhsv->bhtv', att, vc)
        b_last = b[:, :, -1:, :]
        state = state * jnp.exp(b_last[:, :, 0, :, None]) + jnp.einsum(
            'bhsk,bhsv->bhkv', kc * jnp.exp(b_last - b), vc)
        return state, o_inter + o_intra

    state0 = jnp.zeros((bsz, C_HEADS, C_KEY_DIM, C_VAL_DIM), f32)
    _, o = lax.scan(step, state0, (chunks(qs), chunks(k.astype(f32)), chunks(v.astype(f32)), chunks(log_alpha)))
    o = from_blocks(jnp.swapaxes(o, 2, 3)).astype(q.dtype)
    o = rms_norm(o, out_gain.reshape(C_HEADS, C_VAL_DIM)) * jax.nn.silu(r)
    return o.reshape(bsz, t_len, -1)


def hybrid_mixer(n, w_in, b_gate, a_qn_g, a_kn_g, b_qn_g, b_kn_g, c_w_alpha, c_b_alpha, c_on_g,
                 w_oa, w_ob, w_oc, w_out):
    bsz, t_len, d = n.shape
    z = n @ w_in
    (aq, ak, av, iq, ik, iw, bq, bk, bv, cq, ck, cv, cr, ca, gl) = jnp.split(
        z, np.cumsum(COL_WIDTHS)[:-1].tolist(), axis=-1)
    ya = dsa_attention(aq.reshape(bsz, t_len, A_HEADS, A_HEAD_DIM), ak, av,
                       iq.reshape(bsz, t_len, IDX_HEADS, IDX_DIM), ik, iw, a_qn_g, a_kn_g)
    yb = stick_breaking_attention(bq.reshape(bsz, t_len, B_HEADS, B_HEAD_DIM),
                                  bk.reshape(bsz, t_len, B_HEADS, B_HEAD_DIM),
                                  bv.reshape(bsz, t_len, B_HEADS, B_HEAD_DIM), b_qn_g, b_kn_g)
    yc = gla_attention(cq.reshape(bsz, t_len, C_HEADS, C_KEY_DIM), ck.reshape(bsz, t_len, C_HEADS, C_KEY_DIM),
                       cv.reshape(bsz, t_len, C_HEADS, C_VAL_DIM), cr.reshape(bsz, t_len, C_HEADS, C_VAL_DIM),
                       ca, c_w_alpha, c_b_alpha, c_on_g)
    gates = jax.nn.sigmoid((gl + b_gate).astype(jnp.float32)).astype(n.dtype).reshape(bsz, t_len, N_BRANCH, d)
    merged = gates[:, :, 0] * (ya @ w_oa) + gates[:, :, 1] * (yb @ w_ob) + gates[:, :, 2] * (yc @ w_oc)
    return merged @ w_out


def peer_ffn(xn, w_pq, sub_keys, u_tab, v_tab):
    bsz, t_len, d = xn.shape
    nblk = (bsz * t_len) // TOKEN_BLOCK
    xt = xn.reshape(nblk, TOKEN_BLOCK, d)

    def block(xb):
        q = (xb @ w_pq).reshape(TOKEN_BLOCK, PEER_HEADS, 2, PEER_QDIM // 2)
        s = jnp.einsum('nhpd,hpkd->nhpk', q, sub_keys).astype(jnp.float32)
        top_s, top_i = lax.top_k(s, PEER_TOPK)
        cand = (top_s[:, :, 0, :, None] + top_s[:, :, 1, None, :]).reshape(TOKEN_BLOCK, PEER_HEADS, -1)
        cand_idx = (top_i[:, :, 0, :, None] * PEER_NKEYS + top_i[:, :, 1, None, :]).reshape(
            TOKEN_BLOCK, PEER_HEADS, -1)
        best_s, best_j = lax.top_k(cand, PEER_TOPK)
        expert = jnp.take_along_axis(cand_idx, best_j, axis=-1)
        g = jax.nn.softmax(best_s, axis=-1)
        u = u_tab[expert]
        vv = v_tab[expert]
        act = jax.nn.gelu(jnp.einsum('nd,nhkd->nhk', xb, u).astype(jnp.float32), approximate=False)
        return jnp.einsum('nhk,nhkd->nd', (g * act).astype(xb.dtype), vv)

    return lax.map(block, xt).reshape(bsz, t_len, d)


def setup_inputs(seed: int = 0) -> dict:
    key = jax.random.key(seed)
    ks = jax.random.split(key, 24)
    L, D = DEPTH, D_MODEL

    def nrm(k, shape, std):
        return jax.random.normal(k, shape, jnp.float32) * std

    return {
        "x": nrm(ks[0], (BATCH, SEQ, D), 1.0),
        "c": nrm(ks[1], (BATCH, D), 1.0),
        "norm1_g": 1.0 + nrm(ks[2], (L, D), 0.02),
        "norm2_g": 1.0 + nrm(ks[3], (L, D), 0.02),
        "w_mod": nrm(ks[4], (L, D, 6 * D), 0.5 * D ** -0.5),
        "b_mod": nrm(ks[5], (L, 6 * D), 0.02),
        "w_in": nrm(ks[6], (L, D, IN_COLS), D ** -0.5),
        "b_gate": nrm(ks[7], (L, N_BRANCH * D), 0.02),
        "a_qn_g": 1.0 + nrm(ks[8], (L, A_HEAD_DIM), 0.02),
        "a_kn_g": 1.0 + nrm(ks[9], (L, A_HEAD_DIM), 0.02),
        "b_qn_g": 1.0 + nrm(ks[10], (L, B_HEAD_DIM), 0.02),
        "b_kn_g": 1.0 + nrm(ks[11], (L, B_HEAD_DIM), 0.02),
        "c_w_alpha": nrm(ks[12], (L, C_GATE_RANK, C_HEADS * C_KEY_DIM), C_GATE_RANK ** -0.5),
        "c_b_alpha": nrm(ks[13], (L, C_HEADS * C_KEY_DIM), 0.02),
        "c_on_g": 1.0 + nrm(ks[14], (L, C_HEADS * C_VAL_DIM), 0.02),
        "w_oa": nrm(ks[15], (L, A_HEADS * A_HEAD_DIM, D), (A_HEADS * A_HEAD_DIM) ** -0.5),
        "w_ob": nrm(ks[16], (L, B_HEADS * B_HEAD_DIM, D), (B_HEADS * B_HEAD_DIM) ** -0.5),
        "w_oc": nrm(ks[17], (L, C_HEADS * C_VAL_DIM, D), (C_HEADS * C_VAL_DIM) ** -0.5),
        "w_out": nrm(ks[18], (L, D, D), D ** -0.5),
        "peer_wq": nrm(ks[19], (L, D, PEER_HEADS * PEER_QDIM), D ** -0.5),
        "peer_subkeys": nrm(ks[20], (L, PEER_HEADS, 2, PEER_NKEYS, PEER_QDIM // 2), (PEER_QDIM // 2) ** -0.5),
        "peer_u": nrm(ks[21], (L, PEER_EXPERTS, D), D ** -0.5),
        "peer_v": nrm(ks[22], (L, PEER_EXPERTS, D), 1.0),
    }


def reference(x, c, norm1_g, norm2_g, w_mod, b_mod, w_in, b_gate, a_qn_g, a_kn_g, b_qn_g, b_kn_g,
              c_w_alpha, c_b_alpha, c_on_g, w_oa, w_ob, w_oc, w_out,
              peer_wq, peer_subkeys, peer_u, peer_v):
    bsz, t_len, d = x.shape
    h = x
    c_act = jax.nn.silu(c)
    for l in range(DEPTH):
        mod = (c_act @ w_mod[l] + b_mod[l]).reshape(bsz, 6, 1, d)
        shift1, scale1, gate1 = mod[:, 0], mod[:, 1], mod[:, 2]
        shift2, scale2, gate2 = mod[:, 3], mod[:, 4], mod[:, 5]
        n = rms_norm(h, norm1_g[l]) * (1.0 + scale1) + shift1
        y = hybrid_mixer(n, w_in[l], b_gate[l], a_qn_g[l], a_kn_g[l], b_qn_g[l], b_kn_g[l],
                         c_w_alpha[l], c_b_alpha[l], c_on_g[l], w_oa[l], w_ob[l], w_oc[l], w_out[l])
        h = h + gate1 * y
        n = rms_norm(h, norm2_g[l]) * (1.0 + scale2) + shift2
        h = h + gate2 * peer_ffn(n, peer_wq[l], peer_subkeys[l], peer_u[l], peer_v[l])
    return h
```

```python
import functools

import jax
import jax.numpy as jnp
import numpy as np
from jax import lax
from jax.experimental import pallas as pl
from jax.experimental.pallas import tpu as pltpu

F32 = jnp.float32
BF16 = jnp.bfloat16

D_MODEL = 1024
EPS = 1e-6
CHUNK = 64
TOPK_MAX = 256
A_HEADS, A_DIM = 8, 64
IDX_HEADS, IDX_DIM = 4, 64
B_HEADS, B_DIM = 8, 64
C_HEADS, C_KEY, C_VAL, C_RANK, C_TAU = 4, 64, 128, 16, 16.0
PEER_HEADS, PEER_NKEYS, PEER_QDIM, PEER_TOPK = 8, 128, 256, 16
N_BRANCH = 3

LANE = 128

_SEGMENTS = (
    ("gl", 3072), ("aq", 512), ("bq", 512), ("bk", 512), ("bv", 512), ("cv", 512), ("cr", 512),
    ("iq", 256), ("cq", 256), ("ck", 256), ("akv", 128), ("ikw", 128), ("ca", 128),
)
COL = {}
_off = 0
for _name, _w in _SEGMENTS:
    COL[_name] = (_off, _w)
    _off += _w
Z_COLS = _off
Z_TILE = Z_COLS // 3

_REF_WIDTHS = (512, 64, 64, 256, 64, 4, 512, 512, 512, 256, 256, 512, 512, 16, 3072)


def _reorder_w_in(w_in):
    offs = np.cumsum((0,) + _REF_WIDTHS)
    p = [w_in[:, offs[i]:offs[i + 1]] for i in range(len(_REF_WIDTHS))]
    aq, ak, av, iq, ik, iw, bq, bk, bv, cq, ck, cv, cr, ca, gl = p
    d = w_in.shape[0]
    z = lambda n: jnp.zeros((d, n), w_in.dtype)
    cols = [gl, aq, bq, bk, bv, cv, cr, iq, cq, ck, ak, av, ik, iw, z(60), ca, z(112)]
    return jnp.concatenate(cols, axis=1).astype(BF16)


def _cb(name):
    off, w = COL[name]
    return off // w


def _mod_kernel(c_ref, w_ref, b_ref, o_ref):
    c = c_ref[...]
    ca = (c * jax.nn.sigmoid(c)).astype(BF16)
    o_ref[...] = jnp.dot(ca, w_ref[...].astype(BF16), preferred_element_type=F32) + b_ref[...]


def _mod(c, w_mod, b_mod):
    bsz, d = c.shape
    n = w_mod.shape[1]
    tn = 768
    return pl.pallas_call(
        _mod_kernel,
        grid=(n // tn,),
        in_specs=[pl.BlockSpec((bsz, d), lambda j: (0, 0)),
                  pl.BlockSpec((d, tn), lambda j: (0, j)),
                  pl.BlockSpec((1, tn), lambda j: (0, j))],
        out_specs=pl.BlockSpec((bsz, tn), lambda j: (0, j)),
        out_shape=jax.ShapeDtypeStruct((bsz, n), F32),
        name="adaln_mod",
    )(c, w_mod, b_mod.reshape(1, n))


def _inproj_kernel(h_ref, g_ref, sc_ref, sh_ref, w_ref, z_ref, n_scr):
    @pl.when(pl.program_id(1) == 0)
    def _():
        x = h_ref[...]
        y = x * lax.rsqrt(jnp.mean(x * x, axis=-1, keepdims=True) + EPS)
        n = y * g_ref[...] * (1.0 + sc_ref[0]) + sh_ref[0]
        n_scr[...] = n.astype(BF16)

    z_ref[...] = jnp.dot(n_scr[...], w_ref[...], preferred_element_type=F32).astype(BF16)


def _inproj(h2, norm_g, scale, shift, w_z, t_len, tm=512):
    n_tok, d = h2.shape
    tm = min(tm, t_len)
    bpt = t_len // tm
    return pl.pallas_call(
        _inproj_kernel,
        grid=(n_tok // tm, Z_COLS // Z_TILE),
        in_specs=[pl.BlockSpec((tm, d), lambda i, j: (i, 0)),
                  pl.BlockSpec((1, d), lambda i, j: (0, 0)),
                  pl.BlockSpec((1, 1, d), lambda i, j: (i // bpt, 0, 0)),
                  pl.BlockSpec((1, 1, d), lambda i, j: (i // bpt, 0, 0)),
                  pl.BlockSpec((d, Z_TILE), lambda i, j: (0, j))],
        out_specs=pl.BlockSpec((tm, Z_TILE), lambda i, j: (i, j)),
        out_shape=jax.ShapeDtypeStruct((n_tok, Z_COLS), BF16),
        scratch_shapes=[pltpu.VMEM((tm, d), BF16)],
        compiler_params=pltpu.CompilerParams(
            dimension_semantics=("parallel", "arbitrary"), vmem_limit_bytes=48 * 1024 * 1024),
        name="norm_inproj",
    )(h2, norm_g.reshape(1, d), scale, shift, w_z)


def _head_rms(x, gain):
    return x * lax.rsqrt(jnp.mean(x * x, axis=-1, keepdims=True) + EPS) * gain


DSA_QB = 128
DSA_GROUP = 4
DSA_BISECT = 30


def _dsa_kernel(aq_ref, akv_ref, iq_ref, ikwq_ref, ikwk_ref, qg_ref, kg_ref, o_ref, sm_scr, sel_scr,
                *, q_off, n_sel):
    qb = aq_ref.shape[1]
    tk = akv_ref.shape[1]
    q0 = (q_off + pl.program_id(1)) * qb
    t_pos = q0 + lax.broadcasted_iota(jnp.int32, (qb, 1), 0)
    limit = (t_pos // CHUNK + 1) * CHUNK
    s_pos = lax.broadcasted_iota(jnp.int32, (1, tk), 1)
    adm = s_pos < limit

    kidx = ikwk_ref[0][:, :IDX_DIM]
    qidx = iq_ref[0]
    w = ikwq_ref[0][:, IDX_DIM:IDX_DIM + IDX_HEADS].astype(F32)
    score = jnp.zeros((qb, tk), F32)
    for h in range(IDX_HEADS):
        rel = lax.dot_general(qidx[:, h * IDX_DIM:(h + 1) * IDX_DIM], kidx,
                              (((1,), (1,)), ((), ())), preferred_element_type=F32)
        score = score + w[:, h:h + 1] * jnp.maximum(rel, 0.0)
    neg_inf = jnp.float32(-jnp.inf)
    sm_scr[...] = jnp.where(adm, score, neg_inf)

    hi0 = jnp.max(sm_scr[...], axis=-1, keepdims=True)
    lo0 = jnp.min(jnp.where(adm, score, jnp.float32(jnp.inf)), axis=-1, keepdims=True)
    kf = jnp.float32(n_sel)

    def bisect(_, carry):
        lo, hi = carry
        mid = 0.5 * (lo + hi)
        cnt = jnp.sum(jnp.where(sm_scr[...] > mid, 1.0, 0.0), axis=-1, keepdims=True)
        below = cnt < kf
        return jnp.where(below, lo, mid), jnp.where(below, mid, hi)

    _, hi = lax.fori_loop(0, DSA_BISECT, bisect, (lo0, hi0))
    sm = sm_scr[...]
    v1 = jnp.max(jnp.where(sm <= hi, sm, neg_inf), axis=-1, keepdims=True)
    c_ge1 = jnp.sum(jnp.where(sm >= v1, 1.0, 0.0), axis=-1, keepdims=True)
    v2 = jnp.max(jnp.where(sm < v1, sm, neg_inf), axis=-1, keepdims=True)
    thr = jnp.where(c_ge1 >= kf, v1, v2)
    need = kf - jnp.sum(jnp.where(sm > thr, 1.0, 0.0), axis=-1, keepdims=True)

    r_i = lax.broadcasted_iota(jnp.int32, (LANE, LANE), 0)
    c_i = lax.broadcasted_iota(jnp.int32, (LANE, LANE), 1)
    incl = jnp.where(r_i <= c_i, 1.0, 0.0).astype(BF16)
    all_sel = limit <= n_sel
    run = jnp.zeros((qb, 1), F32)
    for j in range(tk // LANE):
        sl = slice(j * LANE, (j + 1) * LANE)
        smj = sm_scr[:, sl]
        eqj = jnp.where(smj == thr, 1.0, 0.0)
        inc = jnp.dot(eqj.astype(BF16), incl, preferred_element_type=F32)
        rank = inc - eqj + run
        picked = (smj > thr) | ((smj == thr) & (rank < need))
        admj = (j * LANE + lax.broadcasted_iota(jnp.int32, (1, LANE), 1)) < limit
        selj = (all_sel & admj) | (jnp.logical_not(all_sel) & picked)
        sel_scr[:, sl] = jnp.where(selj, 0.0, neg_inf)
        run = run + inc[:, LANE - 1:LANE]

    kv = akv_ref[0].astype(F32)
    k = _head_rms(kv[:, :A_DIM], kg_ref[...]).astype(BF16)
    v = akv_ref[0][:, A_DIM:2 * A_DIM]
    aq = aq_ref[0].astype(F32)
    scale = A_DIM ** -0.5
    for h in range(A_HEADS):
        qh = _head_rms(aq[:, h * A_DIM:(h + 1) * A_DIM], qg_ref[...]).astype(BF16)
        logits = lax.dot_general(qh, k, (((1,), (1,)), ((), ())), preferred_element_type=F32) * scale
        logits = logits + sel_scr[...]
        m = jnp.max(logits, axis=-1, keepdims=True)
        p = jnp.exp(logits - m)
        denom = jnp.sum(p, axis=-1, keepdims=True)
        oh = jnp.dot(p.astype(BF16), v, preferred_element_type=F32) / denom
        o_ref[0, :, h * A_DIM:(h + 1) * A_DIM] = oh.astype(o_ref.dtype)


def _dsa(z3, qg, kg):
    bsz, t_len, _ = z3.shape
    n_sel = min(TOPK_MAX, t_len // 4)
    span = DSA_QB * DSA_GROUP
    outs = []
    for g in range(t_len // span):
        tk = span * (g + 1)
        q_off = g * DSA_GROUP
        qmap = lambda b, i, c: (b, q_off + i, c)
        outs.append(pl.pallas_call(
            functools.partial(_dsa_kernel, q_off=q_off, n_sel=n_sel),
            grid=(bsz, DSA_GROUP),
            in_specs=[pl.BlockSpec((1, DSA_QB, 512), lambda b, i, q_off=q_off: (b, q_off + i, _cb("aq"))),
                      pl.BlockSpec((1, tk, LANE), lambda b, i: (b, 0, _cb("akv"))),
                      pl.BlockSpec((1, DSA_QB, 256), lambda b, i, q_off=q_off: (b, q_off + i, _cb("iq"))),
                      pl.BlockSpec((1, DSA_QB, LANE), lambda b, i, q_off=q_off: (b, q_off + i, _cb("ikw"))),
                      pl.BlockSpec((1, tk, LANE), lambda b, i: (b, 0, _cb("ikw"))),
                      pl.BlockSpec((1, A_DIM), lambda b, i: (0, 0)),
                      pl.BlockSpec((1, A_DIM), lambda b, i: (0, 0))],
            out_specs=pl.BlockSpec((1, DSA_QB, 512), lambda b, i: (b, i, 0)),
            out_shape=jax.ShapeDtypeStruct((bsz, span, 512), BF16),
            scratch_shapes=[pltpu.VMEM((DSA_QB, tk), F32), pltpu.VMEM((DSA_QB, tk), F32)],
            compiler_params=pltpu.CompilerParams(
                dimension_semantics=("parallel", "arbitrary"), vmem_limit_bytes=48 * 1024 * 1024),
            name=f"dsa_attention_g{g}",
        )(z3, z3, z3, z3, z3, qg.reshape(1, A_DIM), kg.reshape(1, A_DIM)))
    return jnp.concatenate(outs, axis=1)


def _split_dot(x, m, x_is_lhs=True):
    hi = x.astype(BF16)
    lo = (x - hi.astype(F32)).astype(BF16)
    if x_is_lhs:
        return jnp.dot(hi, m, preferred_element_type=F32) + jnp.dot(lo, m, preferred_element_type=F32)
    return jnp.dot(m, hi, preferred_element_type=F32) + jnp.dot(m, lo, preferred_element_type=F32)


SB_QB = 128


def _sb_kernel(q_ref, k_ref, v_ref, qg_ref, kg_ref, o_ref, kn_scr):
    qi = pl.program_id(2)
    qb = q_ref.shape[1]
    hd = B_DIM

    @pl.when(qi == 0)
    def _():
        kf = k_ref[0].astype(F32)
        for h in range(2):
            kn_scr[h] = _head_rms(kf[:, h * hd:(h + 1) * hd], kg_ref[...]).astype(BF16)

    q = q_ref[0].astype(F32)
    qn = [_head_rms(q[:, h * hd:(h + 1) * hd], qg_ref[...]).astype(BF16) for h in range(2)]
    r_i = lax.broadcasted_iota(jnp.int32, (qb, qb), 0)
    c_i = lax.broadcasted_iota(jnp.int32, (qb, qb), 1)
    m_incl = jnp.where(r_i >= c_i, 1.0, 0.0).astype(BF16)
    t_pos = qi * qb + lax.broadcasted_iota(jnp.int32, (qb, 1), 0)
    lane = lax.broadcasted_iota(jnp.int32, (1, 2 * hd), 1)
    scale = hd ** -0.5

    def body(i, carry):
        kb = qi - i
        ks = pl.multiple_of(kb * qb, qb)
        s_pos = ks + lax.broadcasted_iota(jnp.int32, (1, qb), 1)
        strict = s_pos < t_pos
        vblk = v_ref[0, pl.ds(ks, qb), :]
        out = []
        for h in range(2):
            run, acc = carry[2 * h], carry[2 * h + 1]
            kh = kn_scr[h, pl.ds(ks, qb), :]
            z = lax.dot_general(qn[h], kh, (((1,), (1,)), ((), ())), preferred_element_type=F32) * scale
            lb = jnp.minimum(z, 0.0) - jnp.log(1.0 + jnp.exp(-jnp.abs(z)))
            lk = jnp.where(strict, lb - z, 0.0)
            inc = _split_dot(lk, m_incl)
            later = inc - lk + run
            a = jnp.where(strict, jnp.exp(lb + later), 0.0)
            pv = jnp.dot(a.astype(BF16), vblk, preferred_element_type=F32)
            out += [run + inc[:, 0:1], acc + pv]
        return tuple(out)

    init = (jnp.zeros((qb, 1), F32), jnp.zeros((qb, 2 * hd), F32)) * 2
    res = lax.fori_loop(0, qi + 1, body, init)
    o_ref[0] = jnp.where(lane < hd, res[1], res[3]).astype(o_ref.dtype)


def _stickbreak(z3, qg, kg):
    bsz, t_len, _ = z3.shape
    nq = t_len // SB_QB
    npair = B_HEADS // 2
    cq, ck, cv = COL["bq"][0] // LANE, COL["bk"][0] // LANE, COL["bv"][0] // LANE
    return pl.pallas_call(
        _sb_kernel,
        grid=(bsz, npair, nq),
        in_specs=[pl.BlockSpec((1, SB_QB, LANE), lambda b, p, i: (b, i, cq + p)),
                  pl.BlockSpec((1, t_len, LANE), lambda b, p, i: (b, 0, ck + p)),
                  pl.BlockSpec((1, t_len, LANE), lambda b, p, i: (b, 0, cv + p)),
                  pl.BlockSpec((1, B_DIM), lambda b, p, i: (0, 0)),
                  pl.BlockSpec((1, B_DIM), lambda b, p, i: (0, 0))],
        out_specs=pl.BlockSpec((1, SB_QB, LANE), lambda b, p, i: (b, i, p)),
        out_shape=jax.ShapeDtypeStruct((bsz, t_len, B_HEADS * B_DIM), BF16),
        scratch_shapes=[pltpu.VMEM((2, t_len, B_DIM), BF16)],
        compiler_params=pltpu.CompilerParams(
            dimension_semantics=("parallel", "parallel", "arbitrary"), vmem_limit_bytes=48 * 1024 * 1024),
        name="stickbreak_attention",
    )(z3, z3, z3, qg.reshape(1, B_DIM), kg.reshape(1, B_DIM))


GLA_SUB = 16


def _gla_kernel(q_ref, k_ref, v_ref, r_ref, a_ref, wa_ref, ba_ref, og_ref, o_ref, st_scr):
    t_len = q_ref.shape[1]
    nc = t_len // CHUNK
    dk, dv, sub = C_KEY, C_VAL, GLA_SUB
    r_i = lax.broadcasted_iota(jnp.int32, (CHUNK, CHUNK), 0)
    c_i = lax.broadcasted_iota(jnp.int32, (CHUNK, CHUNK), 1)
    l_incl = jnp.where(r_i >= c_i, 1.0, 0.0).astype(BF16)
    sub_r = lax.broadcasted_iota(jnp.int32, (sub, 1), 0)
    st_scr[...] = jnp.zeros_like(st_scr)
    neg_inf = jnp.float32(-jnp.inf)

    def chunk(c, _):
        rows = pl.ds(pl.multiple_of(c * CHUNK, CHUNK), CHUNK)
        pre = jnp.dot(a_ref[0, rows, :], wa_ref[...], preferred_element_type=F32) + ba_ref[...]
        la2 = (jnp.minimum(pre, 0.0) - jnp.log(1.0 + jnp.exp(-jnp.abs(pre)))) * (1.0 / C_TAU)
        b2 = _split_dot(la2, l_incl, x_is_lhs=False)
        q2 = q_ref[0, rows, :].astype(F32) * (dk ** -0.5)
        k2 = k_ref[0, rows, :].astype(F32)
        for h in range(2):
            b = b2[:, h * dk:(h + 1) * dk]
            q = q2[:, h * dk:(h + 1) * dk]
            k = k2[:, h * dk:(h + 1) * dk]
            v = v_ref[0, rows, h * dv:(h + 1) * dv]
            vf = v.astype(F32)
            st = st_scr[h]
            o = lax.dot_general((q * jnp.exp(b)).astype(BF16), st.astype(BF16),
                                (((1,), (1,)), ((), ())), preferred_element_type=F32)
            intra = []
            for i in range(CHUNK // sub):
                lo_r = i * sub
                bi, qi_, ki, vi = (x[lo_r:lo_r + sub] for x in (b, q, k, vf))
                oi = jnp.zeros((sub, dv), F32)
                if i > 0:
                    ref_row = b[lo_r - 1:lo_r, :]
                    qs = (qi_ * jnp.exp(bi - ref_row)).astype(BF16)
                    ks = (k[:lo_r] * jnp.exp(ref_row - b[:lo_r])).astype(BF16)
                    att = lax.dot_general(qs, ks, (((1,), (1,)), ((), ())), preferred_element_type=F32)
                    oi = jnp.dot(att.astype(BF16), v[:lo_r], preferred_element_type=F32)
                rows_d = []
                for t in range(sub):
                    e = jnp.exp(jnp.where(sub_r <= t, bi[t:t + 1, :] - bi, neg_inf))
                    w = jnp.sum(qi_[t:t + 1, :] * ki * e, axis=-1, keepdims=True)
                    rows_d.append(jnp.sum(w * vi, axis=0, keepdims=True))
                intra.append(oi + jnp.concatenate(rows_d, axis=0))
            o = o + jnp.concatenate(intra, axis=0)
            b_last = b[CHUNK - 1:CHUNK, :]
            kd = (k * jnp.exp(b_last - b)).astype(BF16)
            upd = lax.dot_general(v, kd, (((0,), (0,)), ((), ())), preferred_element_type=F32)
            st_scr[h] = st * jnp.exp(b_last) + upd
            y = o * lax.rsqrt(jnp.mean(o * o, axis=-1, keepdims=True) + EPS) * og_ref[:, h * dv:(h + 1) * dv]
            r = r_ref[0, rows, h * dv:(h + 1) * dv].astype(F32)
            o_ref[0, rows, h * dv:(h + 1) * dv] = (y * (r * jax.nn.sigmoid(r))).astype(o_ref.dtype)
        return 0

    lax.fori_loop(0, nc, chunk, 0)


def _gla(z3, w_alpha, b_alpha, out_gain):
    bsz, t_len, _ = z3.shape
    npair = C_HEADS // 2
    wa = jnp.zeros((LANE, C_HEADS * C_KEY), F32).at[:C_RANK].set(w_alpha).astype(BF16)
    cq, ck = COL["cq"][0] // LANE, COL["ck"][0] // LANE
    cv, cr, ca = COL["cv"][0] // 256, COL["cr"][0] // 256, COL["ca"][0] // LANE
    return pl.pallas_call(
        _gla_kernel,
        grid=(bsz, npair),
        in_specs=[pl.BlockSpec((1, t_len, LANE), lambda b, p: (b, 0, cq + p)),
                  pl.BlockSpec((1, t_len, LANE), lambda b, p: (b, 0, ck + p)),
                  pl.BlockSpec((1, t_len, 256), lambda b, p: (b, 0, cv + p)),
                  pl.BlockSpec((1, t_len, 256), lambda b, p: (b, 0, cr + p)),
                  pl.BlockSpec((1, t_len, LANE), lambda b, p: (b, 0, ca)),
                  pl.BlockSpec((LANE, LANE), lambda b, p: (0, p)),
                  pl.BlockSpec((1, LANE), lambda b, p: (0, p)),
                  pl.BlockSpec((1, 256), lambda b, p: (0, p))],
        out_specs=pl.BlockSpec((1, t_len, 256), lambda b, p: (b, 0, p)),
        out_shape=jax.ShapeDtypeStruct((bsz, t_len, C_HEADS * C_VAL), BF16),
        scratch_shapes=[pltpu.VMEM((2, C_VAL, C_KEY), F32)],
        compiler_params=pltpu.CompilerParams(
            dimension_semantics=("parallel", "parallel"), vmem_limit_bytes=48 * 1024 * 1024),
        name="gla_attention",
    )(z3, z3, z3, z3, z3, wa, b_alpha.reshape(1, -1), out_gain.reshape(1, -1))


def _merge_kernel(h_ref, g0_ref, g1_ref, g2_ref, bg_ref, ya_ref, yb_ref, yc_ref, woa_ref, wob_ref, woc_ref,
                  wout_ref, gate_ref, ng_ref, sc_ref, sh_ref, wpq_ref, h1_ref, n2_ref, q_ref):
    d = h_ref.shape[1]
    merged = jnp.zeros(h_ref.shape, F32)
    for i, (g_ref, y_ref, w_ref) in enumerate(((g0_ref, ya_ref, woa_ref), (g1_ref, yb_ref, wob_ref),
                                               (g2_ref, yc_ref, woc_ref))):
        gate = jax.nn.sigmoid(g_ref[...].astype(F32) + bg_ref[:, i * d:(i + 1) * d])
        merged = merged + gate * jnp.dot(y_ref[...], w_ref[...], preferred_element_type=F32)
    y = jnp.dot(merged.astype(BF16), wout_ref[...], preferred_element_type=F32)
    h1 = h_ref[...] + gate_ref[0] * y
    h1_ref[...] = h1
    n = h1 * lax.rsqrt(jnp.mean(h1 * h1, axis=-1, keepdims=True) + EPS)
    n2 = (n * ng_ref[...] * (1.0 + sc_ref[0]) + sh_ref[0]).astype(BF16)
    n2_ref[...] = n2
    q_ref[...] = jnp.dot(n2, wpq_ref[...], preferred_element_type=F32).astype(BF16)


def _merge(h2, z, ya, yb, yc, b_gate, w_oa, w_ob, w_oc, w_out, gate1, norm_g, scale, shift, w_pq, t_len, tm=256):
    n_tok, d = h2.shape
    bpt = t_len // tm
    nq = w_pq.shape[1]
    row = lambda i: (i, 0)
    fixed = lambda i: (0, 0)
    per_b = lambda i: (i // bpt, 0, 0)
    bw = ya.shape[1]
    return pl.pallas_call(
        _merge_kernel,
        grid=(n_tok // tm,),
        in_specs=[pl.BlockSpec((tm, d), row),
                  pl.BlockSpec((tm, d), lambda i: (i, 0)),
                  pl.BlockSpec((tm, d), lambda i: (i, 1)),
                  pl.BlockSpec((tm, d), lambda i: (i, 2)),
                  pl.BlockSpec((1, N_BRANCH * d), fixed),
                  pl.BlockSpec((tm, bw), row), pl.BlockSpec((tm, bw), row), pl.BlockSpec((tm, bw), row),
                  pl.BlockSpec((bw, d), fixed), pl.BlockSpec((bw, d), fixed), pl.BlockSpec((bw, d), fixed),
                  pl.BlockSpec((d, d), fixed),
                  pl.BlockSpec((1, 1, d), per_b),
                  pl.BlockSpec((1, d), fixed),
                  pl.BlockSpec((1, 1, d), per_b), pl.BlockSpec((1, 1, d), per_b),
                  pl.BlockSpec((d, nq), fixed)],
        out_specs=[pl.BlockSpec((tm, d), row), pl.BlockSpec((tm, d), row), pl.BlockSpec((tm, nq), row)],
        out_shape=[jax.ShapeDtypeStruct((n_tok, d), F32), jax.ShapeDtypeStruct((n_tok, d), BF16),
                   jax.ShapeDtypeStruct((n_tok, nq), BF16)],
        compiler_params=pltpu.CompilerParams(
            dimension_semantics=("parallel",), vmem_limit_bytes=48 * 1024 * 1024),
        name="merge_outproj_peerq",
    )(h2, z, z, z, b_gate.reshape(1, -1), ya, yb, yc, w_oa.astype(BF16), w_ob.astype(BF16), w_oc.astype(BF16),
      w_out.astype(BF16), gate1, norm_g.reshape(1, d), scale, shift, w_pq.astype(BF16))


PEER_TB = 128


def _take_top(vals, tie_ids, payloads, k):
    neg_inf = jnp.float32(-jnp.inf)
    big = jnp.int32(2 ** 30)
    out_v, out_p = [], [[] for _ in payloads]
    for _ in range(k):
        m = jnp.max(vals, axis=0, keepdims=True)
        first = jnp.min(jnp.where(vals == m, tie_ids, big), axis=0, keepdims=True)
        hit = tie_ids == first
        out_v.append(m)
        for lst, p in zip(out_p, payloads):
            lst.append(jnp.sum(jnp.where(hit, p, 0), axis=0, keepdims=True))
        vals = jnp.where(hit, neg_inf, vals)
    return out_v, out_p


def _peer_topk_kernel(q_ref, sk_ref, eid_ref, g_ref):
    tb = q_ref.shape[0]
    hq = PEER_QDIM // 2
    key_id = lax.broadcasted_iota(jnp.int32, (PEER_NKEYS, tb), 0)
    k = PEER_TOPK
    n_j = [k if i == 0 else 8 for i in range(k)]
    pos = jnp.concatenate([i * k + lax.broadcasted_iota(jnp.int32, (n_j[i], tb), 0) for i in range(k)], axis=0)
    for h in range(PEER_HEADS):
        tops, topi = [], []
        for p in range(2):
            qhp = q_ref[:, (2 * h + p) * hq:(2 * h + p + 1) * hq]
            s = lax.dot_general(sk_ref[h, p], qhp, (((1,), (1,)), ((), ())), preferred_element_type=F32)
            v, (ids,) = _take_top(s, key_id, [key_id], k)
            tops.append(v)
            topi.append(ids)
        b_s = jnp.concatenate(tops[1], axis=0)
        b_i = jnp.concatenate(topi[1], axis=0)
        cand = jnp.concatenate([tops[0][i] + b_s[:n_j[i]] for i in range(k)], axis=0)
        cid = jnp.concatenate([topi[0][i] * PEER_NKEYS + b_i[:n_j[i]] for i in range(k)], axis=0)
        best, (eid,) = _take_top(cand, pos, [cid], k)
        best = jnp.concatenate(best, axis=0)
        e = jnp.exp(best - best[0:1])
        g_ref[0, h * k:(h + 1) * k, :] = e / jnp.sum(e, axis=0, keepdims=True)
        eid_ref[0, h * k:(h + 1) * k, :] = jnp.concatenate(eid, axis=0)


def _peer_topk(q, sub_keys):
    n_tok = q.shape[0]
    nblk = n_tok // PEER_TB
    slots = PEER_HEADS * PEER_TOPK
    return pl.pallas_call(
        _peer_topk_kernel,
        grid=(nblk,),
        in_specs=[pl.BlockSpec((PEER_TB, q.shape[1]), lambda i: (i, 0)),
                  pl.BlockSpec(sub_keys.shape, lambda i: (0, 0, 0, 0))],
        out_specs=[pl.BlockSpec((1, slots, PEER_TB), lambda i: (i, 0, 0)),
                   pl.BlockSpec((1, slots, PEER_TB), lambda i: (i, 0, 0))],
        out_shape=[jax.ShapeDtypeStruct((nblk, slots, PEER_TB), jnp.int32),
                   jax.ShapeDtypeStruct((nblk, slots, PEER_TB), F32)],
        compiler_params=pltpu.CompilerParams(dimension_semantics=("parallel",)),
        name="peer_topk",
    )(q, sub_keys.astype(BF16))


PEER_VMEM_LIMIT = 46 * 1024 * 1024
SUBLANE = 8


def _as_tiles(a2):
    return a2.reshape(a2.shape[0], SUBLANE, a2.shape[1] // SUBLANE)


def _pack_table(tab):
    bits = lax.bitcast_convert_type(tab.astype(BF16), jnp.uint16).astype(jnp.uint32)
    return _as_tiles(bits[0::2] | (bits[1::2] << 16))


def _load_table(tab_hbm, tab_vmem, sem):
    @pl.when(pl.program_id(0) == 0)
    def _():
        cp = pltpu.make_async_copy(tab_hbm, tab_vmem, sem)
        cp.start()
        cp.wait()


def _expert_row(tab_vmem, e):
    row = tab_vmem[e >> 1]
    sh = ((1 - (e & 1)) * 16).astype(jnp.uint32)
    bits = lax.shift_left(row, jnp.full(row.shape, sh, jnp.uint32)) & jnp.uint32(0xFFFF0000)
    return lax.bitcast_convert_type(bits, F32)


def _peer_u_kernel(idx_ref, x_ref, g_ref, tab_hbm, c_ref, tab_vmem, sem, act_scr):
    tb = x_ref.shape[0]
    slots = act_scr.shape[1]
    _load_table(tab_hbm, tab_vmem, sem)
    lane = lax.broadcasted_iota(jnp.int32, (1, slots), 1)

    def token(t, _):
        xr = x_ref[t]
        acc = jnp.zeros((1, slots), F32)
        for s in range(slots):
            e = idx_ref[0, 0, s * tb + t]
            prod = _expert_row(tab_vmem, e) * xr
            val = jnp.sum(jnp.sum(prod, axis=0, keepdims=True), axis=1, keepdims=True)
            acc = jnp.where(lane == s, val, acc)
        act_scr[pl.ds(t, 1), :] = acc
        return 0

    lax.fori_loop(0, tb, token, 0)
    act = act_scr[...]
    gelu = 0.5 * act * (1.0 + lax.erf(act * (2.0 ** -0.5)))
    c_ref[...] = jnp.transpose(g_ref[0]) * gelu


def _peer_u(eid, g, x3, tab_u):
    nblk, slots, tb = eid.shape
    n_tok = x3.shape[0]
    return pl.pallas_call(
        _peer_u_kernel,
        grid=(nblk,),
        in_specs=[pl.BlockSpec((1, 1, slots * tb), lambda i: (i, 0, 0), memory_space=pltpu.SMEM),
                  pl.BlockSpec((tb,) + x3.shape[1:], lambda i: (i, 0, 0)),
                  pl.BlockSpec((1, slots, tb), lambda i: (i, 0, 0)),
                  pl.BlockSpec(memory_space=pl.ANY)],
        out_specs=pl.BlockSpec((tb, slots), lambda i: (i, 0)),
        out_shape=jax.ShapeDtypeStruct((n_tok, slots), F32),
        scratch_shapes=[pltpu.VMEM(tab_u.shape, jnp.uint32), pltpu.SemaphoreType.DMA(()),
                        pltpu.VMEM((tb, slots), F32)],
        compiler_params=pltpu.CompilerParams(
            dimension_semantics=("arbitrary",), vmem_limit_bytes=PEER_VMEM_LIMIT),
        name="peer_u",
    )(eid.reshape(nblk, 1, slots * tb), x3, g, tab_u)


def _peer_v_kernel(idx_ref, c_ref, tab_hbm, o_ref, tab_vmem, sem, *, slots):
    tb = o_ref.shape[0]
    _load_table(tab_hbm, tab_vmem, sem)
    n_acc = 4

    def token(t, _):
        accs = [jnp.zeros(o_ref.shape[1:], F32) for _ in range(n_acc)]
        for s in range(slots):
            e = idx_ref[0, 0, s * tb + t]
            coef = c_ref[0, 0, t * slots + s]
            accs[s % n_acc] = accs[s % n_acc] + coef * _expert_row(tab_vmem, e)
        o_ref[t] = (accs[0] + accs[1]) + (accs[2] + accs[3])
        return 0

    lax.fori_loop(0, tb, token, 0)


def _peer_v(eid, c, tab_v):
    nblk, slots, tb = eid.shape
    n_tok = c.shape[0]
    tile = tab_v.shape[1:]
    smem_blk = pl.BlockSpec((1, 1, slots * tb), lambda i: (i, 0, 0), memory_space=pltpu.SMEM)
    return pl.pallas_call(
        functools.partial(_peer_v_kernel, slots=slots),
        grid=(nblk,),
        in_specs=[smem_blk, smem_blk, pl.BlockSpec(memory_space=pl.ANY)],
        out_specs=pl.BlockSpec((tb,) + tile, lambda i: (i, 0, 0)),
        out_shape=jax.ShapeDtypeStruct((n_tok,) + tile, F32),
        scratch_shapes=[pltpu.VMEM(tab_v.shape, jnp.uint32), pltpu.SemaphoreType.DMA(())],
        compiler_params=pltpu.CompilerParams(
            dimension_semantics=("arbitrary",), vmem_limit_bytes=PEER_VMEM_LIMIT),
        name="peer_v",
    )(eid.reshape(nblk, 1, slots * tb), c.reshape(nblk, 1, slots * tb), tab_v)


def _residual_kernel(h_ref, p_ref, gate_ref, o_ref):
    o_ref[...] = h_ref[...] + gate_ref[0] * p_ref[...]


def _residual(h2, p2, gate, t_len, tm=512):
    n_tok, d = h2.shape
    bpt = t_len // tm
    return pl.pallas_call(
        _residual_kernel,
        grid=(n_tok // tm,),
        in_specs=[pl.BlockSpec((tm, d), lambda i: (i, 0)), pl.BlockSpec((tm, d), lambda i: (i, 0)),
                  pl.BlockSpec((1, 1, d), lambda i: (i // bpt, 0, 0))],
        out_specs=pl.BlockSpec((tm, d), lambda i: (i, 0)),
        out_shape=jax.ShapeDtypeStruct((n_tok, d), F32),
        compiler_params=pltpu.CompilerParams(dimension_semantics=("parallel",)),
        name="peer_residual",
    )(h2, p2, gate)


def _layer(h2, mod, t_len, p):
    n_tok, d = h2.shape
    bsz = n_tok // t_len
    z = _inproj(h2, p["norm1_g"], mod[:, 1], mod[:, 0], _reorder_w_in(p["w_in"]), t_len)
    z3 = z.reshape(bsz, t_len, Z_COLS)
    ya = _dsa(z3, p["a_qn_g"], p["a_kn_g"]).reshape(n_tok, -1)
    yb = _stickbreak(z3, p["b_qn_g"], p["b_kn_g"]).reshape(n_tok, -1)
    yc = _gla(z3, p["c_w_alpha"], p["c_b_alpha"], p["c_on_g"]).reshape(n_tok, -1)
    h1, n2, q = _merge(h2, z, ya, yb, yc, p["b_gate"], p["w_oa"], p["w_ob"], p["w_oc"], p["w_out"],
                       mod[:, 2], p["norm2_g"], mod[:, 4], mod[:, 3], p["peer_wq"], t_len)
    eid, g = _peer_topk(q, p["peer_subkeys"])
    x3 = _as_tiles(n2.astype(F32))
    c = _peer_u(eid, g, x3, _pack_table(p["peer_u"]))
    out = _peer_v(eid, c, _pack_table(p["peer_v"]))
    return _residual(h1, out.reshape(n_tok, d), mod[:, 5], t_len)


def kernel(x, c, norm1_g, norm2_g, w_mod, b_mod, w_in, b_gate, a_qn_g, a_kn_g, b_qn_g, b_kn_g, c_w_alpha, c_b_alpha, c_on_g, w_oa, w_ob, w_oc, w_out, peer_wq, peer_subkeys, peer_u, peer_v):
    bsz, t_len, d = x.shape
    params = dict(norm1_g=norm1_g, norm2_g=norm2_g, w_in=w_in, b_gate=b_gate, a_qn_g=a_qn_g, a_kn_g=a_kn_g,
                  b_qn_g=b_qn_g, b_kn_g=b_kn_g, c_w_alpha=c_w_alpha, c_b_alpha=c_b_alpha, c_on_g=c_on_g,
                  w_oa=w_oa, w_ob=w_ob, w_oc=w_oc, w_out=w_out, peer_wq=peer_wq, peer_subkeys=peer_subkeys,
                  peer_u=peer_u, peer_v=peer_v)
    h2 = x.reshape(bsz * t_len, d)
    for l in range(w_in.shape[0]):
        mod = _mod(c, w_mod[l], b_mod[l]).reshape(bsz, 6, 1, d)
        h2 = _layer(h2, mod, t_len, {k: v[l] for k, v in params.items()})
    return h2.reshape(bsz, t_len, d)
```

```python
import functools

import jax
import jax.numpy as jnp
import numpy as np
from jax import lax
from jax.experimental import pallas as pl
from jax.experimental.pallas import tpu as pltpu

F32 = jnp.float32
BF16 = jnp.bfloat16

D_MODEL = 1024
EPS = 1e-6
CHUNK = 64
TOPK_MAX = 256
A_HEADS, A_DIM = 8, 64
IDX_HEADS, IDX_DIM = 4, 64
B_HEADS, B_DIM = 8, 64
C_HEADS, C_KEY, C_VAL, C_RANK, C_TAU = 4, 64, 128, 16, 16.0
PEER_HEADS, PEER_NKEYS, PEER_QDIM, PEER_TOPK = 8, 128, 256, 16
N_BRANCH = 3

LANE = 128

_SEGMENTS = (
    ("gl", 3072), ("aq", 512), ("bq", 512), ("bk", 512), ("bv", 512), ("cv", 512), ("cr", 512),
    ("iq", 256), ("cq", 256), ("ck", 256), ("akv", 128), ("ikw", 128), ("ca", 128),
)
COL = {}
_off = 0
for _name, _w in _SEGMENTS:
    COL[_name] = (_off, _w)
    _off += _w
Z_COLS = _off
Z_TILE = Z_COLS // 3

_REF_WIDTHS = (512, 64, 64, 256, 64, 4, 512, 512, 512, 256, 256, 512, 512, 16, 3072)


def _reorder_w_in(w_in):
    offs = np.cumsum((0,) + _REF_WIDTHS)
    p = [w_in[:, offs[i]:offs[i + 1]] for i in range(len(_REF_WIDTHS))]
    aq, ak, av, iq, ik, iw, bq, bk, bv, cq, ck, cv, cr, ca, gl = p
    d = w_in.shape[0]
    z = lambda n: jnp.zeros((d, n), w_in.dtype)
    cols = [gl, aq, bq, bk, bv, cv, cr, iq, cq, ck, ak, av, ik, iw, z(60), ca, z(112)]
    return jnp.concatenate(cols, axis=1).astype(BF16)


def _cb(name):
    off, w = COL[name]
    return off // w


def _mod_kernel(c_ref, w_ref, b_ref, o_ref):
    c = c_ref[...]
    ca = (c * jax.nn.sigmoid(c)).astype(BF16)
    o_ref[...] = jnp.dot(ca, w_ref[...].astype(BF16), preferred_element_type=F32) + b_ref[...]


def _mod(c, w_mod, b_mod):
    bsz, d = c.shape
    n = w_mod.shape[1]
    tn = 768
    return pl.pallas_call(
        _mod_kernel,
        grid=(n // tn,),
        in_specs=[pl.BlockSpec((bsz, d), lambda j: (0, 0)),
                  pl.BlockSpec((d, tn), lambda j: (0, j)),
                  pl.BlockSpec((1, tn), lambda j: (0, j))],
        out_specs=pl.BlockSpec((bsz, tn), lambda j: (0, j)),
        out_shape=jax.ShapeDtypeStruct((bsz, n), F32),
        name="adaln_mod",
    )(c, w_mod, b_mod.reshape(1, n))


def _inproj_kernel(h_ref, g_ref, sc_ref, sh_ref, w_ref, z_ref, n_scr):
    @pl.when(pl.program_id(1) == 0)
    def _():
        x = h_ref[...]
        y = x * lax.rsqrt(jnp.mean(x * x, axis=-1, keepdims=True) + EPS)
        n = y * g_ref[...] * (1.0 + sc_ref[0]) + sh_ref[0]
        n_scr[...] = n.astype(BF16)

    z_ref[...] = jnp.dot(n_scr[...], w_ref[...], preferred_element_type=F32).astype(BF16)


def _inproj(h2, norm_g, scale, shift, w_z, t_len, tm=512):
    n_tok, d = h2.shape
    tm = min(tm, t_len)
    bpt = t_len // tm
    return pl.pallas_call(
        _inproj_kernel,
        grid=(n_tok // tm, Z_COLS // Z_TILE),
        in_specs=[pl.BlockSpec((tm, d), lambda i, j: (i, 0)),
                  pl.BlockSpec((1, d), lambda i, j: (0, 0)),
                  pl.BlockSpec((1, 1, d), lambda i, j: (i // bpt, 0, 0)),
                  pl.BlockSpec((1, 1, d), lambda i, j: (i // bpt, 0, 0)),
                  pl.BlockSpec((d, Z_TILE), lambda i, j: (0, j))],
        out_specs=pl.BlockSpec((tm, Z_TILE), lambda i, j: (i, j)),
        out_shape=jax.ShapeDtypeStruct((n_tok, Z_COLS), BF16),
        scratch_shapes=[pltpu.VMEM((tm, d), BF16)],
        compiler_params=pltpu.CompilerParams(
            dimension_semantics=("parallel", "arbitrary"), vmem_limit_bytes=48 * 1024 * 1024),
        name="norm_inproj",
    )(h2, norm_g.reshape(1, d), scale, shift, w_z)


def _head_rms(x, gain):
    return x * lax.rsqrt(jnp.mean(x * x, axis=-1, keepdims=True) + EPS) * gain


DSA_QB = 128
DSA_GROUP = 4
DSA_BISECT = 30


def _dsa_kernel(aq_ref, akv_ref, iq_ref, ikwq_ref, ikwk_ref, qg_ref, kg_ref, o_ref, sm_scr, sel_scr,
                *, q_off, n_sel):
    qb = aq_ref.shape[1]
    tk = akv_ref.shape[1]
    q0 = (q_off + pl.program_id(1)) * qb
    t_pos = q0 + lax.broadcasted_iota(jnp.int32, (qb, 1), 0)
    limit = (t_pos // CHUNK + 1) * CHUNK
    s_pos = lax.broadcasted_iota(jnp.int32, (1, tk), 1)
    adm = s_pos < limit

    kidx = ikwk_ref[0][:, :IDX_DIM]
    qidx = iq_ref[0]
    w = ikwq_ref[0][:, IDX_DIM:IDX_DIM + IDX_HEADS].astype(F32)
    score = jnp.zeros((qb, tk), F32)
    for h in range(IDX_HEADS):
        rel = lax.dot_general(qidx[:, h * IDX_DIM:(h + 1) * IDX_DIM], kidx,
                              (((1,), (1,)), ((), ())), preferred_element_type=F32)
        score = score + w[:, h:h + 1] * jnp.maximum(rel, 0.0)
    neg_inf = jnp.float32(-jnp.inf)
    sm_scr[...] = jnp.where(adm, score, neg_inf)

    hi0 = jnp.max(sm_scr[...], axis=-1, keepdims=True)
    lo0 = jnp.min(jnp.where(adm, score, jnp.float32(jnp.inf)), axis=-1, keepdims=True)
    kf = jnp.float32(n_sel)

    def bisect(_, carry):
        lo, hi = carry
        mid = 0.5 * (lo + hi)
        cnt = jnp.sum(jnp.where(sm_scr[...] > mid, 1.0, 0.0), axis=-1, keepdims=True)
        below = cnt < kf
        return jnp.where(below, lo, mid), jnp.where(below, mid, hi)

    _, hi = lax.fori_loop(0, DSA_BISECT, bisect, (lo0, hi0))
    sm = sm_scr[...]
    v1 = jnp.max(jnp.where(sm <= hi, sm, neg_inf), axis=-1, keepdims=True)
    c_ge1 = jnp.sum(jnp.where(sm >= v1, 1.0, 0.0), axis=-1, keepdims=True)
    v2 = jnp.max(jnp.where(sm < v1, sm, neg_inf), axis=-1, keepdims=True)
    thr = jnp.where(c_ge1 >= kf, v1, v2)
    need = kf - jnp.sum(jnp.where(sm > thr, 1.0, 0.0), axis=-1, keepdims=True)

    r_i = lax.broadcasted_iota(jnp.int32, (LANE, LANE), 0)
    c_i = lax.broadcasted_iota(jnp.int32, (LANE, LANE), 1)
    incl = jnp.where(r_i <= c_i, 1.0, 0.0).astype(BF16)
    all_sel = limit <= n_sel
    run = jnp.zeros((qb, 1), F32)
    for j in range(tk // LANE):
        sl = slice(j * LANE, (j + 1) * LANE)
        smj = sm_scr[:, sl]
        eqj = jnp.where(smj == thr, 1.0, 0.0)
        inc = jnp.dot(eqj.astype(BF16), incl, preferred_element_type=F32)
        rank = inc - eqj + run
        picked = (smj > thr) | ((smj == thr) & (rank < need))
        admj = (j * LANE + lax.broadcasted_iota(jnp.int32, (1, LANE), 1)) < limit
        selj = (all_sel & admj) | (jnp.logical_not(all_sel) & picked)
        sel_scr[:, sl] = jnp.where(selj, 0.0, neg_inf)
        run = run + inc[:, LANE - 1:LANE]

    kv = akv_ref[0].astype(F32)
    k = _head_rms(kv[:, :A_DIM], kg_ref[...]).astype(BF16)
    v = akv_ref[0][:, A_DIM:2 * A_DIM]
    aq = aq_ref[0].astype(F32)
    scale = A_DIM ** -0.5
    for h in range(A_HEADS):
        qh = _head_rms(aq[:, h * A_DIM:(h + 1) * A_DIM], qg_ref[...]).astype(BF16)
        logits = lax.dot_general(qh, k, (((1,), (1,)), ((), ())), preferred_element_type=F32) * scale
        logits = logits + sel_scr[...]
        m = jnp.max(logits, axis=-1, keepdims=True)
        p = jnp.exp(logits - m)
        denom = jnp.sum(p, axis=-1, keepdims=True)
        oh = jnp.dot(p.astype(BF16), v, preferred_element_type=F32) / denom
        o_ref[0, :, h * A_DIM:(h + 1) * A_DIM] = oh.astype(o_ref.dtype)


def _dsa(z3, qg, kg):
    bsz, t_len, _ = z3.shape
    n_sel = min(TOPK_MAX, t_len // 4)
    span = DSA_QB * DSA_GROUP
    outs = []
    for g in range(t_len // span):
        tk = span * (g + 1)
        q_off = g * DSA_GROUP
        qmap = lambda b, i, c: (b, q_off + i, c)
        outs.append(pl.pallas_call(
            functools.partial(_dsa_kernel, q_off=q_off, n_sel=n_sel),
            grid=(bsz, DSA_GROUP),
            in_specs=[pl.BlockSpec((1, DSA_QB, 512), lambda b, i, q_off=q_off: (b, q_off + i, _cb("aq"))),
                      pl.BlockSpec((1, tk, LANE), lambda b, i: (b, 0, _cb("akv"))),
                      pl.BlockSpec((1, DSA_QB, 256), lambda b, i, q_off=q_off: (b, q_off + i, _cb("iq"))),
                      pl.BlockSpec((1, DSA_QB, LANE), lambda b, i, q_off=q_off: (b, q_off + i, _cb("ikw"))),
                      pl.BlockSpec((1, tk, LANE), lambda b, i: (b, 0, _cb("ikw"))),
                      pl.BlockSpec((1, A_DIM), lambda b, i: (0, 0)),
                      pl.BlockSpec((1, A_DIM), lambda b, i: (0, 0))],
            out_specs=pl.BlockSpec((1, DSA_QB, 512), lambda b, i: (b, i, 0)),
            out_shape=jax.ShapeDtypeStruct((bsz, span, 512), BF16),
            scratch_shapes=[pltpu.VMEM((DSA_QB, tk), F32), pltpu.VMEM((DSA_QB, tk), F32)],
            compiler_params=pltpu.CompilerParams(
                dimension_semantics=("parallel", "arbitrary"), vmem_limit_bytes=48 * 1024 * 1024),
            name=f"dsa_attention_g{g}",
        )(z3, z3, z3, z3, z3, qg.reshape(1, A_DIM), kg.reshape(1, A_DIM)))
    return jnp.concatenate(outs, axis=1)


def _split_dot(x, m, x_is_lhs=True):
    hi = x.astype(BF16)
    lo = (x - hi.astype(F32)).astype(BF16)
    if x_is_lhs:
        return jnp.dot(hi, m, preferred_element_type=F32) + jnp.dot(lo, m, preferred_element_type=F32)
    return jnp.dot(m, hi, preferred_element_type=F32) + jnp.dot(m, lo, preferred_element_type=F32)


SB_QB = 128
SB_SWEEP = 4


def _sb_kernel(q_ref, k_ref, v_ref, qg_ref, kg_ref, o_ref, kn_scr):
    qi = pl.program_id(2)
    qb = q_ref.shape[1]
    hd = B_DIM

    @pl.when(qi == 0)
    def _():
        kf = k_ref[0].astype(F32)
        for h in range(2):
            kn_scr[h] = _head_rms(kf[:, h * hd:(h + 1) * hd], kg_ref[...]).astype(BF16)

    q = q_ref[0].astype(F32)
    qn = [_head_rms(q[:, h * hd:(h + 1) * hd], qg_ref[...]).astype(BF16) for h in range(2)]
    r_i = lax.broadcasted_iota(jnp.int32, (qb, qb), 0)
    c_i = lax.broadcasted_iota(jnp.int32, (qb, qb), 1)
    m_incl = jnp.where(r_i >= c_i, 1.0, 0.0).astype(BF16)
    t_pos = qi * qb + lax.broadcasted_iota(jnp.int32, (qb, 1), 0)
    lane = lax.broadcasted_iota(jnp.int32, (1, 2 * hd), 1)
    scale = hd ** -0.5

    def body(i, carry):
        sb = qi // SB_SWEEP - i
        carry = list(carry)
        blocks = list(reversed(range(SB_SWEEP)))
        starts = [pl.multiple_of((sb * SB_SWEEP + j) * qb, qb) for j in blocks]
        stricts = [(ks + lax.broadcasted_iota(jnp.int32, (1, qb), 1)) < t_pos for ks in starts]
        zs = [[lax.dot_general(qn[h], kn_scr[h, pl.ds(ks, qb), :], (((1,), (1,)), ((), ())),
                               preferred_element_type=F32) * scale for h in range(2)] for ks in starts]
        lbs = [[jnp.minimum(z, 0.0) - jnp.log(1.0 + jnp.exp(-jnp.abs(z))) for z in zb] for zb in zs]
        lks = [[jnp.where(st, lb - z, 0.0) for lb, z in zip(lbb, zb)]
               for st, lbb, zb in zip(stricts, lbs, zs)]
        incs = [[_split_dot(lk, m_incl) for lk in lkb] for lkb in lks]
        probs = []
        for st, lbb, lkb, incb in zip(stricts, lbs, lks, incs):
            row = []
            for h in range(2):
                later = incb[h] - lkb[h] + carry[2 * h]
                row.append(jnp.where(st, jnp.exp(lbb[h] + later), 0.0).astype(BF16))
                carry[2 * h] = carry[2 * h] + incb[h][:, 0:1]
            probs.append(row)
        for ks, row in zip(starts, probs):
            vblk = v_ref[0, pl.ds(ks, qb), :]
            for h in range(2):
                carry[2 * h + 1] = carry[2 * h + 1] + jnp.dot(row[h], vblk, preferred_element_type=F32)
        return tuple(carry)

    init = (jnp.zeros((qb, 1), F32), jnp.zeros((qb, 2 * hd), F32)) * 2
    res = lax.fori_loop(0, qi // SB_SWEEP + 1, body, init)
    o_ref[0] = jnp.where(lane < hd, res[1], res[3]).astype(o_ref.dtype)


def _stickbreak(z3, qg, kg):
    bsz, t_len, _ = z3.shape
    nq = t_len // SB_QB
    npair = B_HEADS // 2
    cq, ck, cv = COL["bq"][0] // LANE, COL["bk"][0] // LANE, COL["bv"][0] // LANE
    return pl.pallas_call(
        _sb_kernel,
        grid=(bsz, npair, nq),
        in_specs=[pl.BlockSpec((1, SB_QB, LANE), lambda b, p, i: (b, i, cq + p)),
                  pl.BlockSpec((1, t_len, LANE), lambda b, p, i: (b, 0, ck + p)),
                  pl.BlockSpec((1, t_len, LANE), lambda b, p, i: (b, 0, cv + p)),
                  pl.BlockSpec((1, B_DIM), lambda b, p, i: (0, 0)),
                  pl.BlockSpec((1, B_DIM), lambda b, p, i: (0, 0))],
        out_specs=pl.BlockSpec((1, SB_QB, LANE), lambda b, p, i: (b, i, p)),
        out_shape=jax.ShapeDtypeStruct((bsz, t_len, B_HEADS * B_DIM), BF16),
        scratch_shapes=[pltpu.VMEM((2, t_len, B_DIM), BF16)],
        compiler_params=pltpu.CompilerParams(
            dimension_semantics=("parallel", "parallel", "arbitrary"), vmem_limit_bytes=48 * 1024 * 1024),
        name="stickbreak_attention",
    )(z3, z3, z3, qg.reshape(1, B_DIM), kg.reshape(1, B_DIM))


GLA_SUB = 16


def _gla_kernel(q_ref, k_ref, v_ref, r_ref, a_ref, wa_ref, ba_ref, og_ref, o_ref, st_scr):
    t_len = q_ref.shape[1]
    nc = t_len // CHUNK
    dk, dv, sub = C_KEY, C_VAL, GLA_SUB
    r_i = lax.broadcasted_iota(jnp.int32, (CHUNK, CHUNK), 0)
    c_i = lax.broadcasted_iota(jnp.int32, (CHUNK, CHUNK), 1)
    l_incl = jnp.where(r_i >= c_i, 1.0, 0.0).astype(BF16)
    st_scr[...] = jnp.zeros_like(st_scr)
    neg_inf = jnp.float32(-jnp.inf)
    pr = lax.broadcasted_iota(jnp.int32, (sub * sub, 1), 0)
    pair_ok = pr % sub <= pr // sub
    fr = lax.broadcasted_iota(jnp.int32, (sub, sub * sub), 0)
    fc = lax.broadcasted_iota(jnp.int32, (sub, sub * sub), 1)
    fold_t = jnp.where(fc // sub == fr, 1.0, 0.0).astype(BF16)
    hr = lax.broadcasted_iota(jnp.int32, (2 * dk, 2 * dv), 0)
    hc = lax.broadcasted_iota(jnp.int32, (2 * dk, 2 * dv), 1)
    head_ones = jnp.where(hr // dk == hc // dv, 1.0, 0.0).astype(BF16)

    def per_t(x):
        return jnp.broadcast_to(x[:, None, :], (sub, sub, x.shape[1])).reshape(sub * sub, x.shape[1])

    def per_s(x):
        return jnp.broadcast_to(x[None, :, :], (sub, sub, x.shape[1])).reshape(sub * sub, x.shape[1])

    def chunk(c, _):
        rows = pl.ds(pl.multiple_of(c * CHUNK, CHUNK), CHUNK)
        pre = jnp.dot(a_ref[0, rows, :], wa_ref[...], preferred_element_type=F32) + ba_ref[...]
        la2 = (jnp.minimum(pre, 0.0) - jnp.log(1.0 + jnp.exp(-jnp.abs(pre)))) * (1.0 / C_TAU)
        b2 = _split_dot(la2, l_incl, x_is_lhs=False)
        q2 = q_ref[0, rows, :].astype(F32) * (dk ** -0.5)
        k2 = k_ref[0, rows, :].astype(F32)
        v2f = v_ref[0, rows, :].astype(F32)

        diag = []
        for i in range(CHUNK // sub):
            sl = slice(i * sub, (i + 1) * sub)
            bt, qt = (per_t(x[sl]) for x in (b2, q2))
            bs, ks_, vs = (per_s(x[sl]) for x in (b2, k2, v2f))
            e = jnp.exp(jnp.where(pair_ok, bt - bs, neg_inf))
            w2 = _split_dot(qt * ks_ * e, head_ones)
            diag.append(jnp.dot(fold_t, (w2 * vs).astype(BF16), preferred_element_type=F32))

        for h in range(2):
            b = b2[:, h * dk:(h + 1) * dk]
            q = q2[:, h * dk:(h + 1) * dk]
            k = k2[:, h * dk:(h + 1) * dk]
            v = v_ref[0, rows, h * dv:(h + 1) * dv]
            st = st_scr[h]
            o = lax.dot_general((q * jnp.exp(b)).astype(BF16), st.astype(BF16),
                                (((1,), (1,)), ((), ())), preferred_element_type=F32)
            intra = []
            for i in range(CHUNK // sub):
                lo_r = i * sub
                bi, qi_ = b[lo_r:lo_r + sub], q[lo_r:lo_r + sub]
                oi = diag[i][:, h * dv:(h + 1) * dv]
                if i > 0:
                    ref_row = b[lo_r - 1:lo_r, :]
                    qs = (qi_ * jnp.exp(bi - ref_row)).astype(BF16)
                    ks = (k[:lo_r] * jnp.exp(ref_row - b[:lo_r])).astype(BF16)
                    att = lax.dot_general(qs, ks, (((1,), (1,)), ((), ())), preferred_element_type=F32)
                    oi = oi + jnp.dot(att.astype(BF16), v[:lo_r], preferred_element_type=F32)
                intra.append(oi)
            o = o + jnp.concatenate(intra, axis=0)
            b_last = b[CHUNK - 1:CHUNK, :]
            kd = (k * jnp.exp(b_last - b)).astype(BF16)
            upd = lax.dot_general(v, kd, (((0,), (0,)), ((), ())), preferred_element_type=F32)
            st_scr[h] = st * jnp.exp(b_last) + upd
            y = o * lax.rsqrt(jnp.mean(o * o, axis=-1, keepdims=True) + EPS) * og_ref[:, h * dv:(h + 1) * dv]
            r = r_ref[0, rows, h * dv:(h + 1) * dv].astype(F32)
            o_ref[0, rows, h * dv:(h + 1) * dv] = (y * (r * jax.nn.sigmoid(r))).astype(o_ref.dtype)
        return 0

    lax.fori_loop(0, nc, chunk, 0)


def _gla(z3, w_alpha, b_alpha, out_gain):
    bsz, t_len, _ = z3.shape
    npair = C_HEADS // 2
    wa = jnp.zeros((LANE, C_HEADS * C_KEY), F32).at[:C_RANK].set(w_alpha).astype(BF16)
    cq, ck = COL["cq"][0] // LANE, COL["ck"][0] // LANE
    cv, cr, ca = COL["cv"][0] // 256, COL["cr"][0] // 256, COL["ca"][0] // LANE
    return pl.pallas_call(
        _gla_kernel,
        grid=(bsz, npair),
        in_specs=[pl.BlockSpec((1, t_len, LANE), lambda b, p: (b, 0, cq + p)),
                  pl.BlockSpec((1, t_len, LANE), lambda b, p: (b, 0, ck + p)),
                  pl.BlockSpec((1, t_len, 256), lambda b, p: (b, 0, cv + p)),
                  pl.BlockSpec((1, t_len, 256), lambda b, p: (b, 0, cr + p)),
                  pl.BlockSpec((1, t_len, LANE), lambda b, p: (b, 0, ca)),
                  pl.BlockSpec((LANE, LANE), lambda b, p: (0, p)),
                  pl.BlockSpec((1, LANE), lambda b, p: (0, p)),
                  pl.BlockSpec((1, 256), lambda b, p: (0, p))],
        out_specs=pl.BlockSpec((1, t_len, 256), lambda b, p: (b, 0, p)),
        out_shape=jax.ShapeDtypeStruct((bsz, t_len, C_HEADS * C_VAL), BF16),
        scratch_shapes=[pltpu.VMEM((2, C_VAL, C_KEY), F32)],
        compiler_params=pltpu.CompilerParams(
            dimension_semantics=("parallel", "parallel"), vmem_limit_bytes=48 * 1024 * 1024),
        name="gla_attention",
    )(z3, z3, z3, z3, z3, wa, b_alpha.reshape(1, -1), out_gain.reshape(1, -1))


def _merge_kernel(h_ref, g0_ref, g1_ref, g2_ref, bg_ref, ya_ref, yb_ref, yc_ref, woa_ref, wob_ref, woc_ref,
                  wout_ref, gate_ref, ng_ref, sc_ref, sh_ref, wpq_ref, h1_ref, n2_ref, q_ref):
    d = h_ref.shape[1]
    merged = jnp.zeros(h_ref.shape, F32)
    for i, (g_ref, y_ref, w_ref) in enumerate(((g0_ref, ya_ref, woa_ref), (g1_ref, yb_ref, wob_ref),
                                               (g2_ref, yc_ref, woc_ref))):
        gate = jax.nn.sigmoid(g_ref[...].astype(F32) + bg_ref[:, i * d:(i + 1) * d])
        merged = merged + gate * jnp.dot(y_ref[...], w_ref[...], preferred_element_type=F32)
    y = jnp.dot(merged.astype(BF16), wout_ref[...], preferred_element_type=F32)
    h1 = h_ref[...] + gate_ref[0] * y
    h1_ref[...] = h1
    n = h1 * lax.rsqrt(jnp.mean(h1 * h1, axis=-1, keepdims=True) + EPS)
    n2 = (n * ng_ref[...] * (1.0 + sc_ref[0]) + sh_ref[0]).astype(BF16)
    n2_ref[...] = n2
    q_ref[...] = jnp.dot(n2, wpq_ref[...], preferred_element_type=F32).astype(BF16)


def _merge(h2, z, ya, yb, yc, b_gate, w_oa, w_ob, w_oc, w_out, gate1, norm_g, scale, shift, w_pq, t_len, tm=256):
    n_tok, d = h2.shape
    bpt = t_len // tm
    nq = w_pq.shape[1]
    row = lambda i: (i, 0)
    fixed = lambda i: (0, 0)
    per_b = lambda i: (i // bpt, 0, 0)
    bw = ya.shape[1]
    return pl.pallas_call(
        _merge_kernel,
        grid=(n_tok // tm,),
        in_specs=[pl.BlockSpec((tm, d), row),
                  pl.BlockSpec((tm, d), lambda i: (i, 0)),
                  pl.BlockSpec((tm, d), lambda i: (i, 1)),
                  pl.BlockSpec((tm, d), lambda i: (i, 2)),
                  pl.BlockSpec((1, N_BRANCH * d), fixed),
                  pl.BlockSpec((tm, bw), row), pl.BlockSpec((tm, bw), row), pl.BlockSpec((tm, bw), row),
                  pl.BlockSpec((bw, d), fixed), pl.BlockSpec((bw, d), fixed), pl.BlockSpec((bw, d), fixed),
                  pl.BlockSpec((d, d), fixed),
                  pl.BlockSpec((1, 1, d), per_b),
                  pl.BlockSpec((1, d), fixed),
                  pl.BlockSpec((1, 1, d), per_b), pl.BlockSpec((1, 1, d), per_b),
                  pl.BlockSpec((d, nq), fixed)],
        out_specs=[pl.BlockSpec((tm, d), row), pl.BlockSpec((tm, d), row), pl.BlockSpec((tm, nq), row)],
        out_shape=[jax.ShapeDtypeStruct((n_tok, d), F32), jax.ShapeDtypeStruct((n_tok, d), BF16),
                   jax.ShapeDtypeStruct((n_tok, nq), BF16)],
        compiler_params=pltpu.CompilerParams(
            dimension_semantics=("parallel",), vmem_limit_bytes=48 * 1024 * 1024),
        name="merge_outproj_peerq",
    )(h2, z, z, z, b_gate.reshape(1, -1), ya, yb, yc, w_oa.astype(BF16), w_ob.astype(BF16), w_oc.astype(BF16),
      w_out.astype(BF16), gate1, norm_g.reshape(1, d), scale, shift, w_pq.astype(BF16))


PEER_TB = 128


def _take_top(vals, tie_ids, payloads, k):
    neg_inf = jnp.float32(-jnp.inf)
    big = jnp.int32(2 ** 30)
    out_v, out_p = [], [[] for _ in payloads]
    for _ in range(k):
        m = jnp.max(vals, axis=0, keepdims=True)
        first = jnp.min(jnp.where(vals == m, tie_ids, big), axis=0, keepdims=True)
        hit = tie_ids == first
        out_v.append(m)
        for lst, p in zip(out_p, payloads):
            lst.append(jnp.sum(jnp.where(hit, p, 0), axis=0, keepdims=True))
        vals = jnp.where(hit, neg_inf, vals)
    return out_v, out_p


def _peer_topk_kernel(q_ref, sk_ref, eid_ref, g_ref):
    tb = q_ref.shape[0]
    hq = PEER_QDIM // 2
    key_id = lax.broadcasted_iota(jnp.int32, (PEER_NKEYS, tb), 0)
    k = PEER_TOPK
    n_j = [k if i == 0 else 8 for i in range(k)]
    pos = jnp.concatenate([i * k + lax.broadcasted_iota(jnp.int32, (n_j[i], tb), 0) for i in range(k)], axis=0)
    for h in range(PEER_HEADS):
        tops, topi = [], []
        for p in range(2):
            qhp = q_ref[:, (2 * h + p) * hq:(2 * h + p + 1) * hq]
            s = lax.dot_general(sk_ref[h, p], qhp, (((1,), (1,)), ((), ())), preferred_element_type=F32)
            v, (ids,) = _take_top(s, key_id, [key_id], k)
            tops.append(v)
            topi.append(ids)
        b_s = jnp.concatenate(tops[1], axis=0)
        b_i = jnp.concatenate(topi[1], axis=0)
        cand = jnp.concatenate([tops[0][i] + b_s[:n_j[i]] for i in range(k)], axis=0)
        cid = jnp.concatenate([topi[0][i] * PEER_NKEYS + b_i[:n_j[i]] for i in range(k)], axis=0)
        best, (eid,) = _take_top(cand, pos, [cid], k)
        best = jnp.concatenate(best, axis=0)
        e = jnp.exp(best - best[0:1])
        g_ref[0, h * k:(h + 1) * k, :] = e / jnp.sum(e, axis=0, keepdims=True)
        eid_ref[0, h * k:(h + 1) * k, :] = jnp.concatenate(eid, axis=0)


def _peer_topk(q, sub_keys):
    n_tok = q.shape[0]
    nblk = n_tok // PEER_TB
    slots = PEER_HEADS * PEER_TOPK
    return pl.pallas_call(
        _peer_topk_kernel,
        grid=(nblk,),
        in_specs=[pl.BlockSpec((PEER_TB, q.shape[1]), lambda i: (i, 0)),
                  pl.BlockSpec(sub_keys.shape, lambda i: (0, 0, 0, 0))],
        out_specs=[pl.BlockSpec((1, slots, PEER_TB), lambda i: (i, 0, 0)),
                   pl.BlockSpec((1, slots, PEER_TB), lambda i: (i, 0, 0))],
        out_shape=[jax.ShapeDtypeStruct((nblk, slots, PEER_TB), jnp.int32),
                   jax.ShapeDtypeStruct((nblk, slots, PEER_TB), F32)],
        compiler_params=pltpu.CompilerParams(dimension_semantics=("parallel",)),
        name="peer_topk",
    )(q, sub_keys.astype(BF16))


PEER_VMEM_LIMIT = 50 * 1024 * 1024
SUBLANE = 8
PACK = 2 * SUBLANE
PEER_GRP = LANE // PACK
PEER_TOK_UNROLL = 4


def _pair_tiles(tab):
    e, d = tab.shape
    t = tab.astype(BF16).reshape(e // 2, 2, d // LANE, LANE)
    return jnp.swapaxes(t, 1, 2).reshape(e // 2, PACK, LANE)


def _dup_tiles(x2):
    n, d = x2.shape
    x4 = x2.reshape(n, d // LANE, 1, LANE)
    return jnp.broadcast_to(x4, (n, d // LANE, 2, LANE)).reshape(n, PACK, LANE)


def _expand_mats(slots):
    col = np.arange(slots * PACK)
    s_of = (col // LANE) * PEER_GRP + (col % LANE) // PACK
    h_of = col % 2
    m = np.zeros((2, slots, slots * PACK), np.float32)
    for h in range(2):
        m[h, s_of[h_of == h], col[h_of == h]] = 1.0
    return jnp.asarray(m, BF16)


def _table_spec(tab):
    return pl.BlockSpec(tab.shape, lambda i: (0, 0, 0), pipeline_mode=pl.Buffered(1))


def _peer_u_kernel(rows_ref, xx_ref, eid_ref, g_ref, ex_ref, tab_ref, cw_ref, ae_scr, ao_scr):
    tb = xx_ref.shape[0]
    slots = ae_scr.shape[0]
    lane = lax.broadcasted_iota(jnp.int32, (SUBLANE, tb), 1)
    r_i = lax.broadcasted_iota(jnp.int32, (PACK, LANE), 0)
    k_i = lax.broadcasted_iota(jnp.int32, (PACK, LANE), 1)
    sel = jnp.where((k_i // PACK == r_i % SUBLANE) & (k_i % 2 == r_i // SUBLANE), 1.0, 0.0).astype(BF16)

    def token(t, _):
        xt = xx_ref[t]
        xg = jnp.concatenate([xt] * PEER_GRP, axis=0)
        here = lane == t
        for g in range(slots // PEER_GRP):
            tiles = [tab_ref[rows_ref[0, 0, t * slots + g * PEER_GRP + p]] for p in range(PEER_GRP)]
            prods = jnp.concatenate(tiles, axis=0) * xg
            part = jnp.dot(sel, prods, preferred_element_type=F32)
            tot = jnp.sum(part, axis=1, keepdims=True)
            rs = slice(g * PEER_GRP, (g + 1) * PEER_GRP)
            ae_scr[rs, :] = jnp.where(here, tot[:SUBLANE], ae_scr[rs, :])
            ao_scr[rs, :] = jnp.where(here, tot[SUBLANE:], ao_scr[rs, :])
        return 0

    def token_group(i, carry):
        for j in range(PEER_TOK_UNROLL):
            token(i * PEER_TOK_UNROLL + j, carry)
        return carry

    lax.fori_loop(0, tb // PEER_TOK_UNROLL, token_group, 0)
    odd = (eid_ref[0] & 1) == 1
    act = jnp.where(odd, ao_scr[...], ae_scr[...])
    coef = g_ref[0] * (0.5 * act * (1.0 + lax.erf(act * (2.0 ** -0.5))))
    zero = jnp.zeros_like(coef)
    tn = (((0,), (0,)), ((), ()))
    cw = lax.dot_general(jnp.where(odd, zero, coef).astype(BF16), ex_ref[0], tn, preferred_element_type=F32)
    cw = cw + lax.dot_general(jnp.where(odd, coef, zero).astype(BF16), ex_ref[1], tn, preferred_element_type=F32)
    cw_ref[...] = cw


def _peer_u(rows, eid, g, xx, tab_u, expand):
    nblk, slots, tb = eid.shape
    n_tok = xx.shape[0]
    return pl.pallas_call(
        _peer_u_kernel,
        grid=(nblk,),
        in_specs=[pl.BlockSpec((1, 1, slots * tb), lambda i: (i, 0, 0), memory_space=pltpu.SMEM),
                  pl.BlockSpec((tb,) + xx.shape[1:], lambda i: (i, 0, 0)),
                  pl.BlockSpec((1, slots, tb), lambda i: (i, 0, 0)),
                  pl.BlockSpec((1, slots, tb), lambda i: (i, 0, 0)),
                  pl.BlockSpec(expand.shape, lambda i: (0, 0, 0)),
                  _table_spec(tab_u)],
        out_specs=pl.BlockSpec((tb, slots * PACK), lambda i: (i, 0)),
        out_shape=jax.ShapeDtypeStruct((n_tok, slots * PACK), F32),
        scratch_shapes=[pltpu.VMEM((slots, tb), F32), pltpu.VMEM((slots, tb), F32)],
        compiler_params=pltpu.CompilerParams(
            dimension_semantics=("arbitrary",), vmem_limit_bytes=PEER_VMEM_LIMIT),
        name="peer_u",
    )(rows, xx, eid, g, expand, tab_u)


def _peer_v_kernel(rows_ref, cw_ref, tab_ref, o_ref, *, slots):
    tb = o_ref.shape[0]
    r_i = lax.broadcasted_iota(jnp.int32, (PACK, LANE), 0)
    k_i = lax.broadcasted_iota(jnp.int32, (PACK, LANE), 1)
    keep = (k_i % PACK) // 2 == r_i

    def token(t, _):
        acc = jnp.zeros((PACK, LANE), F32)
        cwt = cw_ref[t]
        for g in range(slots // PEER_GRP):
            tiles = [tab_ref[rows_ref[0, 0, t * slots + g * PEER_GRP + p]] for p in range(PEER_GRP)]
            lhs = jnp.where(keep, jnp.broadcast_to(cwt[g:g + 1, :], (PACK, LANE)), 0.0).astype(BF16)
            acc = acc + jnp.dot(lhs, jnp.concatenate(tiles, axis=0), preferred_element_type=F32)
        o_ref[t] = acc[:SUBLANE]
        return 0

    def token_group(i, carry):
        for j in range(PEER_TOK_UNROLL):
            token(i * PEER_TOK_UNROLL + j, carry)
        return carry

    lax.fori_loop(0, tb // PEER_TOK_UNROLL, token_group, 0)


def _peer_v(rows, cw, tab_v, tb):
    n_tok, groups, _ = cw.shape
    slots = groups * PEER_GRP
    return pl.pallas_call(
        functools.partial(_peer_v_kernel, slots=slots),
        grid=(n_tok // tb,),
        in_specs=[pl.BlockSpec((1, 1, slots * tb), lambda i: (i, 0, 0), memory_space=pltpu.SMEM),
                  pl.BlockSpec((tb, groups, LANE), lambda i: (i, 0, 0)),
                  _table_spec(tab_v)],
        out_specs=pl.BlockSpec((tb, SUBLANE, LANE), lambda i: (i, 0, 0)),
        out_shape=jax.ShapeDtypeStruct((n_tok, SUBLANE, LANE), F32),
        compiler_params=pltpu.CompilerParams(
            dimension_semantics=("arbitrary",), vmem_limit_bytes=PEER_VMEM_LIMIT),
        name="peer_v",
    )(rows, cw, tab_v)


def _residual_kernel(h_ref, p_ref, gate_ref, o_ref):
    o_ref[...] = h_ref[...] + gate_ref[0] * p_ref[...]


def _residual(h2, p2, gate, t_len, tm=512):
    n_tok, d = h2.shape
    bpt = t_len // tm
    return pl.pallas_call(
        _residual_kernel,
        grid=(n_tok // tm,),
        in_specs=[pl.BlockSpec((tm, d), lambda i: (i, 0)), pl.BlockSpec((tm, d), lambda i: (i, 0)),
                  pl.BlockSpec((1, 1, d), lambda i: (i // bpt, 0, 0))],
        out_specs=pl.BlockSpec((tm, d), lambda i: (i, 0)),
        out_shape=jax.ShapeDtypeStruct((n_tok, d), F32),
        compiler_params=pltpu.CompilerParams(dimension_semantics=("parallel",)),
        name="peer_residual",
    )(h2, p2, gate)


def _layer(h2, mod, t_len, p):
    n_tok, d = h2.shape
    bsz = n_tok // t_len
    z = _inproj(h2, p["norm1_g"], mod[:, 1], mod[:, 0], _reorder_w_in(p["w_in"]), t_len)
    z3 = z.reshape(bsz, t_len, Z_COLS)
    ya = _dsa(z3, p["a_qn_g"], p["a_kn_g"]).reshape(n_tok, -1)
    yb = _stickbreak(z3, p["b_qn_g"], p["b_kn_g"]).reshape(n_tok, -1)
    yc = _gla(z3, p["c_w_alpha"], p["c_b_alpha"], p["c_on_g"]).reshape(n_tok, -1)
    h1, n2, q = _merge(h2, z, ya, yb, yc, p["b_gate"], p["w_oa"], p["w_ob"], p["w_oc"], p["w_out"],
                       mod[:, 2], p["norm2_g"], mod[:, 4], mod[:, 3], p["peer_wq"], t_len)
    eid, g = _peer_topk(q, p["peer_subkeys"])
    nblk, slots, tb = eid.shape
    rows = jnp.swapaxes(eid >> 1, 1, 2).reshape(nblk, 1, tb * slots)
    cw = _peer_u(rows, eid, g, _dup_tiles(n2), _pair_tiles(p["peer_u"]), _expand_mats(slots))
    out = _peer_v(rows, cw.reshape(n_tok, slots // PEER_GRP, LANE), _pair_tiles(p["peer_v"]), tb)
    return _residual(h1, out.reshape(n_tok, d), mod[:, 5], t_len)


def kernel(x, c, norm1_g, norm2_g, w_mod, b_mod, w_in, b_gate, a_qn_g, a_kn_g, b_qn_g, b_kn_g, c_w_alpha, c_b_alpha, c_on_g, w_oa, w_ob, w_oc, w_out, peer_wq, peer_subkeys, peer_u, peer_v):
    bsz, t_len, d = x.shape
    params = dict(norm1_g=norm1_g, norm2_g=norm2_g, w_in=w_in, b_gate=b_gate, a_qn_g=a_qn_g, a_kn_g=a_kn_g,
                  b_qn_g=b_qn_g, b_kn_g=b_kn_g, c_w_alpha=c_w_alpha, c_b_alpha=c_b_alpha, c_on_g=c_on_g,
                  w_oa=w_oa, w_ob=w_ob, w_oc=w_oc, w_out=w_out, peer_wq=peer_wq, peer_subkeys=peer_subkeys,
                  peer_u=peer_u, peer_v=peer_v)
    h2 = x.reshape(bsz * t_len, d)
    for l in range(w_in.shape[0]):
        mod = _mod(c, w_mod[l], b_mod[l]).reshape(bsz, 6, 1, d)
        h2 = _layer(h2, mod, t_len, {k: v[l] for k, v in params.items()})
    return h2.reshape(bsz, t_len, d)
```

```python
import functools

import jax
import jax.numpy as jnp
import numpy as np
from jax import lax
from jax.experimental import pallas as pl
from jax.experimental.pallas import tpu as pltpu

F32 = jnp.float32
BF16 = jnp.bfloat16

D_MODEL = 1024
EPS = 1e-6
CHUNK = 64
TOPK_MAX = 256
A_HEADS, A_DIM = 8, 64
IDX_HEADS, IDX_DIM = 4, 64
B_HEADS, B_DIM = 8, 64
C_HEADS, C_KEY, C_VAL, C_RANK, C_TAU = 4, 64, 128, 16, 16.0
PEER_HEADS, PEER_NKEYS, PEER_QDIM, PEER_TOPK = 8, 128, 256, 16
N_BRANCH = 3

LANE = 128

_SEGMENTS = (
    ("gl", 3072), ("aq", 512), ("bq", 512), ("bk", 512), ("bv", 512), ("cv", 512), ("cr", 512),
    ("iq", 256), ("cq", 256), ("ck", 256), ("akv", 128), ("ikw", 128), ("ca", 128),
)
COL = {}
_off = 0
for _name, _w in _SEGMENTS:
    COL[_name] = (_off, _w)
    _off += _w
Z_COLS = _off
Z_TILE = Z_COLS // 3

_REF_WIDTHS = (512, 64, 64, 256, 64, 4, 512, 512, 512, 256, 256, 512, 512, 16, 3072)


def _reorder_w_in(w_in):
    offs = np.cumsum((0,) + _REF_WIDTHS)
    p = [w_in[:, offs[i]:offs[i + 1]] for i in range(len(_REF_WIDTHS))]
    aq, ak, av, iq, ik, iw, bq, bk, bv, cq, ck, cv, cr, ca, gl = p
    d = w_in.shape[0]
    z = lambda n: jnp.zeros((d, n), w_in.dtype)
    cols = [gl, aq, bq, bk, bv, cv, cr, iq, cq, ck, ak, av, ik, iw, z(60), ca, z(112)]
    return jnp.concatenate(cols, axis=1).astype(BF16)


def _cb(name):
    off, w = COL[name]
    return off // w


def _mod_kernel(c_ref, w_ref, b_ref, o_ref):
    c = c_ref[...]
    ca = (c * jax.nn.sigmoid(c)).astype(BF16)
    o_ref[...] = jnp.dot(ca, w_ref[...].astype(BF16), preferred_element_type=F32) + b_ref[...]


def _mod(c, w_mod, b_mod):
    bsz, d = c.shape
    n = w_mod.shape[1]
    tn = 768
    return pl.pallas_call(
        _mod_kernel,
        grid=(n // tn,),
        in_specs=[pl.BlockSpec((bsz, d), lambda j: (0, 0)),
                  pl.BlockSpec((d, tn), lambda j: (0, j)),
                  pl.BlockSpec((1, tn), lambda j: (0, j))],
        out_specs=pl.BlockSpec((bsz, tn), lambda j: (0, j)),
        out_shape=jax.ShapeDtypeStruct((bsz, n), F32),
        name="adaln_mod",
    )(c, w_mod, b_mod.reshape(1, n))


def _inproj_kernel(h_ref, g_ref, sc_ref, sh_ref, w_ref, z_ref, n_scr):
    @pl.when(pl.program_id(1) == 0)
    def _():
        x = h_ref[...]
        y = x * lax.rsqrt(jnp.mean(x * x, axis=-1, keepdims=True) + EPS)
        n = y * g_ref[...] * (1.0 + sc_ref[0]) + sh_ref[0]
        n_scr[...] = n.astype(BF16)

    z_ref[...] = jnp.dot(n_scr[...], w_ref[...], preferred_element_type=F32).astype(BF16)


def _inproj(h2, norm_g, scale, shift, w_z, t_len, tm=512):
    n_tok, d = h2.shape
    tm = min(tm, t_len)
    bpt = t_len // tm
    return pl.pallas_call(
        _inproj_kernel,
        grid=(n_tok // tm, Z_COLS // Z_TILE),
        in_specs=[pl.BlockSpec((tm, d), lambda i, j: (i, 0)),
                  pl.BlockSpec((1, d), lambda i, j: (0, 0)),
                  pl.BlockSpec((1, 1, d), lambda i, j: (i // bpt, 0, 0)),
                  pl.BlockSpec((1, 1, d), lambda i, j: (i // bpt, 0, 0)),
                  pl.BlockSpec((d, Z_TILE), lambda i, j: (0, j))],
        out_specs=pl.BlockSpec((tm, Z_TILE), lambda i, j: (i, j)),
        out_shape=jax.ShapeDtypeStruct((n_tok, Z_COLS), BF16),
        scratch_shapes=[pltpu.VMEM((tm, d), BF16)],
        compiler_params=pltpu.CompilerParams(
            dimension_semantics=("parallel", "arbitrary"), vmem_limit_bytes=48 * 1024 * 1024),
        name="norm_inproj",
    )(h2, norm_g.reshape(1, d), scale, shift, w_z)


def _head_rms(x, gain):
    return x * lax.rsqrt(jnp.mean(x * x, axis=-1, keepdims=True) + EPS) * gain


DSA_QB = 128
DSA_GROUP = 4
DSA_BISECT = 30


def _dsa_kernel(aq_ref, akv_ref, iq_ref, ikwq_ref, ikwk_ref, qg_ref, kg_ref, o_ref, sm_scr, sel_scr,
                *, q_off, n_sel):
    qb = aq_ref.shape[1]
    tk = akv_ref.shape[1]
    q0 = (q_off + pl.program_id(1)) * qb
    t_pos = q0 + lax.broadcasted_iota(jnp.int32, (qb, 1), 0)
    limit = (t_pos // CHUNK + 1) * CHUNK
    s_pos = lax.broadcasted_iota(jnp.int32, (1, tk), 1)
    adm = s_pos < limit

    kidx = ikwk_ref[0][:, :IDX_DIM]
    qidx = iq_ref[0]
    w = ikwq_ref[0][:, IDX_DIM:IDX_DIM + IDX_HEADS].astype(F32)
    score = jnp.zeros((qb, tk), F32)
    for h in range(IDX_HEADS):
        rel = lax.dot_general(qidx[:, h * IDX_DIM:(h + 1) * IDX_DIM], kidx,
                              (((1,), (1,)), ((), ())), preferred_element_type=F32)
        score = score + w[:, h:h + 1] * jnp.maximum(rel, 0.0)
    neg_inf = jnp.float32(-jnp.inf)
    sm_scr[...] = jnp.where(adm, score, neg_inf)

    hi0 = jnp.max(sm_scr[...], axis=-1, keepdims=True)
    lo0 = jnp.min(jnp.where(adm, score, jnp.float32(jnp.inf)), axis=-1, keepdims=True)
    kf = jnp.float32(n_sel)

    def bisect(_, carry):
        lo, hi = carry
        mid = 0.5 * (lo + hi)
        cnt = jnp.sum(jnp.where(sm_scr[...] > mid, 1.0, 0.0), axis=-1, keepdims=True)
        below = cnt < kf
        return jnp.where(below, lo, mid), jnp.where(below, mid, hi)

    _, hi = lax.fori_loop(0, DSA_BISECT, bisect, (lo0, hi0))
    sm = sm_scr[...]
    v1 = jnp.max(jnp.where(sm <= hi, sm, neg_inf), axis=-1, keepdims=True)
    c_ge1 = jnp.sum(jnp.where(sm >= v1, 1.0, 0.0), axis=-1, keepdims=True)
    v2 = jnp.max(jnp.where(sm < v1, sm, neg_inf), axis=-1, keepdims=True)
    thr = jnp.where(c_ge1 >= kf, v1, v2)
    need = kf - jnp.sum(jnp.where(sm > thr, 1.0, 0.0), axis=-1, keepdims=True)

    r_i = lax.broadcasted_iota(jnp.int32, (LANE, LANE), 0)
    c_i = lax.broadcasted_iota(jnp.int32, (LANE, LANE), 1)
    incl = jnp.where(r_i <= c_i, 1.0, 0.0).astype(BF16)
    all_sel = limit <= n_sel
    run = jnp.zeros((qb, 1), F32)
    for j in range(tk // LANE):
        sl = slice(j * LANE, (j + 1) * LANE)
        smj = sm_scr[:, sl]
        eqj = jnp.where(smj == thr, 1.0, 0.0)
        inc = jnp.dot(eqj.astype(BF16), incl, preferred_element_type=F32)
        rank = inc - eqj + run
        picked = (smj > thr) | ((smj == thr) & (rank < need))
        admj = (j * LANE + lax.broadcasted_iota(jnp.int32, (1, LANE), 1)) < limit
        selj = (all_sel & admj) | (jnp.logical_not(all_sel) & picked)
        sel_scr[:, sl] = jnp.where(selj, 0.0, neg_inf)
        run = run + inc[:, LANE - 1:LANE]

    kv = akv_ref[0].astype(F32)
    k = _head_rms(kv[:, :A_DIM], kg_ref[...]).astype(BF16)
    v = akv_ref[0][:, A_DIM:2 * A_DIM]
    aq = aq_ref[0].astype(F32)
    scale = A_DIM ** -0.5
    for h in range(A_HEADS):
        qh = _head_rms(aq[:, h * A_DIM:(h + 1) * A_DIM], qg_ref[...]).astype(BF16)
        logits = lax.dot_general(qh, k, (((1,), (1,)), ((), ())), preferred_element_type=F32) * scale
        logits = logits + sel_scr[...]
        m = jnp.max(logits, axis=-1, keepdims=True)
        p = jnp.exp(logits - m)
        denom = jnp.sum(p, axis=-1, keepdims=True)
        oh = jnp.dot(p.astype(BF16), v, preferred_element_type=F32) / denom
        o_ref[0, :, h * A_DIM:(h + 1) * A_DIM] = oh.astype(o_ref.dtype)


def _dsa(z3, qg, kg):
    bsz, t_len, _ = z3.shape
    n_sel = min(TOPK_MAX, t_len // 4)
    span = DSA_QB * DSA_GROUP
    outs = []
    for g in range(t_len // span):
        tk = span * (g + 1)
        q_off = g * DSA_GROUP
        qmap = lambda b, i, c: (b, q_off + i, c)
        outs.append(pl.pallas_call(
            functools.partial(_dsa_kernel, q_off=q_off, n_sel=n_sel),
            grid=(bsz, DSA_GROUP),
            in_specs=[pl.BlockSpec((1, DSA_QB, 512), lambda b, i, q_off=q_off: (b, q_off + i, _cb("aq"))),
                      pl.BlockSpec((1, tk, LANE), lambda b, i: (b, 0, _cb("akv"))),
                      pl.BlockSpec((1, DSA_QB, 256), lambda b, i, q_off=q_off: (b, q_off + i, _cb("iq"))),
                      pl.BlockSpec((1, DSA_QB, LANE), lambda b, i, q_off=q_off: (b, q_off + i, _cb("ikw"))),
                      pl.BlockSpec((1, tk, LANE), lambda b, i: (b, 0, _cb("ikw"))),
                      pl.BlockSpec((1, A_DIM), lambda b, i: (0, 0)),
                      pl.BlockSpec((1, A_DIM), lambda b, i: (0, 0))],
            out_specs=pl.BlockSpec((1, DSA_QB, 512), lambda b, i: (b, i, 0)),
            out_shape=jax.ShapeDtypeStruct((bsz, span, 512), BF16),
            scratch_shapes=[pltpu.VMEM((DSA_QB, tk), F32), pltpu.VMEM((DSA_QB, tk), F32)],
            compiler_params=pltpu.CompilerParams(
                dimension_semantics=("parallel", "arbitrary"), vmem_limit_bytes=48 * 1024 * 1024),
            name=f"dsa_attention_g{g}",
        )(z3, z3, z3, z3, z3, qg.reshape(1, A_DIM), kg.reshape(1, A_DIM)))
    return jnp.concatenate(outs, axis=1)


def _split_dot(x, m, x_is_lhs=True):
    hi = x.astype(BF16)
    lo = (x - hi.astype(F32)).astype(BF16)
    if x_is_lhs:
        return jnp.dot(hi, m, preferred_element_type=F32) + jnp.dot(lo, m, preferred_element_type=F32)
    return jnp.dot(m, hi, preferred_element_type=F32) + jnp.dot(m, lo, preferred_element_type=F32)


SB_QB = 128
SB_SWEEP = 4


def _sb_kernel(q_ref, k_ref, v_ref, qg_ref, kg_ref, o_ref, kn_scr):
    qi = pl.program_id(2)
    qb = q_ref.shape[1]
    hd = B_DIM

    @pl.when(qi == 0)
    def _():
        kf = k_ref[0].astype(F32)
        for h in range(2):
            kn_scr[h] = _head_rms(kf[:, h * hd:(h + 1) * hd], kg_ref[...]).astype(BF16)

    q = q_ref[0].astype(F32)
    qn = [_head_rms(q[:, h * hd:(h + 1) * hd], qg_ref[...]).astype(BF16) for h in range(2)]
    r_i = lax.broadcasted_iota(jnp.int32, (qb, qb), 0)
    c_i = lax.broadcasted_iota(jnp.int32, (qb, qb), 1)
    m_incl = jnp.where(r_i >= c_i, 1.0, 0.0).astype(BF16)
    t_pos = qi * qb + lax.broadcasted_iota(jnp.int32, (qb, 1), 0)
    lane = lax.broadcasted_iota(jnp.int32, (1, 2 * hd), 1)
    scale = hd ** -0.5

    def body(i, carry):
        sb = qi // SB_SWEEP - i
        carry = list(carry)
        blocks = list(reversed(range(SB_SWEEP)))
        starts = [pl.multiple_of((sb * SB_SWEEP + j) * qb, qb) for j in blocks]
        stricts = [(ks + lax.broadcasted_iota(jnp.int32, (1, qb), 1)) < t_pos for ks in starts]
        zs = [[lax.dot_general(qn[h], kn_scr[h, pl.ds(ks, qb), :], (((1,), (1,)), ((), ())),
                               preferred_element_type=F32) * scale for h in range(2)] for ks in starts]
        lbs = [[jnp.minimum(z, 0.0) - jnp.log(1.0 + jnp.exp(-jnp.abs(z))) for z in zb] for zb in zs]
        lks = [[jnp.where(st, lb - z, 0.0) for lb, z in zip(lbb, zb)]
               for st, lbb, zb in zip(stricts, lbs, zs)]
        incs = [[_split_dot(lk, m_incl) for lk in lkb] for lkb in lks]
        probs = []
        for st, lbb, lkb, incb in zip(stricts, lbs, lks, incs):
            row = []
            for h in range(2):
                later = incb[h] - lkb[h] + carry[2 * h]
                row.append(jnp.where(st, jnp.exp(lbb[h] + later), 0.0).astype(BF16))
                carry[2 * h] = carry[2 * h] + incb[h][:, 0:1]
            probs.append(row)
        for ks, row in zip(starts, probs):
            vblk = v_ref[0, pl.ds(ks, qb), :]
            for h in range(2):
                carry[2 * h + 1] = carry[2 * h + 1] + jnp.dot(row[h], vblk, preferred_element_type=F32)
        return tuple(carry)

    init = (jnp.zeros((qb, 1), F32), jnp.zeros((qb, 2 * hd), F32)) * 2
    res = lax.fori_loop(0, qi // SB_SWEEP + 1, body, init)
    o_ref[0] = jnp.where(lane < hd, res[1], res[3]).astype(o_ref.dtype)


def _stickbreak(z3, qg, kg):
    bsz, t_len, _ = z3.shape
    nq = t_len // SB_QB
    npair = B_HEADS // 2
    cq, ck, cv = COL["bq"][0] // LANE, COL["bk"][0] // LANE, COL["bv"][0] // LANE
    return pl.pallas_call(
        _sb_kernel,
        grid=(bsz, npair, nq),
        in_specs=[pl.BlockSpec((1, SB_QB, LANE), lambda b, p, i: (b, i, cq + p)),
                  pl.BlockSpec((1, t_len, LANE), lambda b, p, i: (b, 0, ck + p)),
                  pl.BlockSpec((1, t_len, LANE), lambda b, p, i: (b, 0, cv + p)),
                  pl.BlockSpec((1, B_DIM), lambda b, p, i: (0, 0)),
                  pl.BlockSpec((1, B_DIM), lambda b, p, i: (0, 0))],
        out_specs=pl.BlockSpec((1, SB_QB, LANE), lambda b, p, i: (b, i, p)),
        out_shape=jax.ShapeDtypeStruct((bsz, t_len, B_HEADS * B_DIM), BF16),
        scratch_shapes=[pltpu.VMEM((2, t_len, B_DIM), BF16)],
        compiler_params=pltpu.CompilerParams(
            dimension_semantics=("parallel", "parallel", "arbitrary"), vmem_limit_bytes=48 * 1024 * 1024),
        name="stickbreak_attention",
    )(z3, z3, z3, qg.reshape(1, B_DIM), kg.reshape(1, B_DIM))


GLA_SUB = 16


def _gla_kernel(q_ref, k_ref, v_ref, r_ref, a_ref, wa_ref, ba_ref, og_ref, o_ref, st_scr):
    t_len = q_ref.shape[1]
    nc = t_len // CHUNK
    dk, dv, sub = C_KEY, C_VAL, GLA_SUB
    r_i = lax.broadcasted_iota(jnp.int32, (CHUNK, CHUNK), 0)
    c_i = lax.broadcasted_iota(jnp.int32, (CHUNK, CHUNK), 1)
    l_incl = jnp.where(r_i >= c_i, 1.0, 0.0).astype(BF16)
    st_scr[...] = jnp.zeros_like(st_scr)
    neg_inf = jnp.float32(-jnp.inf)
    pr = lax.broadcasted_iota(jnp.int32, (sub * sub, 1), 0)
    pair_ok = pr % sub <= pr // sub
    fr = lax.broadcasted_iota(jnp.int32, (sub, sub * sub), 0)
    fc = lax.broadcasted_iota(jnp.int32, (sub, sub * sub), 1)
    fold_t = jnp.where(fc // sub == fr, 1.0, 0.0).astype(BF16)
    hr = lax.broadcasted_iota(jnp.int32, (2 * dk, 2 * dv), 0)
    hc = lax.broadcasted_iota(jnp.int32, (2 * dk, 2 * dv), 1)
    head_ones = jnp.where(hr // dk == hc // dv, 1.0, 0.0).astype(BF16)

    def per_t(x):
        return jnp.broadcast_to(x[:, None, :], (sub, sub, x.shape[1])).reshape(sub * sub, x.shape[1])

    def per_s(x):
        return jnp.broadcast_to(x[None, :, :], (sub, sub, x.shape[1])).reshape(sub * sub, x.shape[1])

    def chunk(c, _):
        rows = pl.ds(pl.multiple_of(c * CHUNK, CHUNK), CHUNK)
        pre = jnp.dot(a_ref[0, rows, :], wa_ref[...], preferred_element_type=F32) + ba_ref[...]
        la2 = (jnp.minimum(pre, 0.0) - jnp.log(1.0 + jnp.exp(-jnp.abs(pre)))) * (1.0 / C_TAU)
        b2 = _split_dot(la2, l_incl, x_is_lhs=False)
        q2 = q_ref[0, rows, :].astype(F32) * (dk ** -0.5)
        k2 = k_ref[0, rows, :].astype(F32)
        v2f = v_ref[0, rows, :].astype(F32)

        diag = []
        for i in range(CHUNK // sub):
            sl = slice(i * sub, (i + 1) * sub)
            bt, qt = (per_t(x[sl]) for x in (b2, q2))
            bs, ks_, vs = (per_s(x[sl]) for x in (b2, k2, v2f))
            e = jnp.exp(jnp.where(pair_ok, bt - bs, neg_inf))
            w2 = _split_dot(qt * ks_ * e, head_ones)
            diag.append(jnp.dot(fold_t, (w2 * vs).astype(BF16), preferred_element_type=F32))

        for h in range(2):
            b = b2[:, h * dk:(h + 1) * dk]
            q = q2[:, h * dk:(h + 1) * dk]
            k = k2[:, h * dk:(h + 1) * dk]
            v = v_ref[0, rows, h * dv:(h + 1) * dv]
            st = st_scr[h]
            o = lax.dot_general((q * jnp.exp(b)).astype(BF16), st.astype(BF16),
                                (((1,), (1,)), ((), ())), preferred_element_type=F32)
            intra = []
            for i in range(CHUNK // sub):
                lo_r = i * sub
                bi, qi_ = b[lo_r:lo_r + sub], q[lo_r:lo_r + sub]
                oi = diag[i][:, h * dv:(h + 1) * dv]
                if i > 0:
                    ref_row = b[lo_r - 1:lo_r, :]
                    qs = (qi_ * jnp.exp(bi - ref_row)).astype(BF16)
                    ks = (k[:lo_r] * jnp.exp(ref_row - b[:lo_r])).astype(BF16)
                    att = lax.dot_general(qs, ks, (((1,), (1,)), ((), ())), preferred_element_type=F32)
                    oi = oi + jnp.dot(att.astype(BF16), v[:lo_r], preferred_element_type=F32)
                intra.append(oi)
            o = o + jnp.concatenate(intra, axis=0)
            b_last = b[CHUNK - 1:CHUNK, :]
            kd = (k * jnp.exp(b_last - b)).astype(BF16)
            upd = lax.dot_general(v, kd, (((0,), (0,)), ((), ())), preferred_element_type=F32)
            st_scr[h] = st * jnp.exp(b_last) + upd
            y = o * lax.rsqrt(jnp.mean(o * o, axis=-1, keepdims=True) + EPS) * og_ref[:, h * dv:(h + 1) * dv]
            r = r_ref[0, rows, h * dv:(h + 1) * dv].astype(F32)
            o_ref[0, rows, h * dv:(h + 1) * dv] = (y * (r * jax.nn.sigmoid(r))).astype(o_ref.dtype)
        return 0

    lax.fori_loop(0, nc, chunk, 0)


def _gla(z3, w_alpha, b_alpha, out_gain):
    bsz, t_len, _ = z3.shape
    npair = C_HEADS // 2
    wa = jnp.zeros((LANE, C_HEADS * C_KEY), F32).at[:C_RANK].set(w_alpha).astype(BF16)
    cq, ck = COL["cq"][0] // LANE, COL["ck"][0] // LANE
    cv, cr, ca = COL["cv"][0] // 256, COL["cr"][0] // 256, COL["ca"][0] // LANE
    return pl.pallas_call(
        _gla_kernel,
        grid=(bsz, npair),
        in_specs=[pl.BlockSpec((1, t_len, LANE), lambda b, p: (b, 0, cq + p)),
                  pl.BlockSpec((1, t_len, LANE), lambda b, p: (b, 0, ck + p)),
                  pl.BlockSpec((1, t_len, 256), lambda b, p: (b, 0, cv + p)),
                  pl.BlockSpec((1, t_len, 256), lambda b, p: (b, 0, cr + p)),
                  pl.BlockSpec((1, t_len, LANE), lambda b, p: (b, 0, ca)),
                  pl.BlockSpec((LANE, LANE), lambda b, p: (0, p)),
                  pl.BlockSpec((1, LANE), lambda b, p: (0, p)),
                  pl.BlockSpec((1, 256), lambda b, p: (0, p))],
        out_specs=pl.BlockSpec((1, t_len, 256), lambda b, p: (b, 0, p)),
        out_shape=jax.ShapeDtypeStruct((bsz, t_len, C_HEADS * C_VAL), BF16),
        scratch_shapes=[pltpu.VMEM((2, C_VAL, C_KEY), F32)],
        compiler_params=pltpu.CompilerParams(
            dimension_semantics=("parallel", "parallel"), vmem_limit_bytes=48 * 1024 * 1024),
        name="gla_attention",
    )(z3, z3, z3, z3, z3, wa, b_alpha.reshape(1, -1), out_gain.reshape(1, -1))


def _merge_kernel(h_ref, g0_ref, g1_ref, g2_ref, bg_ref, ya_ref, yb_ref, yc_ref, woa_ref, wob_ref, woc_ref,
                  wout_ref, gate_ref, ng_ref, sc_ref, sh_ref, wpq_ref, h1_ref, n2_ref, q_ref):
    d = h_ref.shape[1]
    merged = jnp.zeros(h_ref.shape, F32)
    for i, (g_ref, y_ref, w_ref) in enumerate(((g0_ref, ya_ref, woa_ref), (g1_ref, yb_ref, wob_ref),
                                               (g2_ref, yc_ref, woc_ref))):
        gate = jax.nn.sigmoid(g_ref[...].astype(F32) + bg_ref[:, i * d:(i + 1) * d])
        merged = merged + gate * jnp.dot(y_ref[...], w_ref[...], preferred_element_type=F32)
    y = jnp.dot(merged.astype(BF16), wout_ref[...], preferred_element_type=F32)
    h1 = h_ref[...] + gate_ref[0] * y
    h1_ref[...] = h1
    n = h1 * lax.rsqrt(jnp.mean(h1 * h1, axis=-1, keepdims=True) + EPS)
    n2 = (n * ng_ref[...] * (1.0 + sc_ref[0]) + sh_ref[0]).astype(BF16)
    n2_ref[...] = n2
    q_ref[...] = jnp.dot(n2, wpq_ref[...], preferred_element_type=F32).astype(BF16)


def _merge(h2, z, ya, yb, yc, b_gate, w_oa, w_ob, w_oc, w_out, gate1, norm_g, scale, shift, w_pq, t_len, tm=256):
    n_tok, d = h2.shape
    bpt = t_len // tm
    nq = w_pq.shape[1]
    row = lambda i: (i, 0)
    fixed = lambda i: (0, 0)
    per_b = lambda i: (i // bpt, 0, 0)
    bw = ya.shape[1]
    return pl.pallas_call(
        _merge_kernel,
        grid=(n_tok // tm,),
        in_specs=[pl.BlockSpec((tm, d), row),
                  pl.BlockSpec((tm, d), lambda i: (i, 0)),
                  pl.BlockSpec((tm, d), lambda i: (i, 1)),
                  pl.BlockSpec((tm, d), lambda i: (i, 2)),
                  pl.BlockSpec((1, N_BRANCH * d), fixed),
                  pl.BlockSpec((tm, bw), row), pl.BlockSpec((tm, bw), row), pl.BlockSpec((tm, bw), row),
                  pl.BlockSpec((bw, d), fixed), pl.BlockSpec((bw, d), fixed), pl.BlockSpec((bw, d), fixed),
                  pl.BlockSpec((d, d), fixed),
                  pl.BlockSpec((1, 1, d), per_b),
                  pl.BlockSpec((1, d), fixed),
                  pl.BlockSpec((1, 1, d), per_b), pl.BlockSpec((1, 1, d), per_b),
                  pl.BlockSpec((d, nq), fixed)],
        out_specs=[pl.BlockSpec((tm, d), row), pl.BlockSpec((tm, d), row), pl.BlockSpec((tm, nq), row)],
        out_shape=[jax.ShapeDtypeStruct((n_tok, d), F32), jax.ShapeDtypeStruct((n_tok, d), BF16),
                   jax.ShapeDtypeStruct((n_tok, nq), BF16)],
        compiler_params=pltpu.CompilerParams(
            dimension_semantics=("parallel",), vmem_limit_bytes=48 * 1024 * 1024),
        name="merge_outproj_peerq",
    )(h2, z, z, z, b_gate.reshape(1, -1), ya, yb, yc, w_oa.astype(BF16), w_ob.astype(BF16), w_oc.astype(BF16),
      w_out.astype(BF16), gate1, norm_g.reshape(1, d), scale, shift, w_pq.astype(BF16))


PEER_TB = 128


def _take_top(vals, ids, k):
    neg_inf = jnp.float32(-jnp.inf)
    big = jnp.float32(2.0 ** 30)
    out_v, out_i = [], []
    for _ in range(k):
        m = jnp.max(vals, axis=0, keepdims=True)
        first = jnp.min(jnp.where(vals == m, ids, big), axis=0, keepdims=True)
        out_v.append(m)
        out_i.append(first)
        vals = jnp.where(ids == first, neg_inf, vals)
    return out_v, out_i


def _peer_topk_kernel(q_ref, sk_ref, eid_ref, g_ref):
    tb = q_ref.shape[0]
    hq = PEER_QDIM // 2
    key_id = lax.broadcasted_iota(jnp.int32, (PEER_NKEYS, tb), 0).astype(F32)
    k = PEER_TOPK
    n_exp = PEER_NKEYS * PEER_NKEYS
    n_j = [k if i == 0 else 8 for i in range(k)]
    pos = jnp.concatenate([(i * k + lax.broadcasted_iota(jnp.int32, (n_j[i], tb), 0)) * n_exp
                           for i in range(k)], axis=0).astype(F32)
    for h in range(PEER_HEADS):
        tops, topi = [], []
        for p in range(2):
            qhp = q_ref[:, (2 * h + p) * hq:(2 * h + p + 1) * hq]
            s = lax.dot_general(sk_ref[h, p], qhp, (((1,), (1,)), ((), ())), preferred_element_type=F32)
            v, ids = _take_top(s, key_id, k)
            tops.append(v)
            topi.append(ids)
        b_s = jnp.concatenate(tops[1], axis=0)
        b_i = jnp.concatenate(topi[1], axis=0)
        cand = jnp.concatenate([tops[0][i] + b_s[:n_j[i]] for i in range(k)], axis=0)
        cid = jnp.concatenate([topi[0][i] * PEER_NKEYS + b_i[:n_j[i]] for i in range(k)], axis=0)
        best, tag = _take_top(cand, pos + cid, k)
        best = jnp.concatenate(best, axis=0)
        e = jnp.exp(best - best[0:1])
        g_ref[0, h * k:(h + 1) * k, :] = e / jnp.sum(e, axis=0, keepdims=True)
        eid_ref[0, h * k:(h + 1) * k, :] = jnp.concatenate(tag, axis=0).astype(jnp.int32) % n_exp


def _peer_topk(q, sub_keys):
    n_tok = q.shape[0]
    nblk = n_tok // PEER_TB
    slots = PEER_HEADS * PEER_TOPK
    return pl.pallas_call(
        _peer_topk_kernel,
        grid=(nblk,),
        in_specs=[pl.BlockSpec((PEER_TB, q.shape[1]), lambda i: (i, 0)),
                  pl.BlockSpec(sub_keys.shape, lambda i: (0, 0, 0, 0))],
        out_specs=[pl.BlockSpec((1, slots, PEER_TB), lambda i: (i, 0, 0)),
                   pl.BlockSpec((1, slots, PEER_TB), lambda i: (i, 0, 0))],
        out_shape=[jax.ShapeDtypeStruct((nblk, slots, PEER_TB), jnp.int32),
                   jax.ShapeDtypeStruct((nblk, slots, PEER_TB), F32)],
        compiler_params=pltpu.CompilerParams(dimension_semantics=("parallel",)),
        name="peer_topk",
    )(q, sub_keys.astype(BF16))


PEER_VMEM_LIMIT = 50 * 1024 * 1024
SUBLANE = 8
PACK = 2 * SUBLANE
PEER_GRP = LANE // PACK
PEER_TOK_UNROLL = 8


def _pair_tiles(tab):
    e, d = tab.shape
    bits = lax.bitcast_convert_type(tab.astype(BF16), jnp.uint16).astype(jnp.uint32)
    bits = bits.reshape(e // 2, 2, d // LANE, LANE)
    return (bits[:, 0] | (bits[:, 1] << 16)).reshape(e // 2 * SUBLANE, LANE)


def _tile(tab_ref, word_row):
    words = tab_ref[pl.ds(pl.multiple_of(word_row, SUBLANE), SUBLANE), :]
    return pltpu.bitcast(words, BF16)


def _dup_tiles(x2):
    n, d = x2.shape
    x4 = x2.reshape(n, d // LANE, 1, LANE)
    return jnp.broadcast_to(x4, (n, d // LANE, 2, LANE)).reshape(n, PACK, LANE)


def _expand_mats(slots):
    col = np.arange(slots * PACK)
    s_of = (col // LANE) * PEER_GRP + (col % LANE) // PACK
    h_of = col % 2
    m = np.zeros((2, slots, slots * PACK), np.float32)
    for h in range(2):
        m[h, s_of[h_of == h], col[h_of == h]] = 1.0
    return jnp.asarray(m, BF16)


def _table_spec(tab):
    return pl.BlockSpec(tab.shape, lambda i: (0, 0), pipeline_mode=pl.Buffered(1))


def _peer_u_kernel(rows_ref, xx_ref, eid_ref, g_ref, ex_ref, tab_ref, cw_ref, ae_scr, ao_scr):
    tb = xx_ref.shape[0]
    slots = ae_scr.shape[0]
    lane = lax.broadcasted_iota(jnp.int32, (SUBLANE, tb), 1)
    r_i = lax.broadcasted_iota(jnp.int32, (PACK, LANE), 0)
    k_i = lax.broadcasted_iota(jnp.int32, (PACK, LANE), 1)
    sel = jnp.where((k_i // PACK == r_i % SUBLANE) & (k_i % 2 == r_i // SUBLANE), 1.0, 0.0).astype(BF16)

    def token(t, _):
        xt = xx_ref[t]
        xg = jnp.concatenate([xt] * PEER_GRP, axis=0)
        here = lane == t
        for g in range(0, slots // PEER_GRP, 2):
            halves = []
            for gg in (g, g + 1):
                tiles = [_tile(tab_ref, rows_ref[0, t, gg * PEER_GRP + p]) for p in range(PEER_GRP)]
                halves.append(jnp.concatenate(tiles, axis=0) * xg)
            part = jnp.dot(sel, jnp.concatenate(halves, axis=1), preferred_element_type=F32)
            for j, gg in enumerate((g, g + 1)):
                tot = jnp.sum(part[:, j * LANE:(j + 1) * LANE], axis=1, keepdims=True)
                rs = slice(gg * PEER_GRP, (gg + 1) * PEER_GRP)
                ae_scr[rs, :] = jnp.where(here, tot[:SUBLANE], ae_scr[rs, :])
                ao_scr[rs, :] = jnp.where(here, tot[SUBLANE:], ao_scr[rs, :])
        return 0

    def token_group(i, carry):
        for j in range(PEER_TOK_UNROLL):
            token(i * PEER_TOK_UNROLL + j, carry)
        return carry

    lax.fori_loop(0, tb // PEER_TOK_UNROLL, token_group, 0)
    odd = (eid_ref[0] & 1) == 1
    act = jnp.where(odd, ao_scr[...], ae_scr[...])
    coef = g_ref[0] * (0.5 * act * (1.0 + lax.erf(act * (2.0 ** -0.5))))
    zero = jnp.zeros_like(coef)
    tn = (((0,), (0,)), ((), ()))
    cw = lax.dot_general(jnp.where(odd, zero, coef).astype(BF16), ex_ref[0], tn, preferred_element_type=F32)
    cw = cw + lax.dot_general(jnp.where(odd, coef, zero).astype(BF16), ex_ref[1], tn, preferred_element_type=F32)
    cw_ref[...] = cw


def _peer_u(rows, eid, g, xx, tab_u, expand):
    nblk, slots, tb = eid.shape
    n_tok = xx.shape[0]
    return pl.pallas_call(
        _peer_u_kernel,
        grid=(nblk,),
        in_specs=[pl.BlockSpec((1, tb, slots), lambda i: (i, 0, 0), memory_space=pltpu.SMEM),
                  pl.BlockSpec((tb,) + xx.shape[1:], lambda i: (i, 0, 0)),
                  pl.BlockSpec((1, slots, tb), lambda i: (i, 0, 0)),
                  pl.BlockSpec((1, slots, tb), lambda i: (i, 0, 0)),
                  pl.BlockSpec(expand.shape, lambda i: (0, 0, 0)),
                  _table_spec(tab_u)],
        out_specs=pl.BlockSpec((tb, slots * PACK), lambda i: (i, 0)),
        out_shape=jax.ShapeDtypeStruct((n_tok, slots * PACK), F32),
        scratch_shapes=[pltpu.VMEM((slots, tb), F32), pltpu.VMEM((slots, tb), F32)],
        compiler_params=pltpu.CompilerParams(
            dimension_semantics=("arbitrary",), vmem_limit_bytes=PEER_VMEM_LIMIT),
        name="peer_u",
    )(rows, xx, eid, g, expand, tab_u)


def _peer_v_kernel(rows_ref, cw_ref, tab_ref, o_ref, *, slots):
    tb = o_ref.shape[0]
    r_i = lax.broadcasted_iota(jnp.int32, (PACK, LANE), 0)
    k_i = lax.broadcasted_iota(jnp.int32, (PACK, LANE), 1)
    keep = (k_i % PACK) // 2 == r_i % SUBLANE
    first = r_i < SUBLANE

    def token(t, _):
        acc = jnp.zeros((PACK, 2 * LANE), F32)
        cwt = cw_ref[t]
        for g in range(0, slots // PEER_GRP, 2):
            halves = [jnp.concatenate([_tile(tab_ref, rows_ref[0, t, gg * PEER_GRP + p]) for p in range(PEER_GRP)],
                                      axis=0) for gg in (g, g + 1)]
            coef = jnp.where(first, jnp.broadcast_to(cwt[g:g + 1, :], (PACK, LANE)),
                             jnp.broadcast_to(cwt[g + 1:g + 2, :], (PACK, LANE)))
            lhs = jnp.where(keep, coef, 0.0).astype(BF16)
            acc = acc + jnp.dot(lhs, jnp.concatenate(halves, axis=1), preferred_element_type=F32)
        o_ref[t] = acc[:SUBLANE, :LANE] + acc[SUBLANE:, LANE:]
        return 0

    def token_group(i, carry):
        for j in range(PEER_TOK_UNROLL):
            token(i * PEER_TOK_UNROLL + j, carry)
        return carry

    lax.fori_loop(0, tb // PEER_TOK_UNROLL, token_group, 0)


def _peer_v(rows, cw, tab_v, tb):
    n_tok, groups, _ = cw.shape
    slots = groups * PEER_GRP
    return pl.pallas_call(
        functools.partial(_peer_v_kernel, slots=slots),
        grid=(n_tok // tb,),
        in_specs=[pl.BlockSpec((1, tb, slots), lambda i: (i, 0, 0), memory_space=pltpu.SMEM),
                  pl.BlockSpec((tb, groups, LANE), lambda i: (i, 0, 0)),
                  _table_spec(tab_v)],
        out_specs=pl.BlockSpec((tb, SUBLANE, LANE), lambda i: (i, 0, 0)),
        out_shape=jax.ShapeDtypeStruct((n_tok, SUBLANE, LANE), F32),
        compiler_params=pltpu.CompilerParams(
            dimension_semantics=("arbitrary",), vmem_limit_bytes=PEER_VMEM_LIMIT),
        name="peer_v",
    )(rows, cw, tab_v)


def _residual_kernel(h_ref, p_ref, gate_ref, o_ref):
    o_ref[...] = h_ref[...] + gate_ref[0] * p_ref[...]


def _residual(h2, p2, gate, t_len, tm=512):
    n_tok, d = h2.shape
    bpt = t_len // tm
    return pl.pallas_call(
        _residual_kernel,
        grid=(n_tok // tm,),
        in_specs=[pl.BlockSpec((tm, d), lambda i: (i, 0)), pl.BlockSpec((tm, d), lambda i: (i, 0)),
                  pl.BlockSpec((1, 1, d), lambda i: (i // bpt, 0, 0))],
        out_specs=pl.BlockSpec((tm, d), lambda i: (i, 0)),
        out_shape=jax.ShapeDtypeStruct((n_tok, d), F32),
        compiler_params=pltpu.CompilerParams(dimension_semantics=("parallel",)),
        name="peer_residual",
    )(h2, p2, gate)


def _layer(h2, mod, t_len, p):
    n_tok, d = h2.shape
    bsz = n_tok // t_len
    z = _inproj(h2, p["norm1_g"], mod[:, 1], mod[:, 0], _reorder_w_in(p["w_in"]), t_len)
    z3 = z.reshape(bsz, t_len, Z_COLS)
    ya = _dsa(z3, p["a_qn_g"], p["a_kn_g"]).reshape(n_tok, -1)
    yb = _stickbreak(z3, p["b_qn_g"], p["b_kn_g"]).reshape(n_tok, -1)
    yc = _gla(z3, p["c_w_alpha"], p["c_b_alpha"], p["c_on_g"]).reshape(n_tok, -1)
    h1, n2, q = _merge(h2, z, ya, yb, yc, p["b_gate"], p["w_oa"], p["w_ob"], p["w_oc"], p["w_out"],
                       mod[:, 2], p["norm2_g"], mod[:, 4], mod[:, 3], p["peer_wq"], t_len)
    eid, g = _peer_topk(q, p["peer_subkeys"])
    nblk, slots, tb = eid.shape
    rows = jnp.swapaxes((eid >> 1) * SUBLANE, 1, 2)
    cw = _peer_u(rows, eid, g, _dup_tiles(n2), _pair_tiles(p["peer_u"]), _expand_mats(slots))
    out = _peer_v(rows, cw.reshape(n_tok, slots // PEER_GRP, LANE), _pair_tiles(p["peer_v"]), tb)
    return _residual(h1, out.reshape(n_tok, d), mod[:, 5], t_len)


def kernel(x, c, norm1_g, norm2_g, w_mod, b_mod, w_in, b_gate, a_qn_g, a_kn_g, b_qn_g, b_kn_g, c_w_alpha, c_b_alpha, c_on_g, w_oa, w_ob, w_oc, w_out, peer_wq, peer_subkeys, peer_u, peer_v):
    bsz, t_len, d = x.shape
    params = dict(norm1_g=norm1_g, norm2_g=norm2_g, w_in=w_in, b_gate=b_gate, a_qn_g=a_qn_g, a_kn_g=a_kn_g,
                  b_qn_g=b_qn_g, b_kn_g=b_kn_g, c_w_alpha=c_w_alpha, c_b_alpha=c_b_alpha, c_on_g=c_on_g,
                  w_oa=w_oa, w_ob=w_ob, w_oc=w_oc, w_out=w_out, peer_wq=peer_wq, peer_subkeys=peer_subkeys,
                  peer_u=peer_u, peer_v=peer_v)
    h2 = x.reshape(bsz * t_len, d)
    for l in range(w_in.shape[0]):
        mod = _mod(c, w_mod[l], b_mod[l]).reshape(bsz, 6, 1, d)
        h2 = _layer(h2, mod, t_len, {k: v[l] for k, v in params.items()})
    return h2.reshape(bsz, t_len, d)
```

```python
import functools

import jax
import jax.numpy as jnp
import numpy as np
from jax import lax
from jax.experimental import pallas as pl
from jax.experimental.pallas import tpu as pltpu

F32 = jnp.float32
BF16 = jnp.bfloat16

D_MODEL = 1024
EPS = 1e-6
CHUNK = 64
TOPK_MAX = 256
A_HEADS, A_DIM = 8, 64
IDX_HEADS, IDX_DIM = 4, 64
B_HEADS, B_DIM = 8, 64
C_HEADS, C_KEY, C_VAL, C_RANK, C_TAU = 4, 64, 128, 16, 16.0
PEER_HEADS, PEER_NKEYS, PEER_QDIM, PEER_TOPK = 8, 128, 256, 16
N_BRANCH = 3

LANE = 128

_SEGMENTS = (
    ("gl", 3072), ("aq", 512), ("bq", 512), ("bk", 512), ("bv", 512), ("cv", 512), ("cr", 512),
    ("iq", 256), ("cq", 256), ("ck", 256), ("akv", 128), ("ikw", 128), ("ca", 128),
)
COL = {}
_off = 0
for _name, _w in _SEGMENTS:
    COL[_name] = (_off, _w)
    _off += _w
Z_COLS = _off
Z_TILE = Z_COLS // 3

_REF_WIDTHS = (512, 64, 64, 256, 64, 4, 512, 512, 512, 256, 256, 512, 512, 16, 3072)


def _reorder_w_in(w_in):
    offs = np.cumsum((0,) + _REF_WIDTHS)
    p = [w_in[:, offs[i]:offs[i + 1]] for i in range(len(_REF_WIDTHS))]
    aq, ak, av, iq, ik, iw, bq, bk, bv, cq, ck, cv, cr, ca, gl = p
    d = w_in.shape[0]
    z = lambda n: jnp.zeros((d, n), w_in.dtype)
    cols = [gl, aq, bq, bk, bv, cv, cr, iq, cq, ck, ak, av, ik, iw, z(60), ca, z(112)]
    return jnp.concatenate(cols, axis=1).astype(BF16)


def _cb(name):
    off, w = COL[name]
    return off // w


def _mod_kernel(c_ref, w_ref, b_ref, o_ref):
    c = c_ref[...]
    ca = (c * jax.nn.sigmoid(c)).astype(BF16)
    o_ref[...] = jnp.dot(ca, w_ref[...].astype(BF16), preferred_element_type=F32) + b_ref[...]


def _mod(c, w_mod, b_mod):
    bsz, d = c.shape
    n = w_mod.shape[1]
    tn = 768
    return pl.pallas_call(
        _mod_kernel,
        grid=(n // tn,),
        in_specs=[pl.BlockSpec((bsz, d), lambda j: (0, 0)),
                  pl.BlockSpec((d, tn), lambda j: (0, j)),
                  pl.BlockSpec((1, tn), lambda j: (0, j))],
        out_specs=pl.BlockSpec((bsz, tn), lambda j: (0, j)),
        out_shape=jax.ShapeDtypeStruct((bsz, n), F32),
        name="adaln_mod",
    )(c, w_mod, b_mod.reshape(1, n))


def _inproj_kernel(h_ref, g_ref, sc_ref, sh_ref, w_ref, z_ref, n_scr):
    @pl.when(pl.program_id(1) == 0)
    def _():
        x = h_ref[...]
        y = x * lax.rsqrt(jnp.mean(x * x, axis=-1, keepdims=True) + EPS)
        n = y * g_ref[...] * (1.0 + sc_ref[0]) + sh_ref[0]
        n_scr[...] = n.astype(BF16)

    z_ref[...] = jnp.dot(n_scr[...], w_ref[...], preferred_element_type=F32).astype(BF16)


def _inproj(h2, norm_g, scale, shift, w_z, t_len, tm=512):
    n_tok, d = h2.shape
    tm = min(tm, t_len)
    bpt = t_len // tm
    return pl.pallas_call(
        _inproj_kernel,
        grid=(n_tok // tm, Z_COLS // Z_TILE),
        in_specs=[pl.BlockSpec((tm, d), lambda i, j: (i, 0)),
                  pl.BlockSpec((1, d), lambda i, j: (0, 0)),
                  pl.BlockSpec((1, 1, d), lambda i, j: (i // bpt, 0, 0)),
                  pl.BlockSpec((1, 1, d), lambda i, j: (i // bpt, 0, 0)),
                  pl.BlockSpec((d, Z_TILE), lambda i, j: (0, j))],
        out_specs=pl.BlockSpec((tm, Z_TILE), lambda i, j: (i, j)),
        out_shape=jax.ShapeDtypeStruct((n_tok, Z_COLS), BF16),
        scratch_shapes=[pltpu.VMEM((tm, d), BF16)],
        compiler_params=pltpu.CompilerParams(
            dimension_semantics=("parallel", "arbitrary"), vmem_limit_bytes=48 * 1024 * 1024),
        name="norm_inproj",
    )(h2, norm_g.reshape(1, d), scale, shift, w_z)


def _head_rms(x, gain):
    return x * lax.rsqrt(jnp.mean(x * x, axis=-1, keepdims=True) + EPS) * gain


DSA_QB = 256
DSA_GROUP = 2
DSA_BISECT = 30


def _dsa_kernel(aq_ref, akv_ref, iq_ref, ikwq_ref, ikwk_ref, qg_ref, kg_ref, o_ref, sm_scr, sel_scr,
                *, q_off, n_sel):
    qb = aq_ref.shape[1]
    tk = akv_ref.shape[1]
    q0 = (q_off + pl.program_id(1)) * qb
    t_pos = q0 + lax.broadcasted_iota(jnp.int32, (qb, 1), 0)
    limit = (t_pos // CHUNK + 1) * CHUNK
    s_pos = lax.broadcasted_iota(jnp.int32, (1, tk), 1)
    adm = s_pos < limit

    kidx = ikwk_ref[0][:, :IDX_DIM]
    qidx = iq_ref[0]
    w = ikwq_ref[0][:, IDX_DIM:IDX_DIM + IDX_HEADS].astype(F32)
    score = jnp.zeros((qb, tk), F32)
    for h in range(IDX_HEADS):
        rel = lax.dot_general(qidx[:, h * IDX_DIM:(h + 1) * IDX_DIM], kidx,
                              (((1,), (1,)), ((), ())), preferred_element_type=F32)
        score = score + w[:, h:h + 1] * jnp.maximum(rel, 0.0)
    neg_inf = jnp.float32(-jnp.inf)
    sm_scr[...] = jnp.where(adm, score, neg_inf)

    hi0 = jnp.max(sm_scr[...], axis=-1, keepdims=True)
    lo0 = jnp.min(jnp.where(adm, score, jnp.float32(jnp.inf)), axis=-1, keepdims=True)
    kf = jnp.float32(n_sel)

    def bisect(_, carry):
        lo, hi = carry
        mid = 0.5 * (lo + hi)
        cnt = jnp.sum(jnp.where(sm_scr[...] > mid, 1.0, 0.0), axis=-1, keepdims=True)
        below = cnt < kf
        return jnp.where(below, lo, mid), jnp.where(below, mid, hi)

    _, hi = lax.fori_loop(0, DSA_BISECT, bisect, (lo0, hi0))
    sm = sm_scr[...]
    v1 = jnp.max(jnp.where(sm <= hi, sm, neg_inf), axis=-1, keepdims=True)
    c_ge1 = jnp.sum(jnp.where(sm >= v1, 1.0, 0.0), axis=-1, keepdims=True)
    v2 = jnp.max(jnp.where(sm < v1, sm, neg_inf), axis=-1, keepdims=True)
    thr = jnp.where(c_ge1 >= kf, v1, v2)
    need = kf - jnp.sum(jnp.where(sm > thr, 1.0, 0.0), axis=-1, keepdims=True)

    r_i = lax.broadcasted_iota(jnp.int32, (LANE, LANE), 0)
    c_i = lax.broadcasted_iota(jnp.int32, (LANE, LANE), 1)
    incl = jnp.where(r_i <= c_i, 1.0, 0.0).astype(BF16)
    all_sel = limit <= n_sel
    run = jnp.zeros((qb, 1), F32)
    for j in range(tk // LANE):
        sl = slice(j * LANE, (j + 1) * LANE)
        smj = sm_scr[:, sl]
        eqj = jnp.where(smj == thr, 1.0, 0.0)
        inc = jnp.dot(eqj.astype(BF16), incl, preferred_element_type=F32)
        rank = inc - eqj + run
        picked = (smj > thr) | ((smj == thr) & (rank < need))
        admj = (j * LANE + lax.broadcasted_iota(jnp.int32, (1, LANE), 1)) < limit
        selj = (all_sel & admj) | (jnp.logical_not(all_sel) & picked)
        sel_scr[:, sl] = jnp.where(selj, 0.0, neg_inf)
        run = run + inc[:, LANE - 1:LANE]

    kv = akv_ref[0].astype(F32)
    k = _head_rms(kv[:, :A_DIM], kg_ref[...]).astype(BF16)
    v = akv_ref[0][:, A_DIM:2 * A_DIM]
    aq = aq_ref[0].astype(F32)
    scale = A_DIM ** -0.5
    for h in range(A_HEADS):
        qh = _head_rms(aq[:, h * A_DIM:(h + 1) * A_DIM], qg_ref[...]).astype(BF16)
        logits = lax.dot_general(qh, k, (((1,), (1,)), ((), ())), preferred_element_type=F32) * scale
        logits = logits + sel_scr[...]
        m = jnp.max(logits, axis=-1, keepdims=True)
        p = jnp.exp(logits - m)
        denom = jnp.sum(p, axis=-1, keepdims=True)
        oh = jnp.dot(p.astype(BF16), v, preferred_element_type=F32) / denom
        o_ref[0, :, h * A_DIM:(h + 1) * A_DIM] = oh.astype(o_ref.dtype)


def _dsa(z3, qg, kg):
    bsz, t_len, _ = z3.shape
    n_sel = min(TOPK_MAX, t_len // 4)
    span = DSA_QB * DSA_GROUP
    outs = []
    for g in range(t_len // span):
        tk = span * (g + 1)
        q_off = g * DSA_GROUP
        qmap = lambda b, i, c: (b, q_off + i, c)
        outs.append(pl.pallas_call(
            functools.partial(_dsa_kernel, q_off=q_off, n_sel=n_sel),
            grid=(bsz, DSA_GROUP),
            in_specs=[pl.BlockSpec((1, DSA_QB, 512), lambda b, i, q_off=q_off: (b, q_off + i, _cb("aq"))),
                      pl.BlockSpec((1, tk, LANE), lambda b, i: (b, 0, _cb("akv"))),
                      pl.BlockSpec((1, DSA_QB, 256), lambda b, i, q_off=q_off: (b, q_off + i, _cb("iq"))),
                      pl.BlockSpec((1, DSA_QB, LANE), lambda b, i, q_off=q_off: (b, q_off + i, _cb("ikw"))),
                      pl.BlockSpec((1, tk, LANE), lambda b, i: (b, 0, _cb("ikw"))),
                      pl.BlockSpec((1, A_DIM), lambda b, i: (0, 0)),
                      pl.BlockSpec((1, A_DIM), lambda b, i: (0, 0))],
            out_specs=pl.BlockSpec((1, DSA_QB, 512), lambda b, i: (b, i, 0)),
            out_shape=jax.ShapeDtypeStruct((bsz, span, 512), BF16),
            scratch_shapes=[pltpu.VMEM((DSA_QB, tk), F32), pltpu.VMEM((DSA_QB, tk), F32)],
            compiler_params=pltpu.CompilerParams(
                dimension_semantics=("parallel", "arbitrary"), vmem_limit_bytes=48 * 1024 * 1024),
            name=f"dsa_attention_g{g}",
        )(z3, z3, z3, z3, z3, qg.reshape(1, A_DIM), kg.reshape(1, A_DIM)))
    return jnp.concatenate(outs, axis=1)


def _split_dot(x, m, x_is_lhs=True):
    hi = x.astype(BF16)
    lo = (x - hi.astype(F32)).astype(BF16)
    if x_is_lhs:
        return jnp.dot(hi, m, preferred_element_type=F32) + jnp.dot(lo, m, preferred_element_type=F32)
    return jnp.dot(m, hi, preferred_element_type=F32) + jnp.dot(m, lo, preferred_element_type=F32)


SB_QB = 256
SB_KB = 128
SB_SWEEP = 4


def _sb_kernel(q_ref, k_ref, v_ref, qg_ref, kg_ref, o_ref, kn_scr):
    qi = pl.program_id(2)
    qb = q_ref.shape[1]
    hd = B_DIM

    @pl.when(qi == 0)
    def _():
        kf = k_ref[0].astype(F32)
        for h in range(2):
            kn_scr[h] = _head_rms(kf[:, h * hd:(h + 1) * hd], kg_ref[...]).astype(BF16)

    q = q_ref[0].astype(F32)
    qn = [_head_rms(q[:, h * hd:(h + 1) * hd], qg_ref[...]).astype(BF16) for h in range(2)]
    kb = SB_KB
    r_i = lax.broadcasted_iota(jnp.int32, (kb, kb), 0)
    c_i = lax.broadcasted_iota(jnp.int32, (kb, kb), 1)
    m_incl = jnp.where(r_i >= c_i, 1.0, 0.0).astype(BF16)
    t_pos = qi * qb + lax.broadcasted_iota(jnp.int32, (qb, 1), 0)
    lane = lax.broadcasted_iota(jnp.int32, (1, 2 * hd), 1)
    scale = hd ** -0.5
    last_sweep = ((qi + 1) * (qb // kb) - 1) // SB_SWEEP

    def body(i, carry):
        sb = last_sweep - i
        carry = list(carry)
        blocks = list(reversed(range(SB_SWEEP)))
        starts = [pl.multiple_of((sb * SB_SWEEP + j) * kb, kb) for j in blocks]
        stricts = [(ks + lax.broadcasted_iota(jnp.int32, (1, kb), 1)) < t_pos for ks in starts]
        zs = [[lax.dot_general(qn[h], kn_scr[h, pl.ds(ks, kb), :], (((1,), (1,)), ((), ())),
                               preferred_element_type=F32) * scale for h in range(2)] for ks in starts]
        lbs = [[jnp.minimum(z, 0.0) - jnp.log(1.0 + jnp.exp(-jnp.abs(z))) for z in zb] for zb in zs]
        lks = [[jnp.where(st, lb - z, 0.0) for lb, z in zip(lbb, zb)]
               for st, lbb, zb in zip(stricts, lbs, zs)]
        incs = [[_split_dot(lk, m_incl) for lk in lkb] for lkb in lks]
        probs = []
        for st, lbb, lkb, incb in zip(stricts, lbs, lks, incs):
            row = []
            for h in range(2):
                later = incb[h] - lkb[h] + carry[2 * h]
                row.append(jnp.where(st, jnp.exp(lbb[h] + later), 0.0).astype(BF16))
                carry[2 * h] = carry[2 * h] + incb[h][:, 0:1]
            probs.append(row)
        for ks, row in zip(starts, probs):
            vblk = v_ref[0, pl.ds(ks, kb), :]
            for h in range(2):
                carry[2 * h + 1] = carry[2 * h + 1] + jnp.dot(row[h], vblk, preferred_element_type=F32)
        return tuple(carry)

    init = (jnp.zeros((qb, 1), F32), jnp.zeros((qb, 2 * hd), F32)) * 2
    res = lax.fori_loop(0, last_sweep + 1, body, init)
    o_ref[0] = jnp.where(lane < hd, res[1], res[3]).astype(o_ref.dtype)


def _stickbreak(z3, qg, kg):
    bsz, t_len, _ = z3.shape
    nq = t_len // SB_QB
    npair = B_HEADS // 2
    cq, ck, cv = COL["bq"][0] // LANE, COL["bk"][0] // LANE, COL["bv"][0] // LANE
    return pl.pallas_call(
        _sb_kernel,
        grid=(bsz, npair, nq),
        in_specs=[pl.BlockSpec((1, SB_QB, LANE), lambda b, p, i: (b, i, cq + p)),
                  pl.BlockSpec((1, t_len, LANE), lambda b, p, i: (b, 0, ck + p)),
                  pl.BlockSpec((1, t_len, LANE), lambda b, p, i: (b, 0, cv + p)),
                  pl.BlockSpec((1, B_DIM), lambda b, p, i: (0, 0)),
                  pl.BlockSpec((1, B_DIM), lambda b, p, i: (0, 0))],
        out_specs=pl.BlockSpec((1, SB_QB, LANE), lambda b, p, i: (b, i, p)),
        out_shape=jax.ShapeDtypeStruct((bsz, t_len, B_HEADS * B_DIM), BF16),
        scratch_shapes=[pltpu.VMEM((2, t_len, B_DIM), BF16)],
        compiler_params=pltpu.CompilerParams(
            dimension_semantics=("parallel", "parallel", "arbitrary"), vmem_limit_bytes=48 * 1024 * 1024),
        name="stickbreak_attention",
    )(z3, z3, z3, qg.reshape(1, B_DIM), kg.reshape(1, B_DIM))


GLA_SUB = 16
GLA_UNROLL = 4


def _gla_kernel(q_ref, k_ref, v_ref, r_ref, a_ref, wa_ref, ba_ref, og_ref, o_ref, st_scr):
    t_len = q_ref.shape[1]
    nc = t_len // CHUNK
    dk, dv, sub = C_KEY, C_VAL, GLA_SUB
    r_i = lax.broadcasted_iota(jnp.int32, (CHUNK, CHUNK), 0)
    c_i = lax.broadcasted_iota(jnp.int32, (CHUNK, CHUNK), 1)
    l_incl = jnp.where(r_i >= c_i, 1.0, 0.0).astype(BF16)
    st_scr[...] = jnp.zeros_like(st_scr)
    neg_inf = jnp.float32(-jnp.inf)
    pr = lax.broadcasted_iota(jnp.int32, (sub * sub, 1), 0)
    pair_ok = pr % sub <= pr // sub
    fr = lax.broadcasted_iota(jnp.int32, (sub, sub * sub), 0)
    fc = lax.broadcasted_iota(jnp.int32, (sub, sub * sub), 1)
    fold_t = jnp.where(fc // sub == fr, 1.0, 0.0).astype(BF16)
    hr = lax.broadcasted_iota(jnp.int32, (2 * dk, 2 * dv), 0)
    hc = lax.broadcasted_iota(jnp.int32, (2 * dk, 2 * dv), 1)
    head_ones = jnp.where(hr // dk == hc // dv, 1.0, 0.0).astype(BF16)

    def per_t(x):
        return jnp.broadcast_to(x[:, None, :], (sub, sub, x.shape[1])).reshape(sub * sub, x.shape[1])

    def per_s(x):
        return jnp.broadcast_to(x[None, :, :], (sub, sub, x.shape[1])).reshape(sub * sub, x.shape[1])

    def trip(i, _):
        rows = [pl.ds(pl.multiple_of((i * GLA_UNROLL + u) * CHUNK, CHUNK), CHUNK) for u in range(GLA_UNROLL)]
        pres = [jnp.dot(a_ref[0, r, :], wa_ref[...], preferred_element_type=F32) + ba_ref[...] for r in rows]
        la2s = [(jnp.minimum(p, 0.0) - jnp.log(1.0 + jnp.exp(-jnp.abs(p)))) * (1.0 / C_TAU) for p in pres]
        b2s = [_split_dot(la2, l_incl, x_is_lhs=False) for la2 in la2s]
        q2s = [q_ref[0, r, :].astype(F32) * (dk ** -0.5) for r in rows]
        k2s = [k_ref[0, r, :].astype(F32) for r in rows]
        v2fs = [v_ref[0, r, :].astype(F32) for r in rows]

        terms, vss = [], []
        for b2, q2, k2, v2f in zip(b2s, q2s, k2s, v2fs):
            for j in range(CHUNK // sub):
                sl = slice(j * sub, (j + 1) * sub)
                bt, qt = (per_t(x[sl]) for x in (b2, q2))
                bs, ks_, vs = (per_s(x[sl]) for x in (b2, k2, v2f))
                terms.append(qt * ks_ * jnp.exp(jnp.where(pair_ok, bt - bs, neg_inf)))
                vss.append(vs)
        w2s = [_split_dot(t, head_ones) for t in terms]
        diags = [jnp.dot(fold_t, (w2 * vs).astype(BF16), preferred_element_type=F32)
                 for w2, vs in zip(w2s, vss)]

        per_chunk = []
        for u in range(GLA_UNROLL):
            heads = []
            for h in range(2):
                b = b2s[u][:, h * dk:(h + 1) * dk]
                q = q2s[u][:, h * dk:(h + 1) * dk]
                k = k2s[u][:, h * dk:(h + 1) * dk]
                v = v_ref[0, rows[u], h * dv:(h + 1) * dv]
                atts = []
                for j in range(1, CHUNK // sub):
                    lo_r = j * sub
                    ref_row = b[lo_r - 1:lo_r, :]
                    qs = (q[lo_r:lo_r + sub] * jnp.exp(b[lo_r:lo_r + sub] - ref_row)).astype(BF16)
                    ks = (k[:lo_r] * jnp.exp(ref_row - b[:lo_r])).astype(BF16)
                    atts.append(lax.dot_general(qs, ks, (((1,), (1,)), ((), ())), preferred_element_type=F32))
                b_last = b[CHUNK - 1:CHUNK, :]
                heads.append(dict(
                    v=v, atts=atts, qe=(q * jnp.exp(b)).astype(BF16), decay=jnp.exp(b_last),
                    kd=(k * jnp.exp(b_last - b)).astype(BF16)))
            per_chunk.append(heads)
        for u in range(GLA_UNROLL):
            for h in range(2):
                hd = per_chunk[u][h]
                intra = [diags[u * (CHUNK // sub)][:, h * dv:(h + 1) * dv]]
                for j in range(1, CHUNK // sub):
                    lo_r = j * sub
                    intra.append(diags[u * (CHUNK // sub) + j][:, h * dv:(h + 1) * dv]
                                 + jnp.dot(hd["atts"][j - 1].astype(BF16), hd["v"][:lo_r], preferred_element_type=F32))
                hd["intra"] = jnp.concatenate(intra, axis=0)
                hd["upd"] = lax.dot_general(hd["v"], hd["kd"], (((0,), (0,)), ((), ())),
                                            preferred_element_type=F32)

        for h in range(2):
            st = st_scr[h]
            for u in range(GLA_UNROLL):
                hd = per_chunk[u][h]
                o = lax.dot_general(hd["qe"], st.astype(BF16), (((1,), (1,)), ((), ())),
                                    preferred_element_type=F32) + hd["intra"]
                st = st * hd["decay"] + hd["upd"]
                y = o * lax.rsqrt(jnp.mean(o * o, axis=-1, keepdims=True) + EPS) * og_ref[:, h * dv:(h + 1) * dv]
                r = r_ref[0, rows[u], h * dv:(h + 1) * dv].astype(F32)
                o_ref[0, rows[u], h * dv:(h + 1) * dv] = (y * (r * jax.nn.sigmoid(r))).astype(o_ref.dtype)
            st_scr[h] = st
        return 0

    lax.fori_loop(0, nc // GLA_UNROLL, trip, 0)


def _gla(z3, w_alpha, b_alpha, out_gain):
    bsz, t_len, _ = z3.shape
    npair = C_HEADS // 2
    wa = jnp.zeros((LANE, C_HEADS * C_KEY), F32).at[:C_RANK].set(w_alpha).astype(BF16)
    cq, ck = COL["cq"][0] // LANE, COL["ck"][0] // LANE
    cv, cr, ca = COL["cv"][0] // 256, COL["cr"][0] // 256, COL["ca"][0] // LANE
    return pl.pallas_call(
        _gla_kernel,
        grid=(bsz, npair),
        in_specs=[pl.BlockSpec((1, t_len, LANE), lambda b, p: (b, 0, cq + p)),
                  pl.BlockSpec((1, t_len, LANE), lambda b, p: (b, 0, ck + p)),
                  pl.BlockSpec((1, t_len, 256), lambda b, p: (b, 0, cv + p)),
                  pl.BlockSpec((1, t_len, 256), lambda b, p: (b, 0, cr + p)),
                  pl.BlockSpec((1, t_len, LANE), lambda b, p: (b, 0, ca)),
                  pl.BlockSpec((LANE, LANE), lambda b, p: (0, p)),
                  pl.BlockSpec((1, LANE), lambda b, p: (0, p)),
                  pl.BlockSpec((1, 256), lambda b, p: (0, p))],
        out_specs=pl.BlockSpec((1, t_len, 256), lambda b, p: (b, 0, p)),
        out_shape=jax.ShapeDtypeStruct((bsz, t_len, C_HEADS * C_VAL), BF16),
        scratch_shapes=[pltpu.VMEM((2, C_VAL, C_KEY), F32)],
        compiler_params=pltpu.CompilerParams(
            dimension_semantics=("parallel", "parallel"), vmem_limit_bytes=48 * 1024 * 1024),
        name="gla_attention",
    )(z3, z3, z3, z3, z3, wa, b_alpha.reshape(1, -1), out_gain.reshape(1, -1))


def _merge_kernel(h_ref, g0_ref, g1_ref, g2_ref, bg_ref, ya_ref, yb_ref, yc_ref, woa_ref, wob_ref, woc_ref,
                  wout_ref, gate_ref, ng_ref, sc_ref, sh_ref, wpq_ref, h1_ref, n2_ref, q_ref):
    d = h_ref.shape[1]
    merged = jnp.zeros(h_ref.shape, F32)
    for i, (g_ref, y_ref, w_ref) in enumerate(((g0_ref, ya_ref, woa_ref), (g1_ref, yb_ref, wob_ref),
                                               (g2_ref, yc_ref, woc_ref))):
        gate = jax.nn.sigmoid(g_ref[...].astype(F32) + bg_ref[:, i * d:(i + 1) * d])
        merged = merged + gate * jnp.dot(y_ref[...], w_ref[...], preferred_element_type=F32)
    y = jnp.dot(merged.astype(BF16), wout_ref[...], preferred_element_type=F32)
    h1 = h_ref[...] + gate_ref[0] * y
    h1_ref[...] = h1
    n = h1 * lax.rsqrt(jnp.mean(h1 * h1, axis=-1, keepdims=True) + EPS)
    n2 = (n * ng_ref[...] * (1.0 + sc_ref[0]) + sh_ref[0]).astype(BF16)
    n2_ref[...] = n2
    q_ref[...] = jnp.dot(n2, wpq_ref[...], preferred_element_type=F32).astype(BF16)


def _merge(h2, z, ya, yb, yc, b_gate, w_oa, w_ob, w_oc, w_out, gate1, norm_g, scale, shift, w_pq, t_len, tm=256):
    n_tok, d = h2.shape
    bpt = t_len // tm
    nq = w_pq.shape[1]
    row = lambda i: (i, 0)
    fixed = lambda i: (0, 0)
    per_b = lambda i: (i // bpt, 0, 0)
    bw = ya.shape[1]
    return pl.pallas_call(
        _merge_kernel,
        grid=(n_tok // tm,),
        in_specs=[pl.BlockSpec((tm, d), row),
                  pl.BlockSpec((tm, d), lambda i: (i, 0)),
                  pl.BlockSpec((tm, d), lambda i: (i, 1)),
                  pl.BlockSpec((tm, d), lambda i: (i, 2)),
                  pl.BlockSpec((1, N_BRANCH * d), fixed),
                  pl.BlockSpec((tm, bw), row), pl.BlockSpec((tm, bw), row), pl.BlockSpec((tm, bw), row),
                  pl.BlockSpec((bw, d), fixed), pl.BlockSpec((bw, d), fixed), pl.BlockSpec((bw, d), fixed),
                  pl.BlockSpec((d, d), fixed),
                  pl.BlockSpec((1, 1, d), per_b),
                  pl.BlockSpec((1, d), fixed),
                  pl.BlockSpec((1, 1, d), per_b), pl.BlockSpec((1, 1, d), per_b),
                  pl.BlockSpec((d, nq), fixed)],
        out_specs=[pl.BlockSpec((tm, d), row), pl.BlockSpec((tm, d), row), pl.BlockSpec((tm, nq), row)],
        out_shape=[jax.ShapeDtypeStruct((n_tok, d), F32), jax.ShapeDtypeStruct((n_tok, d), BF16),
                   jax.ShapeDtypeStruct((n_tok, nq), BF16)],
        compiler_params=pltpu.CompilerParams(
            dimension_semantics=("parallel",), vmem_limit_bytes=48 * 1024 * 1024),
        name="merge_outproj_peerq",
    )(h2, z, z, z, b_gate.reshape(1, -1), ya, yb, yc, w_oa.astype(BF16), w_ob.astype(BF16), w_oc.astype(BF16),
      w_out.astype(BF16), gate1, norm_g.reshape(1, d), scale, shift, w_pq.astype(BF16))


PEER_TB = 128


def _take_top(vals, ids, k):
    neg_inf = jnp.float32(-jnp.inf)
    big = jnp.float32(2.0 ** 30)
    out_v, out_i = [], []
    for _ in range(k):
        m = jnp.max(vals, axis=0, keepdims=True)
        first = jnp.min(jnp.where(vals == m, ids, big), axis=0, keepdims=True)
        out_v.append(m)
        out_i.append(first)
        vals = jnp.where(ids == first, neg_inf, vals)
    return out_v, out_i


def _peer_topk_kernel(q_ref, sk_ref, eid_ref, g_ref):
    tb = q_ref.shape[0]
    hq = PEER_QDIM // 2
    key_id = lax.broadcasted_iota(jnp.int32, (PEER_NKEYS, tb), 0).astype(F32)
    k = PEER_TOPK
    n_exp = PEER_NKEYS * PEER_NKEYS
    n_j = [k if i == 0 else 8 for i in range(k)]
    pos = jnp.concatenate([(i * k + lax.broadcasted_iota(jnp.int32, (n_j[i], tb), 0)) * n_exp
                           for i in range(k)], axis=0).astype(F32)
    for h in range(PEER_HEADS):
        tops, topi = [], []
        for p in range(2):
            qhp = q_ref[:, (2 * h + p) * hq:(2 * h + p + 1) * hq]
            s = lax.dot_general(sk_ref[h, p], qhp, (((1,), (1,)), ((), ())), preferred_element_type=F32)
            v, ids = _take_top(s, key_id, k)
            tops.append(v)
            topi.append(ids)
        b_s = jnp.concatenate(tops[1], axis=0)
        b_i = jnp.concatenate(topi[1], axis=0)
        cand = jnp.concatenate([tops[0][i] + b_s[:n_j[i]] for i in range(k)], axis=0)
        cid = jnp.concatenate([topi[0][i] * PEER_NKEYS + b_i[:n_j[i]] for i in range(k)], axis=0)
        best, tag = _take_top(cand, pos + cid, k)
        best = jnp.concatenate(best, axis=0)
        e = jnp.exp(best - best[0:1])
        g_ref[0, h * k:(h + 1) * k, :] = e / jnp.sum(e, axis=0, keepdims=True)
        eid_ref[0, h * k:(h + 1) * k, :] = jnp.concatenate(tag, axis=0).astype(jnp.int32) % n_exp


def _peer_topk(q, sub_keys):
    n_tok = q.shape[0]
    nblk = n_tok // PEER_TB
    slots = PEER_HEADS * PEER_TOPK
    return pl.pallas_call(
        _peer_topk_kernel,
        grid=(nblk,),
        in_specs=[pl.BlockSpec((PEER_TB, q.shape[1]), lambda i: (i, 0)),
                  pl.BlockSpec(sub_keys.shape, lambda i: (0, 0, 0, 0))],
        out_specs=[pl.BlockSpec((1, slots, PEER_TB), lambda i: (i, 0, 0)),
                   pl.BlockSpec((1, slots, PEER_TB), lambda i: (i, 0, 0))],
        out_shape=[jax.ShapeDtypeStruct((nblk, slots, PEER_TB), jnp.int32),
                   jax.ShapeDtypeStruct((nblk, slots, PEER_TB), F32)],
        compiler_params=pltpu.CompilerParams(dimension_semantics=("parallel",)),
        name="peer_topk",
    )(q, sub_keys.astype(BF16))


PEER_VMEM_LIMIT = 50 * 1024 * 1024
SUBLANE = 8
PACK = 2 * SUBLANE
PEER_GRP = LANE // PACK
PEER_TOK_UNROLL = 8


def _pair_tiles(tab):
    e, d = tab.shape
    bits = lax.bitcast_convert_type(tab.astype(BF16), jnp.uint16).astype(jnp.uint32)
    bits = bits.reshape(e // 2, 2, d // LANE, LANE)
    return (bits[:, 0] | (bits[:, 1] << 16)).reshape(e // 2 * SUBLANE, LANE)


def _tile(tab_ref, word_row):
    words = tab_ref[pl.ds(pl.multiple_of(word_row, SUBLANE), SUBLANE), :]
    return pltpu.bitcast(words, BF16)


def _dup_tiles(x2):
    n, d = x2.shape
    x4 = x2.reshape(n, d // LANE, 1, LANE)
    return jnp.broadcast_to(x4, (n, d // LANE, 2, LANE)).reshape(n, PACK, LANE)


def _expand_mats(slots):
    col = np.arange(slots * PACK)
    s_of = (col // LANE) * PEER_GRP + (col % LANE) // PACK
    h_of = col % 2
    m = np.zeros((2, slots, slots * PACK), np.float32)
    for h in range(2):
        m[h, s_of[h_of == h], col[h_of == h]] = 1.0
    return jnp.asarray(m, BF16)


def _table_spec(tab):
    return pl.BlockSpec(tab.shape, lambda i: (0, 0), pipeline_mode=pl.Buffered(1))


def _peer_u_kernel(rows_ref, xx_ref, eid_ref, g_ref, ex_ref, tab_ref, cw_ref, ae_scr, ao_scr):
    tb = xx_ref.shape[0]
    slots = ae_scr.shape[0]
    lane = lax.broadcasted_iota(jnp.int32, (SUBLANE, tb), 1)
    r_i = lax.broadcasted_iota(jnp.int32, (PACK, LANE), 0)
    k_i = lax.broadcasted_iota(jnp.int32, (PACK, LANE), 1)
    sel = jnp.where((k_i // PACK == r_i % SUBLANE) & (k_i % 2 == r_i // SUBLANE), 1.0, 0.0).astype(BF16)

    def token(t, _):
        xt = xx_ref[t]
        xg = jnp.concatenate([xt] * PEER_GRP, axis=0)
        here = lane == t
        for g in range(0, slots // PEER_GRP, 2):
            halves = []
            for gg in (g, g + 1):
                tiles = [_tile(tab_ref, rows_ref[0, t, gg * PEER_GRP + p]) for p in range(PEER_GRP)]
                halves.append(jnp.concatenate(tiles, axis=0) * xg)
            part = jnp.dot(sel, jnp.concatenate(halves, axis=1), preferred_element_type=F32)
            for j, gg in enumerate((g, g + 1)):
                tot = jnp.sum(part[:, j * LANE:(j + 1) * LANE], axis=1, keepdims=True)
                rs = slice(gg * PEER_GRP, (gg + 1) * PEER_GRP)
                ae_scr[rs, :] = jnp.where(here, tot[:SUBLANE], ae_scr[rs, :])
                ao_scr[rs, :] = jnp.where(here, tot[SUBLANE:], ao_scr[rs, :])
        return 0

    def token_group(i, carry):
        for j in range(PEER_TOK_UNROLL):
            token(i * PEER_TOK_UNROLL + j, carry)
        return carry

    lax.fori_loop(0, tb // PEER_TOK_UNROLL, token_group, 0)
    odd = (eid_ref[0] & 1) == 1
    act = jnp.where(odd, ao_scr[...], ae_scr[...])
    coef = g_ref[0] * (0.5 * act * (1.0 + lax.erf(act * (2.0 ** -0.5))))
    zero = jnp.zeros_like(coef)
    tn = (((0,), (0,)), ((), ()))
    cw = lax.dot_general(jnp.where(odd, zero, coef).astype(BF16), ex_ref[0], tn, preferred_element_type=F32)
    cw = cw + lax.dot_general(jnp.where(odd, coef, zero).astype(BF16), ex_ref[1], tn, preferred_element_type=F32)
    cw_ref[...] = cw


def _peer_u(rows, eid, g, xx, tab_u, expand):
    nblk, slots, tb = eid.shape
    n_tok = xx.shape[0]
    return pl.pallas_call(
        _peer_u_kernel,
        grid=(nblk,),
        in_specs=[pl.BlockSpec((1, tb, slots), lambda i: (i, 0, 0), memory_space=pltpu.SMEM),
                  pl.BlockSpec((tb,) + xx.shape[1:], lambda i: (i, 0, 0)),
                  pl.BlockSpec((1, slots, tb), lambda i: (i, 0, 0)),
                  pl.BlockSpec((1, slots, tb), lambda i: (i, 0, 0)),
                  pl.BlockSpec(expand.shape, lambda i: (0, 0, 0)),
                  _table_spec(tab_u)],
        out_specs=pl.BlockSpec((tb, slots * PACK), lambda i: (i, 0)),
        out_shape=jax.ShapeDtypeStruct((n_tok, slots * PACK), F32),
        scratch_shapes=[pltpu.VMEM((slots, tb), F32), pltpu.VMEM((slots, tb), F32)],
        compiler_params=pltpu.CompilerParams(
            dimension_semantics=("arbitrary",), vmem_limit_bytes=PEER_VMEM_LIMIT),
        name="peer_u",
    )(rows, xx, eid, g, expand, tab_u)


def _peer_v_kernel(rows_ref, cw_ref, tab_ref, o_ref, *, slots):
    tb = o_ref.shape[0]
    r_i = lax.broadcasted_iota(jnp.int32, (PACK, LANE), 0)
    k_i = lax.broadcasted_iota(jnp.int32, (PACK, LANE), 1)
    keep = (k_i % PACK) // 2 == r_i % SUBLANE
    first = r_i < SUBLANE

    def token(t, _):
        acc = jnp.zeros((PACK, 2 * LANE), F32)
        cwt = cw_ref[t]
        for g in range(0, slots // PEER_GRP, 2):
            halves = [jnp.concatenate([_tile(tab_ref, rows_ref[0, t, gg * PEER_GRP + p]) for p in range(PEER_GRP)],
                                      axis=0) for gg in (g, g + 1)]
            coef = jnp.where(first, jnp.broadcast_to(cwt[g:g + 1, :], (PACK, LANE)),
                             jnp.broadcast_to(cwt[g + 1:g + 2, :], (PACK, LANE)))
            lhs = jnp.where(keep, coef, 0.0).astype(BF16)
            acc = acc + jnp.dot(lhs, jnp.concatenate(halves, axis=1), preferred_element_type=F32)
        o_ref[t] = acc[:SUBLANE, :LANE] + acc[SUBLANE:, LANE:]
        return 0

    def token_group(i, carry):
        for j in range(PEER_TOK_UNROLL):
            token(i * PEER_TOK_UNROLL + j, carry)
        return carry

    lax.fori_loop(0, tb // PEER_TOK_UNROLL, token_group, 0)


def _peer_v(rows, cw, tab_v, tb):
    n_tok, groups, _ = cw.shape
    slots = groups * PEER_GRP
    return pl.pallas_call(
        functools.partial(_peer_v_kernel, slots=slots),
        grid=(n_tok // tb,),
        in_specs=[pl.BlockSpec((1, tb, slots), lambda i: (i, 0, 0), memory_space=pltpu.SMEM),
                  pl.BlockSpec((tb, groups, LANE), lambda i: (i, 0, 0)),
                  _table_spec(tab_v)],
        out_specs=pl.BlockSpec((tb, SUBLANE, LANE), lambda i: (i, 0, 0)),
        out_shape=jax.ShapeDtypeStruct((n_tok, SUBLANE, LANE), F32),
        compiler_params=pltpu.CompilerParams(
            dimension_semantics=("arbitrary",), vmem_limit_bytes=PEER_VMEM_LIMIT),
        name="peer_v",
    )(rows, cw, tab_v)


def _residual_kernel(h_ref, p_ref, gate_ref, o_ref):
    o_ref[...] = h_ref[...] + gate_ref[0] * p_ref[...]


def _residual(h2, p2, gate, t_len, tm=512):
    n_tok, d = h2.shape
    bpt = t_len // tm
    return pl.pallas_call(
        _residual_kernel,
        grid=(n_tok // tm,),
        in_specs=[pl.BlockSpec((tm, d), lambda i: (i, 0)), pl.BlockSpec((tm, d), lambda i: (i, 0)),
                  pl.BlockSpec((1, 1, d), lambda i: (i // bpt, 0, 0))],
        out_specs=pl.BlockSpec((tm, d), lambda i: (i, 0)),
        out_shape=jax.ShapeDtypeStruct((n_tok, d), F32),
        compiler_params=pltpu.CompilerParams(dimension_semantics=("parallel",)),
        name="peer_residual",
    )(h2, p2, gate)


def _layer(h2, mod, t_len, p):
    n_tok, d = h2.shape
    bsz = n_tok // t_len
    z = _inproj(h2, p["norm1_g"], mod[:, 1], mod[:, 0], _reorder_w_in(p["w_in"]), t_len)
    z3 = z.reshape(bsz, t_len, Z_COLS)
    ya = _dsa(z3, p["a_qn_g"], p["a_kn_g"]).reshape(n_tok, -1)
    yb = _stickbreak(z3, p["b_qn_g"], p["b_kn_g"]).reshape(n_tok, -1)
    yc = _gla(z3, p["c_w_alpha"], p["c_b_alpha"], p["c_on_g"]).reshape(n_tok, -1)
    h1, n2, q = _merge(h2, z, ya, yb, yc, p["b_gate"], p["w_oa"], p["w_ob"], p["w_oc"], p["w_out"],
                       mod[:, 2], p["norm2_g"], mod[:, 4], mod[:, 3], p["peer_wq"], t_len)
    eid, g = _peer_topk(q, p["peer_subkeys"])
    nblk, slots, tb = eid.shape
    rows = jnp.swapaxes((eid >> 1) * SUBLANE, 1, 2)
    cw = _peer_u(rows, eid, g, _dup_tiles(n2), _pair_tiles(p["peer_u"]), _expand_mats(slots))
    out = _peer_v(rows, cw.reshape(n_tok, slots // PEER_GRP, LANE), _pair_tiles(p["peer_v"]), tb)
    return _residual(h1, out.reshape(n_tok, d), mod[:, 5], t_len)


def kernel(x, c, norm1_g, norm2_g, w_mod, b_mod, w_in, b_gate, a_qn_g, a_kn_g, b_qn_g, b_kn_g, c_w_alpha, c_b_alpha, c_on_g, w_oa, w_ob, w_oc, w_out, peer_wq, peer_subkeys, peer_u, peer_v):
    bsz, t_len, d = x.shape
    params = dict(norm1_g=norm1_g, norm2_g=norm2_g, w_in=w_in, b_gate=b_gate, a_qn_g=a_qn_g, a_kn_g=a_kn_g,
                  b_qn_g=b_qn_g, b_kn_g=b_kn_g, c_w_alpha=c_w_alpha, c_b_alpha=c_b_alpha, c_on_g=c_on_g,
                  w_oa=w_oa, w_ob=w_ob, w_oc=w_oc, w_out=w_out, peer_wq=peer_wq, peer_subkeys=peer_subkeys,
                  peer_u=peer_u, peer_v=peer_v)
    h2 = x.reshape(bsz * t_len, d)
    for l in range(w_in.shape[0]):
        mod = _mod(c, w_mod[l], b_mod[l]).reshape(bsz, 6, 1, d)
        h2 = _layer(h2, mod, t_len, {k: v[l] for k, v in params.items()})
    return h2.reshape(bsz, t_len, d)
```

```python
import functools

import jax
import jax.numpy as jnp
import numpy as np
from jax import lax
from jax.experimental import pallas as pl
from jax.experimental.pallas import tpu as pltpu

F32 = jnp.float32
BF16 = jnp.bfloat16

D_MODEL = 1024
EPS = 1e-6
CHUNK = 64
TOPK_MAX = 256
A_HEADS, A_DIM = 8, 64
IDX_HEADS, IDX_DIM = 4, 64
B_HEADS, B_DIM = 8, 64
C_HEADS, C_KEY, C_VAL, C_RANK, C_TAU = 4, 64, 128, 16, 16.0
PEER_HEADS, PEER_NKEYS, PEER_QDIM, PEER_TOPK = 8, 128, 256, 16
N_BRANCH = 3

LANE = 128

_SEGMENTS = (
    ("gl", 3072), ("aq", 512), ("bq", 512), ("bk", 512), ("bv", 512), ("cv", 512), ("cr", 512),
    ("iq", 256), ("cq", 256), ("ck", 256), ("akv", 128), ("ikw", 128), ("ca", 128),
)
COL = {}
_off = 0
for _name, _w in _SEGMENTS:
    COL[_name] = (_off, _w)
    _off += _w
Z_COLS = _off
Z_TILE = Z_COLS // 3

_REF_WIDTHS = (512, 64, 64, 256, 64, 4, 512, 512, 512, 256, 256, 512, 512, 16, 3072)


def _reorder_w_in(w_in):
    offs = np.cumsum((0,) + _REF_WIDTHS)
    p = [w_in[:, offs[i]:offs[i + 1]] for i in range(len(_REF_WIDTHS))]
    aq, ak, av, iq, ik, iw, bq, bk, bv, cq, ck, cv, cr, ca, gl = p
    d = w_in.shape[0]
    z = lambda n: jnp.zeros((d, n), w_in.dtype)
    cols = [gl, aq, bq, bk, bv, cv, cr, iq, cq, ck, ak, av, ik, iw, z(60), ca, z(112)]
    return jnp.concatenate(cols, axis=1).astype(BF16)


def _cb(name):
    off, w = COL[name]
    return off // w


def _mod_kernel(c_ref, w_ref, b_ref, o_ref):
    c = c_ref[...]
    ca = (c * jax.nn.sigmoid(c)).astype(BF16)
    o_ref[...] = jnp.dot(ca, w_ref[...].astype(BF16), preferred_element_type=F32) + b_ref[...]


def _mod(c, w_mod, b_mod):
    bsz, d = c.shape
    n = w_mod.shape[1]
    tn = 768
    return pl.pallas_call(
        _mod_kernel,
        grid=(n // tn,),
        in_specs=[pl.BlockSpec((bsz, d), lambda j: (0, 0)),
                  pl.BlockSpec((d, tn), lambda j: (0, j)),
                  pl.BlockSpec((1, tn), lambda j: (0, j))],
        out_specs=pl.BlockSpec((bsz, tn), lambda j: (0, j)),
        out_shape=jax.ShapeDtypeStruct((bsz, n), F32),
        name="adaln_mod",
    )(c, w_mod, b_mod.reshape(1, n))


def _inproj_kernel(h_ref, g_ref, sc_ref, sh_ref, w_ref, z_ref, n_scr):
    @pl.when(pl.program_id(1) == 0)
    def _():
        x = h_ref[...]
        y = x * lax.rsqrt(jnp.mean(x * x, axis=-1, keepdims=True) + EPS)
        n = y * g_ref[...] * (1.0 + sc_ref[0]) + sh_ref[0]
        n_scr[...] = n.astype(BF16)

    z_ref[...] = jnp.dot(n_scr[...], w_ref[...], preferred_element_type=F32).astype(BF16)


def _inproj(h2, norm_g, scale, shift, w_z, t_len, tm=512):
    n_tok, d = h2.shape
    tm = min(tm, t_len)
    bpt = t_len // tm
    return pl.pallas_call(
        _inproj_kernel,
        grid=(n_tok // tm, Z_COLS // Z_TILE),
        in_specs=[pl.BlockSpec((tm, d), lambda i, j: (i, 0)),
                  pl.BlockSpec((1, d), lambda i, j: (0, 0)),
                  pl.BlockSpec((1, 1, d), lambda i, j: (i // bpt, 0, 0)),
                  pl.BlockSpec((1, 1, d), lambda i, j: (i // bpt, 0, 0)),
                  pl.BlockSpec((d, Z_TILE), lambda i, j: (0, j))],
        out_specs=pl.BlockSpec((tm, Z_TILE), lambda i, j: (i, j)),
        out_shape=jax.ShapeDtypeStruct((n_tok, Z_COLS), BF16),
        scratch_shapes=[pltpu.VMEM((tm, d), BF16)],
        compiler_params=pltpu.CompilerParams(
            dimension_semantics=("parallel", "arbitrary"), vmem_limit_bytes=48 * 1024 * 1024),
        name="norm_inproj",
    )(h2, norm_g.reshape(1, d), scale, shift, w_z)


def _head_rms(x, gain):
    return x * lax.rsqrt(jnp.mean(x * x, axis=-1, keepdims=True) + EPS) * gain


DSA_QB = 256
DSA_GROUP = 2
DSA_BISECT = 30


def _dsa_kernel(aq_ref, akv_ref, iq_ref, ikwq_ref, ikwk_ref, qg_ref, kg_ref, o_ref, sm_scr, sel_scr,
                *, q_off, n_sel):
    qb = aq_ref.shape[1]
    tk = akv_ref.shape[1]
    q0 = (q_off + pl.program_id(1)) * qb
    t_pos = q0 + lax.broadcasted_iota(jnp.int32, (qb, 1), 0)
    limit = (t_pos // CHUNK + 1) * CHUNK
    s_pos = lax.broadcasted_iota(jnp.int32, (1, tk), 1)
    adm = s_pos < limit

    kidx = ikwk_ref[0][:, :IDX_DIM]
    qidx = iq_ref[0]
    w = ikwq_ref[0][:, IDX_DIM:IDX_DIM + IDX_HEADS].astype(F32)
    score = jnp.zeros((qb, tk), F32)
    for h in range(IDX_HEADS):
        rel = lax.dot_general(qidx[:, h * IDX_DIM:(h + 1) * IDX_DIM], kidx,
                              (((1,), (1,)), ((), ())), preferred_element_type=F32)
        score = score + w[:, h:h + 1] * jnp.maximum(rel, 0.0)
    neg_inf = jnp.float32(-jnp.inf)
    sm_scr[...] = jnp.where(adm, score, neg_inf)

    hi0 = jnp.max(sm_scr[...], axis=-1, keepdims=True)
    lo0 = jnp.min(jnp.where(adm, score, jnp.float32(jnp.inf)), axis=-1, keepdims=True)
    kf = jnp.float32(n_sel)

    def bisect(_, carry):
        lo, hi = carry
        mid = 0.5 * (lo + hi)
        cnt = jnp.sum(jnp.where(sm_scr[...] > mid, 1.0, 0.0), axis=-1, keepdims=True)
        below = cnt < kf
        return jnp.where(below, lo, mid), jnp.where(below, mid, hi)

    _, hi = lax.fori_loop(0, DSA_BISECT, bisect, (lo0, hi0))
    sm = sm_scr[...]
    v1 = jnp.max(jnp.where(sm <= hi, sm, neg_inf), axis=-1, keepdims=True)
    c_ge1 = jnp.sum(jnp.where(sm >= v1, 1.0, 0.0), axis=-1, keepdims=True)
    v2 = jnp.max(jnp.where(sm < v1, sm, neg_inf), axis=-1, keepdims=True)
    c_ge2 = jnp.sum(jnp.where(sm >= v2, 1.0, 0.0), axis=-1, keepdims=True)
    v3 = jnp.max(jnp.where(sm < v2, sm, neg_inf), axis=-1, keepdims=True)
    thr = jnp.where(c_ge1 >= kf, v1, jnp.where(c_ge2 >= kf, v2, v3))
    need = kf - jnp.sum(jnp.where(sm > thr, 1.0, 0.0), axis=-1, keepdims=True)

    r_i = lax.broadcasted_iota(jnp.int32, (LANE, LANE), 0)
    c_i = lax.broadcasted_iota(jnp.int32, (LANE, LANE), 1)
    incl = jnp.where(r_i <= c_i, 1.0, 0.0).astype(BF16)
    all_sel = limit <= n_sel
    run = jnp.zeros((qb, 1), F32)
    for j in range(tk // LANE):
        sl = slice(j * LANE, (j + 1) * LANE)
        smj = sm_scr[:, sl]
        eqj = jnp.where(smj == thr, 1.0, 0.0)
        inc = jnp.dot(eqj.astype(BF16), incl, preferred_element_type=F32)
        rank = inc - eqj + run
        picked = (smj > thr) | ((smj == thr) & (rank < need))
        admj = (j * LANE + lax.broadcasted_iota(jnp.int32, (1, LANE), 1)) < limit
        selj = (all_sel & admj) | (jnp.logical_not(all_sel) & picked)
        sel_scr[:, sl] = jnp.where(selj, 0.0, neg_inf)
        run = run + inc[:, LANE - 1:LANE]

    kv = akv_ref[0].astype(F32)
    k = _head_rms(kv[:, :A_DIM], kg_ref[...]).astype(BF16)
    v = akv_ref[0][:, A_DIM:2 * A_DIM]
    aq = aq_ref[0].astype(F32)
    scale = A_DIM ** -0.5
    for h in range(A_HEADS):
        qh = (_head_rms(aq[:, h * A_DIM:(h + 1) * A_DIM], qg_ref[...]) * scale).astype(BF16)
        logits = lax.dot_general(qh, k, (((1,), (1,)), ((), ())), preferred_element_type=F32) + sel_scr[...]
        m = jnp.max(logits, axis=-1, keepdims=True)
        p = jnp.exp(logits - m)
        denom = jnp.sum(p, axis=-1, keepdims=True)
        oh = jnp.dot(p.astype(BF16), v, preferred_element_type=F32) / denom
        o_ref[0, :, h * A_DIM:(h + 1) * A_DIM] = oh.astype(o_ref.dtype)


def _dsa(z3, qg, kg):
    bsz, t_len, _ = z3.shape
    n_sel = min(TOPK_MAX, t_len // 4)
    span = DSA_QB * DSA_GROUP
    outs = []
    for g in range(t_len // span):
        tk = span * (g + 1)
        q_off = g * DSA_GROUP
        outs.append(pl.pallas_call(
            functools.partial(_dsa_kernel, q_off=q_off, n_sel=n_sel),
            grid=(bsz, DSA_GROUP),
            in_specs=[pl.BlockSpec((1, DSA_QB, 512), lambda b, i, q_off=q_off: (b, q_off + i, _cb("aq"))),
                      pl.BlockSpec((1, tk, LANE), lambda b, i: (b, 0, _cb("akv"))),
                      pl.BlockSpec((1, DSA_QB, 256), lambda b, i, q_off=q_off: (b, q_off + i, _cb("iq"))),
                      pl.BlockSpec((1, DSA_QB, LANE), lambda b, i, q_off=q_off: (b, q_off + i, _cb("ikw"))),
                      pl.BlockSpec((1, tk, LANE), lambda b, i: (b, 0, _cb("ikw"))),
                      pl.BlockSpec((1, A_DIM), lambda b, i: (0, 0)),
                      pl.BlockSpec((1, A_DIM), lambda b, i: (0, 0))],
            out_specs=pl.BlockSpec((1, DSA_QB, 512), lambda b, i: (b, i, 0)),
            out_shape=jax.ShapeDtypeStruct((bsz, span, 512), BF16),
            scratch_shapes=[pltpu.VMEM((DSA_QB, tk), F32), pltpu.VMEM((DSA_QB, tk), F32)],
            compiler_params=pltpu.CompilerParams(
                dimension_semantics=("parallel", "arbitrary"), vmem_limit_bytes=48 * 1024 * 1024),
            name=f"dsa_attention_g{g}",
        )(z3, z3, z3, z3, z3, qg.reshape(1, A_DIM), kg.reshape(1, A_DIM)))
    return jnp.concatenate(outs, axis=1)


def _split_dot(x, m, x_is_lhs=True):
    hi = x.astype(BF16)
    lo = (x - hi.astype(F32)).astype(BF16)
    if x_is_lhs:
        return jnp.dot(hi, m, preferred_element_type=F32) + jnp.dot(lo, m, preferred_element_type=F32)
    return jnp.dot(m, hi, preferred_element_type=F32) + jnp.dot(m, lo, preferred_element_type=F32)


def _split_dot2(x, m2):
    hi = x.astype(BF16)
    lo = (x - hi.astype(F32)).astype(BF16)
    return jnp.dot(jnp.concatenate([hi, lo], axis=1), m2, preferred_element_type=F32)


SB_QB = 256
SB_KB = 128
SB_SWEEP = 4


def _sb_kernel(q_ref, k_ref, v_ref, qg_ref, kg_ref, o_ref, kn_scr):
    qi = pl.program_id(2)
    qb = q_ref.shape[1]
    hd = B_DIM

    @pl.when(qi == 0)
    def _():
        kf = k_ref[0].astype(F32)
        for h in range(2):
            kn_scr[h] = _head_rms(kf[:, h * hd:(h + 1) * hd], kg_ref[...]).astype(BF16)

    q = q_ref[0].astype(F32)
    scale = hd ** -0.5
    qn = [(_head_rms(q[:, h * hd:(h + 1) * hd], qg_ref[...]) * scale).astype(BF16) for h in range(2)]
    kb = SB_KB
    r_i = lax.broadcasted_iota(jnp.int32, (kb, kb), 0)
    c_i = lax.broadcasted_iota(jnp.int32, (kb, kb), 1)
    m_incl = jnp.where(r_i >= c_i, 1.0, 0.0).astype(BF16)
    m_incl2 = jnp.concatenate([m_incl, m_incl], axis=0)
    t_pos = qi * qb + lax.broadcasted_iota(jnp.int32, (qb, 1), 0)
    lane = lax.broadcasted_iota(jnp.int32, (1, 2 * hd), 1)
    last_sweep = ((qi + 1) * (qb // kb) - 1) // SB_SWEEP

    def body(i, carry):
        sb = last_sweep - i
        carry = list(carry)
        blocks = list(reversed(range(SB_SWEEP)))
        starts = [pl.multiple_of((sb * SB_SWEEP + j) * kb, kb) for j in blocks]
        stricts = [(ks + lax.broadcasted_iota(jnp.int32, (1, kb), 1)) < t_pos for ks in starts]
        zs = [[lax.dot_general(qn[h], kn_scr[h, pl.ds(ks, kb), :], (((1,), (1,)), ((), ())),
                               preferred_element_type=F32) for h in range(2)] for ks in starts]
        drops = [[jnp.where(st, jnp.maximum(z, 0.0) + jnp.log(1.0 + jnp.exp(-jnp.abs(z))), 0.0) for z in zb]
                 for st, zb in zip(stricts, zs)]
        sums = [[_split_dot2(d, m_incl2) for d in db] for db in drops]
        probs = []
        for st, zb, sb_ in zip(stricts, zs, sums):
            row = []
            for h in range(2):
                row.append(jnp.where(st, jnp.exp(zb[h] - sb_[h] - carry[2 * h]), 0.0).astype(BF16))
                carry[2 * h] = carry[2 * h] + sb_[h][:, 0:1]
            probs.append(row)
        for ks, row in zip(starts, probs):
            vblk = v_ref[0, pl.ds(ks, kb), :]
            for h in range(2):
                carry[2 * h + 1] = carry[2 * h + 1] + jnp.dot(row[h], vblk, preferred_element_type=F32)
        return tuple(carry)

    init = (jnp.zeros((qb, 1), F32), jnp.zeros((qb, 2 * hd), F32)) * 2
    res = lax.fori_loop(0, last_sweep + 1, body, init)
    o_ref[0] = jnp.where(lane < hd, res[1], res[3]).astype(o_ref.dtype)


def _stickbreak(z3, qg, kg):
    bsz, t_len, _ = z3.shape
    nq = t_len // SB_QB
    npair = B_HEADS // 2
    cq, ck, cv = COL["bq"][0] // LANE, COL["bk"][0] // LANE, COL["bv"][0] // LANE
    return pl.pallas_call(
        _sb_kernel,
        grid=(bsz, npair, nq),
        in_specs=[pl.BlockSpec((1, SB_QB, LANE), lambda b, p, i: (b, i, cq + p)),
                  pl.BlockSpec((1, t_len, LANE), lambda b, p, i: (b, 0, ck + p)),
                  pl.BlockSpec((1, t_len, LANE), lambda b, p, i: (b, 0, cv + p)),
                  pl.BlockSpec((1, B_DIM), lambda b, p, i: (0, 0)),
                  pl.BlockSpec((1, B_DIM), lambda b, p, i: (0, 0))],
        out_specs=pl.BlockSpec((1, SB_QB, LANE), lambda b, p, i: (b, i, p)),
        out_shape=jax.ShapeDtypeStruct((bsz, t_len, B_HEADS * B_DIM), BF16),
        scratch_shapes=[pltpu.VMEM((2, t_len, B_DIM), BF16)],
        compiler_params=pltpu.CompilerParams(
            dimension_semantics=("parallel", "parallel", "arbitrary"), vmem_limit_bytes=48 * 1024 * 1024),
        name="stickbreak_attention",
    )(z3, z3, z3, qg.reshape(1, B_DIM), kg.reshape(1, B_DIM))


GLA_SUB = 16
GLA_UNROLL = 4


def _gla_kernel(q_ref, k_ref, v_ref, r_ref, a_ref, wa_ref, ba_ref, og_ref, o_ref, st_scr):
    t_len = q_ref.shape[1]
    nc = t_len // CHUNK
    dk, dv, sub = C_KEY, C_VAL, GLA_SUB
    r_i = lax.broadcasted_iota(jnp.int32, (CHUNK, CHUNK), 0)
    c_i = lax.broadcasted_iota(jnp.int32, (CHUNK, CHUNK), 1)
    l_incl = jnp.where(r_i >= c_i, 1.0, 0.0).astype(BF16)
    st_scr[...] = jnp.zeros_like(st_scr)
    neg_inf = jnp.float32(-jnp.inf)
    pr = lax.broadcasted_iota(jnp.int32, (sub * sub, 1), 0)
    pair_ok = pr % sub <= pr // sub
    fr = lax.broadcasted_iota(jnp.int32, (sub, sub * sub), 0)
    fc = lax.broadcasted_iota(jnp.int32, (sub, sub * sub), 1)
    fold_t = jnp.where(fc // sub == fr, 1.0, 0.0).astype(BF16)
    hr = lax.broadcasted_iota(jnp.int32, (2 * dk, 2 * dv), 0)
    hc = lax.broadcasted_iota(jnp.int32, (2 * dk, 2 * dv), 1)
    head_ones = jnp.where(hr // dk == hc // dv, 1.0, 0.0).astype(BF16)

    def per_t(x):
        return jnp.broadcast_to(x[:, None, :], (sub, sub, x.shape[1])).reshape(sub * sub, x.shape[1])

    def per_s(x):
        return jnp.broadcast_to(x[None, :, :], (sub, sub, x.shape[1])).reshape(sub * sub, x.shape[1])

    def trip(i, _):
        rows = [pl.ds(pl.multiple_of((i * GLA_UNROLL + u) * CHUNK, CHUNK), CHUNK) for u in range(GLA_UNROLL)]
        pres = [jnp.dot(a_ref[0, r, :], wa_ref[...], preferred_element_type=F32) + ba_ref[...] for r in rows]
        la2s = [(jnp.minimum(p, 0.0) - jnp.log(1.0 + jnp.exp(-jnp.abs(p)))) * (1.0 / C_TAU) for p in pres]
        b2s = [_split_dot(la2, l_incl, x_is_lhs=False) for la2 in la2s]
        q2s = [q_ref[0, r, :].astype(F32) * (dk ** -0.5) for r in rows]
        k2s = [k_ref[0, r, :].astype(F32) for r in rows]
        v2fs = [v_ref[0, r, :].astype(F32) for r in rows]

        terms, vss = [], []
        for b2, q2, k2, v2f in zip(b2s, q2s, k2s, v2fs):
            for j in range(CHUNK // sub):
                sl = slice(j * sub, (j + 1) * sub)
                bt, qt = (per_t(x[sl]) for x in (b2, q2))
                bs, ks_, vs = (per_s(x[sl]) for x in (b2, k2, v2f))
                terms.append(qt * ks_ * jnp.exp(jnp.where(pair_ok, bt - bs, neg_inf)))
                vss.append(vs)
        w2s = [_split_dot(t, head_ones) for t in terms]
        diags = [jnp.dot(fold_t, (w2 * vs).astype(BF16), preferred_element_type=F32)
                 for w2, vs in zip(w2s, vss)]

        per_chunk = []
        for u in range(GLA_UNROLL):
            heads = []
            for h in range(2):
                b = b2s[u][:, h * dk:(h + 1) * dk]
                q = q2s[u][:, h * dk:(h + 1) * dk]
                k = k2s[u][:, h * dk:(h + 1) * dk]
                v = v_ref[0, rows[u], h * dv:(h + 1) * dv]
                atts = []
                for j in range(1, CHUNK // sub):
                    lo_r = j * sub
                    ref_row = b[lo_r - 1:lo_r, :]
                    qs = (q[lo_r:lo_r + sub] * jnp.exp(b[lo_r:lo_r + sub] - ref_row)).astype(BF16)
                    ks = (k[:lo_r] * jnp.exp(ref_row - b[:lo_r])).astype(BF16)
                    atts.append(lax.dot_general(qs, ks, (((1,), (1,)), ((), ())), preferred_element_type=F32))
                b_last = b[CHUNK - 1:CHUNK, :]
                heads.append(dict(
                    v=v, atts=atts, qe=(q * jnp.exp(b)).astype(BF16), decay=jnp.exp(b_last),
                    kd=(k * jnp.exp(b_last - b)).astype(BF16)))
            per_chunk.append(heads)
        for u in range(GLA_UNROLL):
            for h in range(2):
                hd = per_chunk[u][h]
                intra = [diags[u * (CHUNK // sub)][:, h * dv:(h + 1) * dv]]
                for j in range(1, CHUNK // sub):
                    lo_r = j * sub
                    intra.append(diags[u * (CHUNK // sub) + j][:, h * dv:(h + 1) * dv]
                                 + jnp.dot(hd["atts"][j - 1].astype(BF16), hd["v"][:lo_r], preferred_element_type=F32))
                hd["intra"] = jnp.concatenate(intra, axis=0)
                hd["upd"] = lax.dot_general(hd["v"], hd["kd"], (((0,), (0,)), ((), ())),
                                            preferred_element_type=F32)

        for h in range(2):
            st = st_scr[h]
            for u in range(GLA_UNROLL):
                hd = per_chunk[u][h]
                o = lax.dot_general(hd["qe"], st.astype(BF16), (((1,), (1,)), ((), ())),
                                    preferred_element_type=F32) + hd["intra"]
                st = st * hd["decay"] + hd["upd"]
                y = o * lax.rsqrt(jnp.mean(o * o, axis=-1, keepdims=True) + EPS) * og_ref[:, h * dv:(h + 1) * dv]
                r = r_ref[0, rows[u], h * dv:(h + 1) * dv].astype(F32)
                o_ref[0, rows[u], h * dv:(h + 1) * dv] = (y * (r * jax.nn.sigmoid(r))).astype(o_ref.dtype)
            st_scr[h] = st
        return 0

    lax.fori_loop(0, nc // GLA_UNROLL, trip, 0)


def _gla(z3, w_alpha, b_alpha, out_gain):
    bsz, t_len, _ = z3.shape
    npair = C_HEADS // 2
    wa = jnp.zeros((LANE, C_HEADS * C_KEY), F32).at[:C_RANK].set(w_alpha).astype(BF16)
    cq, ck = COL["cq"][0] // LANE, COL["ck"][0] // LANE
    cv, cr, ca = COL["cv"][0] // 256, COL["cr"][0] // 256, COL["ca"][0] // LANE
    return pl.pallas_call(
        _gla_kernel,
        grid=(bsz, npair),
        in_specs=[pl.BlockSpec((1, t_len, LANE), lambda b, p: (b, 0, cq + p)),
                  pl.BlockSpec((1, t_len, LANE), lambda b, p: (b, 0, ck + p)),
                  pl.BlockSpec((1, t_len, 256), lambda b, p: (b, 0, cv + p)),
                  pl.BlockSpec((1, t_len, 256), lambda b, p: (b, 0, cr + p)),
                  pl.BlockSpec((1, t_len, LANE), lambda b, p: (b, 0, ca)),
                  pl.BlockSpec((LANE, LANE), lambda b, p: (0, p)),
                  pl.BlockSpec((1, LANE), lambda b, p: (0, p)),
                  pl.BlockSpec((1, 256), lambda b, p: (0, p))],
        out_specs=pl.BlockSpec((1, t_len, 256), lambda b, p: (b, 0, p)),
        out_shape=jax.ShapeDtypeStruct((bsz, t_len, C_HEADS * C_VAL), BF16),
        scratch_shapes=[pltpu.VMEM((2, C_VAL, C_KEY), F32)],
        compiler_params=pltpu.CompilerParams(
            dimension_semantics=("parallel", "parallel"), vmem_limit_bytes=48 * 1024 * 1024),
        name="gla_attention",
    )(z3, z3, z3, z3, z3, wa, b_alpha.reshape(1, -1), out_gain.reshape(1, -1))


def _merge_kernel(h_ref, g0_ref, g1_ref, g2_ref, bg_ref, ya_ref, yb_ref, yc_ref, woa_ref, wob_ref, woc_ref,
                  wout_ref, gate_ref, ng_ref, sc_ref, sh_ref, wpq_ref, h1_ref, n2_ref, q_ref):
    d = h_ref.shape[1]
    merged = jnp.zeros(h_ref.shape, F32)
    for i, (g_ref, y_ref, w_ref) in enumerate(((g0_ref, ya_ref, woa_ref), (g1_ref, yb_ref, wob_ref),
                                               (g2_ref, yc_ref, woc_ref))):
        gate = jax.nn.sigmoid(g_ref[...].astype(F32) + bg_ref[:, i * d:(i + 1) * d])
        merged = merged + gate * jnp.dot(y_ref[...], w_ref[...], preferred_element_type=F32)
    y = jnp.dot(merged.astype(BF16), wout_ref[...], preferred_element_type=F32)
    h1 = h_ref[...] + gate_ref[0] * y
    h1_ref[...] = h1
    n = h1 * lax.rsqrt(jnp.mean(h1 * h1, axis=-1, keepdims=True) + EPS)
    n2 = (n * ng_ref[...] * (1.0 + sc_ref[0]) + sh_ref[0]).astype(BF16)
    n2_ref[...] = n2
    q_ref[...] = jnp.dot(n2, wpq_ref[...], preferred_element_type=F32).astype(BF16)


def _merge(h2, z, ya, yb, yc, b_gate, w_oa, w_ob, w_oc, w_out, gate1, norm_g, scale, shift, w_pq, t_len, tm=256):
    n_tok, d = h2.shape
    bpt = t_len // tm
    nq = w_pq.shape[1]
    row = lambda i: (i, 0)
    fixed = lambda i: (0, 0)
    per_b = lambda i: (i // bpt, 0, 0)
    bw = ya.shape[1]
    return pl.pallas_call(
        _merge_kernel,
        grid=(n_tok // tm,),
        in_specs=[pl.BlockSpec((tm, d), row),
                  pl.BlockSpec((tm, d), lambda i: (i, 0)),
                  pl.BlockSpec((tm, d), lambda i: (i, 1)),
                  pl.BlockSpec((tm, d), lambda i: (i, 2)),
                  pl.BlockSpec((1, N_BRANCH * d), fixed),
                  pl.BlockSpec((tm, bw), row), pl.BlockSpec((tm, bw), row), pl.BlockSpec((tm, bw), row),
                  pl.BlockSpec((bw, d), fixed), pl.BlockSpec((bw, d), fixed), pl.BlockSpec((bw, d), fixed),
                  pl.BlockSpec((d, d), fixed),
                  pl.BlockSpec((1, 1, d), per_b),
                  pl.BlockSpec((1, d), fixed),
                  pl.BlockSpec((1, 1, d), per_b), pl.BlockSpec((1, 1, d), per_b),
                  pl.BlockSpec((d, nq), fixed)],
        out_specs=[pl.BlockSpec((tm, d), row), pl.BlockSpec((tm, d), row), pl.BlockSpec((tm, nq), row)],
        out_shape=[jax.ShapeDtypeStruct((n_tok, d), F32), jax.ShapeDtypeStruct((n_tok, d), BF16),
                   jax.ShapeDtypeStruct((n_tok, nq), BF16)],
        compiler_params=pltpu.CompilerParams(
            dimension_semantics=("parallel",), vmem_limit_bytes=48 * 1024 * 1024),
        name="merge_outproj_peerq",
    )(h2, z, z, z, b_gate.reshape(1, -1), ya, yb, yc, w_oa.astype(BF16), w_ob.astype(BF16), w_oc.astype(BF16),
      w_out.astype(BF16), gate1, norm_g.reshape(1, d), scale, shift, w_pq.astype(BF16))


PEER_TB = 128


def _take_top(vals, ids, k):
    neg_inf = jnp.float32(-jnp.inf)
    big = jnp.float32(2.0 ** 30)
    out_v, out_i = [], []
    for _ in range(k):
        m = jnp.max(vals, axis=0, keepdims=True)
        first = jnp.min(jnp.where(vals == m, ids, big), axis=0, keepdims=True)
        out_v.append(m)
        out_i.append(first)
        vals = jnp.where(ids == first, neg_inf, vals)
    return out_v, out_i


def _peer_topk_kernel(q_ref, sk_ref, eid_ref, g_ref):
    tb = q_ref.shape[0]
    hq = PEER_QDIM // 2
    key_id = lax.broadcasted_iota(jnp.int32, (PEER_NKEYS, tb), 0).astype(F32)
    k = PEER_TOPK
    n_exp = PEER_NKEYS * PEER_NKEYS
    n_j = [k if i == 0 else 8 for i in range(k)]
    pos = jnp.concatenate([(i * k + lax.broadcasted_iota(jnp.int32, (n_j[i], tb), 0)) * n_exp
                           for i in range(k)], axis=0).astype(F32)
    for h in range(PEER_HEADS):
        tops, topi = [], []
        for p in range(2):
            qhp = q_ref[:, (2 * h + p) * hq:(2 * h + p + 1) * hq]
            s = lax.dot_general(sk_ref[h, p], qhp, (((1,), (1,)), ((), ())), preferred_element_type=F32)
            v, ids = _take_top(s, key_id, k)
            tops.append(v)
            topi.append(ids)
        b_s = jnp.concatenate(tops[1], axis=0)
        b_i = jnp.concatenate(topi[1], axis=0)
        cand = jnp.concatenate([tops[0][i] + b_s[:n_j[i]] for i in range(k)], axis=0)
        cid = jnp.concatenate([topi[0][i] * PEER_NKEYS + b_i[:n_j[i]] for i in range(k)], axis=0)
        best, tag = _take_top(cand, pos + cid, k)
        best = jnp.concatenate(best, axis=0)
        e = jnp.exp(best - best[0:1])
        g_ref[0, h * k:(h + 1) * k, :] = e / jnp.sum(e, axis=0, keepdims=True)
        eid_ref[0, h * k:(h + 1) * k, :] = jnp.concatenate(tag, axis=0).astype(jnp.int32) % n_exp


def _peer_topk(q, sub_keys):
    n_tok = q.shape[0]
    nblk = n_tok // PEER_TB
    slots = PEER_HEADS * PEER_TOPK
    return pl.pallas_call(
        _peer_topk_kernel,
        grid=(nblk,),
        in_specs=[pl.BlockSpec((PEER_TB, q.shape[1]), lambda i: (i, 0)),
                  pl.BlockSpec(sub_keys.shape, lambda i: (0, 0, 0, 0))],
        out_specs=[pl.BlockSpec((1, slots, PEER_TB), lambda i: (i, 0, 0)),
                   pl.BlockSpec((1, slots, PEER_TB), lambda i: (i, 0, 0))],
        out_shape=[jax.ShapeDtypeStruct((nblk, slots, PEER_TB), jnp.int32),
                   jax.ShapeDtypeStruct((nblk, slots, PEER_TB), F32)],
        compiler_params=pltpu.CompilerParams(dimension_semantics=("parallel",)),
        name="peer_topk",
    )(q, sub_keys.astype(BF16))


PEER_VMEM_LIMIT = 50 * 1024 * 1024
SUBLANE = 8
PACK = 2 * SUBLANE
PEER_GRP = LANE // PACK
PEER_TOK_UNROLL = 8


def _pair_tiles(tab):
    e, d = tab.shape
    bits = lax.bitcast_convert_type(tab.astype(BF16), jnp.uint16).astype(jnp.uint32)
    bits = bits.reshape(e // 2, 2, d // LANE, LANE)
    return (bits[:, 0] | (bits[:, 1] << 16)).reshape(e // 2 * SUBLANE, LANE)


def _tile(tab_ref, word_row):
    words = tab_ref[pl.ds(pl.multiple_of(word_row, SUBLANE), SUBLANE), :]
    return pltpu.bitcast(words, BF16)


def _dup_tiles(x2):
    n, d = x2.shape
    x4 = x2.reshape(n, d // LANE, 1, LANE)
    return jnp.broadcast_to(x4, (n, d // LANE, 2, LANE)).reshape(n, PACK, LANE)


def _expand_mats(slots):
    col = np.arange(slots * PACK)
    s_of = (col // LANE) * PEER_GRP + (col % LANE) // PACK
    h_of = col % 2
    m = np.zeros((2, slots, slots * PACK), np.float32)
    for h in range(2):
        m[h, s_of[h_of == h], col[h_of == h]] = 1.0
    return jnp.asarray(m, BF16)


def _table_spec(tab):
    return pl.BlockSpec(tab.shape, lambda i: (0, 0), pipeline_mode=pl.Buffered(1))


PEER_STAGE = 2


def _index_stage(slots):
    return [pltpu.SMEM((PEER_STAGE, PEER_TOK_UNROLL, slots), jnp.int32), pltpu.SemaphoreType.DMA((PEER_STAGE,))]


def _for_each_token(rows_ref, idx_smem, sems, tb, token_fn):
    u = PEER_TOK_UNROLL
    ntrip = tb // u

    def stage(trip, buf):
        src = rows_ref.at[0, pl.ds(pl.multiple_of(trip * u, u), u), :]
        return pltpu.make_async_copy(src, idx_smem.at[buf], sems.at[buf])

    for b in range(PEER_STAGE):
        stage(b, b).start()

    def outer(j, carry):
        for b in range(PEER_STAGE):
            trip = j * PEER_STAGE + b
            stage(trip, b).wait()
            for tt in range(u):
                token_fn(trip * u + tt, lambda s, b=b, tt=tt: idx_smem[b, tt, s])

            @pl.when(trip + PEER_STAGE < ntrip)
            def _():
                stage(trip + PEER_STAGE, b).start()
        return carry

    lax.fori_loop(0, ntrip // PEER_STAGE, outer, 0)


def _peer_u_kernel(rows_ref, xx_ref, eid_ref, g_ref, ex_ref, tab_ref, cw_ref, ae_scr, ao_scr, idx_smem, sems):
    tb = xx_ref.shape[0]
    slots = ae_scr.shape[0]
    lane = lax.broadcasted_iota(jnp.int32, (SUBLANE, tb), 1)
    r_i = lax.broadcasted_iota(jnp.int32, (PACK, LANE), 0)
    k_i = lax.broadcasted_iota(jnp.int32, (PACK, LANE), 1)
    sel = jnp.where((k_i // PACK == r_i % SUBLANE) & (k_i % 2 == r_i // SUBLANE), 1.0, 0.0).astype(BF16)

    def token(t, idx):
        xt = xx_ref[t]
        xg = jnp.concatenate([xt] * PEER_GRP, axis=0)
        here = lane == t
        for g in range(0, slots // PEER_GRP, 2):
            halves = []
            for gg in (g, g + 1):
                tiles = [_tile(tab_ref, idx(gg * PEER_GRP + p)) for p in range(PEER_GRP)]
                halves.append(jnp.concatenate(tiles, axis=0) * xg)
            part = jnp.dot(sel, jnp.concatenate(halves, axis=1), preferred_element_type=F32)
            for j, gg in enumerate((g, g + 1)):
                tot = jnp.sum(part[:, j * LANE:(j + 1) * LANE], axis=1, keepdims=True)
                rs = slice(gg * PEER_GRP, (gg + 1) * PEER_GRP)
                ae_scr[rs, :] = jnp.where(here, tot[:SUBLANE], ae_scr[rs, :])
                ao_scr[rs, :] = jnp.where(here, tot[SUBLANE:], ao_scr[rs, :])

    _for_each_token(rows_ref, idx_smem, sems, tb, token)
    odd = (eid_ref[0] & 1) == 1
    act = jnp.where(odd, ao_scr[...], ae_scr[...])
    coef = g_ref[0] * (0.5 * act * (1.0 + lax.erf(act * (2.0 ** -0.5))))
    zero = jnp.zeros_like(coef)
    tn = (((0,), (0,)), ((), ()))
    cw = lax.dot_general(jnp.where(odd, zero, coef).astype(BF16), ex_ref[0], tn, preferred_element_type=F32)
    cw = cw + lax.dot_general(jnp.where(odd, coef, zero).astype(BF16), ex_ref[1], tn, preferred_element_type=F32)
    cw_ref[...] = cw


def _peer_u(rows, eid, g, xx, tab_u, expand):
    nblk, slots, tb = eid.shape
    n_tok = xx.shape[0]
    return pl.pallas_call(
        _peer_u_kernel,
        grid=(nblk,),
        in_specs=[pl.BlockSpec((1, tb, slots), lambda i: (i, 0, 0)),
                  pl.BlockSpec((tb,) + xx.shape[1:], lambda i: (i, 0, 0)),
                  pl.BlockSpec((1, slots, tb), lambda i: (i, 0, 0)),
                  pl.BlockSpec((1, slots, tb), lambda i: (i, 0, 0)),
                  pl.BlockSpec(expand.shape, lambda i: (0, 0, 0)),
                  _table_spec(tab_u)],
        out_specs=pl.BlockSpec((tb, slots * PACK), lambda i: (i, 0)),
        out_shape=jax.ShapeDtypeStruct((n_tok, slots * PACK), F32),
        scratch_shapes=[pltpu.VMEM((slots, tb), F32), pltpu.VMEM((slots, tb), F32)] + _index_stage(slots),
        compiler_params=pltpu.CompilerParams(
            dimension_semantics=("arbitrary",), vmem_limit_bytes=PEER_VMEM_LIMIT),
        name="peer_u",
    )(rows, xx, eid, g, expand, tab_u)


def _peer_v_kernel(rows_ref, cw_ref, tab_ref, o_ref, idx_smem, sems, *, slots):
    tb = o_ref.shape[0]
    r_i = lax.broadcasted_iota(jnp.int32, (PACK, LANE), 0)
    k_i = lax.broadcasted_iota(jnp.int32, (PACK, LANE), 1)
    keep = (k_i % PACK) // 2 == r_i % SUBLANE
    first = r_i < SUBLANE

    def token(t, idx):
        acc = jnp.zeros((PACK, 2 * LANE), F32)
        cwt = cw_ref[t]
        for g in range(0, slots // PEER_GRP, 2):
            halves = [jnp.concatenate([_tile(tab_ref, idx(gg * PEER_GRP + p)) for p in range(PEER_GRP)],
                                      axis=0) for gg in (g, g + 1)]
            coef = jnp.where(first, jnp.broadcast_to(cwt[g:g + 1, :], (PACK, LANE)),
                             jnp.broadcast_to(cwt[g + 1:g + 2, :], (PACK, LANE)))
            lhs = jnp.where(keep, coef, 0.0).astype(BF16)
            acc = acc + jnp.dot(lhs, jnp.concatenate(halves, axis=1), preferred_element_type=F32)
        o_ref[t] = acc[:SUBLANE, :LANE] + acc[SUBLANE:, LANE:]

    _for_each_token(rows_ref, idx_smem, sems, tb, token)


def _peer_v(rows, cw, tab_v, tb):
    n_tok, groups, _ = cw.shape
    slots = groups * PEER_GRP
    return pl.pallas_call(
        functools.partial(_peer_v_kernel, slots=slots),
        grid=(n_tok // tb,),
        in_specs=[pl.BlockSpec((1, tb, slots), lambda i: (i, 0, 0)),
                  pl.BlockSpec((tb, groups, LANE), lambda i: (i, 0, 0)),
                  _table_spec(tab_v)],
        out_specs=pl.BlockSpec((tb, SUBLANE, LANE), lambda i: (i, 0, 0)),
        out_shape=jax.ShapeDtypeStruct((n_tok, SUBLANE, LANE), F32),
        scratch_shapes=_index_stage(slots),
        compiler_params=pltpu.CompilerParams(
            dimension_semantics=("arbitrary",), vmem_limit_bytes=PEER_VMEM_LIMIT),
        name="peer_v",
    )(rows, cw, tab_v)


def _residual_kernel(h_ref, p_ref, gate_ref, o_ref):
    o_ref[...] = h_ref[...] + gate_ref[0] * p_ref[...]


def _residual(h2, p2, gate, t_len, tm=512):
    n_tok, d = h2.shape
    bpt = t_len // tm
    return pl.pallas_call(
        _residual_kernel,
        grid=(n_tok // tm,),
        in_specs=[pl.BlockSpec((tm, d), lambda i: (i, 0)), pl.BlockSpec((tm, d), lambda i: (i, 0)),
                  pl.BlockSpec((1, 1, d), lambda i: (i // bpt, 0, 0))],
        out_specs=pl.BlockSpec((tm, d), lambda i: (i, 0)),
        out_shape=jax.ShapeDtypeStruct((n_tok, d), F32),
        compiler_params=pltpu.CompilerParams(dimension_semantics=("parallel",)),
        name="peer_residual",
    )(h2, p2, gate)


def _layer(h2, mod, t_len, p):
    n_tok, d = h2.shape
    bsz = n_tok // t_len
    z = _inproj(h2, p["norm1_g"], mod[:, 1], mod[:, 0], _reorder_w_in(p["w_in"]), t_len)
    z3 = z.reshape(bsz, t_len, Z_COLS)
    ya = _dsa(z3, p["a_qn_g"], p["a_kn_g"]).reshape(n_tok, -1)
    yb = _stickbreak(z3, p["b_qn_g"], p["b_kn_g"]).reshape(n_tok, -1)
    yc = _gla(z3, p["c_w_alpha"], p["c_b_alpha"], p["c_on_g"]).reshape(n_tok, -1)
    h1, n2, q = _merge(h2, z, ya, yb, yc, p["b_gate"], p["w_oa"], p["w_ob"], p["w_oc"], p["w_out"],
                       mod[:, 2], p["norm2_g"], mod[:, 4], mod[:, 3], p["peer_wq"], t_len)
    eid, g = _peer_topk(q, p["peer_subkeys"])
    nblk, slots, tb = eid.shape
    rows = jnp.swapaxes((eid >> 1) * SUBLANE, 1, 2)
    cw = _peer_u(rows, eid, g, _dup_tiles(n2), _pair_tiles(p["peer_u"]), _expand_mats(slots))
    out = _peer_v(rows, cw.reshape(n_tok, slots // PEER_GRP, LANE), _pair_tiles(p["peer_v"]), tb)
    return _residual(h1, out.reshape(n_tok, d), mod[:, 5], t_len)


def kernel(x, c, norm1_g, norm2_g, w_mod, b_mod, w_in, b_gate, a_qn_g, a_kn_g, b_qn_g, b_kn_g, c_w_alpha, c_b_alpha, c_on_g, w_oa, w_ob, w_oc, w_out, peer_wq, peer_subkeys, peer_u, peer_v):
    bsz, t_len, d = x.shape
    params = dict(norm1_g=norm1_g, norm2_g=norm2_g, w_in=w_in, b_gate=b_gate, a_qn_g=a_qn_g, a_kn_g=a_kn_g,
                  b_qn_g=b_qn_g, b_kn_g=b_kn_g, c_w_alpha=c_w_alpha, c_b_alpha=c_b_alpha, c_on_g=c_on_g,
                  w_oa=w_oa, w_ob=w_ob, w_oc=w_oc, w_out=w_out, peer_wq=peer_wq, peer_subkeys=peer_subkeys,
                  peer_u=peer_u, peer_v=peer_v)
    h2 = x.reshape(bsz * t_len, d)
    for l in range(w_in.shape[0]):
        mod = _mod(c, w_mod[l], b_mod[l]).reshape(bsz, 6, 1, d)
        h2 = _layer(h2, mod, t_len, {k: v[l] for k, v in params.items()})
    return h2.reshape(bsz, t_len, d)
```

```python
import functools

import jax
import jax.numpy as jnp
import numpy as np
from jax import lax
from jax.experimental import pallas as pl
from jax.experimental.pallas import tpu as pltpu

F32 = jnp.float32
BF16 = jnp.bfloat16

D_MODEL = 1024
EPS = 1e-6
CHUNK = 64
TOPK_MAX = 256
A_HEADS, A_DIM = 8, 64
IDX_HEADS, IDX_DIM = 4, 64
B_HEADS, B_DIM = 8, 64
C_HEADS, C_KEY, C_VAL, C_RANK, C_TAU = 4, 64, 128, 16, 16.0
PEER_HEADS, PEER_NKEYS, PEER_QDIM, PEER_TOPK = 8, 128, 256, 16
N_BRANCH = 3

LANE = 128

_SEGMENTS = (
    ("gl", 3072), ("aq", 512), ("bq", 512), ("bk", 512), ("bv", 512), ("cv", 512), ("cr", 512),
    ("iq", 256), ("cq", 256), ("ck", 256), ("akv", 128), ("ikw", 128), ("ca", 128),
)
COL = {}
_off = 0
for _name, _w in _SEGMENTS:
    COL[_name] = (_off, _w)
    _off += _w
Z_COLS = _off
Z_TILE = Z_COLS // 3

_REF_WIDTHS = (512, 64, 64, 256, 64, 4, 512, 512, 512, 256, 256, 512, 512, 16, 3072)


def _reorder_w_in(w_in):
    offs = np.cumsum((0,) + _REF_WIDTHS)
    p = [w_in[:, offs[i]:offs[i + 1]] for i in range(len(_REF_WIDTHS))]
    aq, ak, av, iq, ik, iw, bq, bk, bv, cq, ck, cv, cr, ca, gl = p
    d = w_in.shape[0]
    z = lambda n: jnp.zeros((d, n), w_in.dtype)
    cols = [gl, aq, bq, bk, bv, cv, cr, iq, cq, ck, ak, av, ik, iw, z(60), ca, z(112)]
    return jnp.concatenate(cols, axis=1).astype(BF16)


def _cb(name):
    off, w = COL[name]
    return off // w


def _mod_kernel(c_ref, w_ref, b_ref, o_ref):
    c = c_ref[...]
    ca = (c * jax.nn.sigmoid(c)).astype(BF16)
    o_ref[...] = jnp.dot(ca, w_ref[...].astype(BF16), preferred_element_type=F32) + b_ref[...]


def _mod(c, w_mod, b_mod):
    bsz, d = c.shape
    n = w_mod.shape[1]
    tn = 768
    return pl.pallas_call(
        _mod_kernel,
        grid=(n // tn,),
        in_specs=[pl.BlockSpec((bsz, d), lambda j: (0, 0)),
                  pl.BlockSpec((d, tn), lambda j: (0, j)),
                  pl.BlockSpec((1, tn), lambda j: (0, j))],
        out_specs=pl.BlockSpec((bsz, tn), lambda j: (0, j)),
        out_shape=jax.ShapeDtypeStruct((bsz, n), F32),
        name="adaln_mod",
    )(c, w_mod, b_mod.reshape(1, n))


def _inproj_kernel(h_ref, g_ref, sc_ref, sh_ref, w_ref, z_ref, n_scr):
    @pl.when(pl.program_id(1) == 0)
    def _():
        x = h_ref[...]
        y = x * lax.rsqrt(jnp.mean(x * x, axis=-1, keepdims=True) + EPS)
        n = y * g_ref[...] * (1.0 + sc_ref[0]) + sh_ref[0]
        n_scr[...] = n.astype(BF16)

    z_ref[...] = jnp.dot(n_scr[...], w_ref[...], preferred_element_type=F32).astype(BF16)


def _inproj(h2, norm_g, scale, shift, w_z, t_len, tm=512):
    n_tok, d = h2.shape
    tm = min(tm, t_len)
    bpt = t_len // tm
    return pl.pallas_call(
        _inproj_kernel,
        grid=(n_tok // tm, Z_COLS // Z_TILE),
        in_specs=[pl.BlockSpec((tm, d), lambda i, j: (i, 0)),
                  pl.BlockSpec((1, d), lambda i, j: (0, 0)),
                  pl.BlockSpec((1, 1, d), lambda i, j: (i // bpt, 0, 0)),
                  pl.BlockSpec((1, 1, d), lambda i, j: (i // bpt, 0, 0)),
                  pl.BlockSpec((d, Z_TILE), lambda i, j: (0, j))],
        out_specs=pl.BlockSpec((tm, Z_TILE), lambda i, j: (i, j)),
        out_shape=jax.ShapeDtypeStruct((n_tok, Z_COLS), BF16),
        scratch_shapes=[pltpu.VMEM((tm, d), BF16)],
        compiler_params=pltpu.CompilerParams(
            dimension_semantics=("parallel", "arbitrary"), vmem_limit_bytes=48 * 1024 * 1024),
        name="norm_inproj",
    )(h2, norm_g.reshape(1, d), scale, shift, w_z)


def _head_rms(x, gain):
    return x * lax.rsqrt(jnp.mean(x * x, axis=-1, keepdims=True) + EPS) * gain


DSA_QB = 256
DSA_GROUP = 1
DSA_BISECT = 30


def _dsa_kernel(aq_ref, akv_ref, iq_ref, ikwq_ref, ikwk_ref, qg_ref, kg_ref, o_ref, sm_scr, sel_scr,
                *, q_off, n_sel):
    qb = aq_ref.shape[1]
    tk = akv_ref.shape[1]
    q0 = (q_off + pl.program_id(1)) * qb
    t_pos = q0 + lax.broadcasted_iota(jnp.int32, (qb, 1), 0)
    limit = (t_pos // CHUNK + 1) * CHUNK
    s_pos = lax.broadcasted_iota(jnp.int32, (1, tk), 1)
    adm = s_pos < limit

    kidx = ikwk_ref[0][:, :IDX_DIM]
    qidx = iq_ref[0]
    w = ikwq_ref[0][:, IDX_DIM:IDX_DIM + IDX_HEADS].astype(F32)
    score = jnp.zeros((qb, tk), F32)
    for h in range(IDX_HEADS):
        rel = lax.dot_general(qidx[:, h * IDX_DIM:(h + 1) * IDX_DIM], kidx,
                              (((1,), (1,)), ((), ())), preferred_element_type=F32)
        score = score + w[:, h:h + 1] * jnp.maximum(rel, 0.0)
    neg_inf = jnp.float32(-jnp.inf)
    sm_scr[...] = jnp.where(adm, score, neg_inf)

    hi0 = jnp.max(sm_scr[...], axis=-1, keepdims=True)
    lo0 = jnp.min(jnp.where(adm, score, jnp.float32(jnp.inf)), axis=-1, keepdims=True)
    kf = jnp.float32(n_sel)

    def bisect(_, carry):
        lo, hi = carry
        mid = 0.5 * (lo + hi)
        cnt = jnp.sum(jnp.where(sm_scr[...] > mid, 1.0, 0.0), axis=-1, keepdims=True)
        below = cnt < kf
        return jnp.where(below, lo, mid), jnp.where(below, mid, hi)

    _, hi = lax.fori_loop(0, DSA_BISECT, bisect, (lo0, hi0))
    sm = sm_scr[...]
    v1 = jnp.max(jnp.where(sm <= hi, sm, neg_inf), axis=-1, keepdims=True)
    c_ge1 = jnp.sum(jnp.where(sm >= v1, 1.0, 0.0), axis=-1, keepdims=True)
    v2 = jnp.max(jnp.where(sm < v1, sm, neg_inf), axis=-1, keepdims=True)
    c_ge2 = jnp.sum(jnp.where(sm >= v2, 1.0, 0.0), axis=-1, keepdims=True)
    v3 = jnp.max(jnp.where(sm < v2, sm, neg_inf), axis=-1, keepdims=True)
    thr = jnp.where(c_ge1 >= kf, v1, jnp.where(c_ge2 >= kf, v2, v3))
    need = kf - jnp.sum(jnp.where(sm > thr, 1.0, 0.0), axis=-1, keepdims=True)

    r_i = lax.broadcasted_iota(jnp.int32, (LANE, LANE), 0)
    c_i = lax.broadcasted_iota(jnp.int32, (LANE, LANE), 1)
    incl = jnp.where(r_i <= c_i, 1.0, 0.0).astype(BF16)
    all_sel = limit <= n_sel
    run = jnp.zeros((qb, 1), F32)
    for j in range(tk // LANE):
        sl = slice(j * LANE, (j + 1) * LANE)
        smj = sm_scr[:, sl]
        eqj = jnp.where(smj == thr, 1.0, 0.0)
        inc = jnp.dot(eqj.astype(BF16), incl, preferred_element_type=F32)
        rank = inc - eqj + run
        picked = (smj > thr) | ((smj == thr) & (rank < need))
        admj = (j * LANE + lax.broadcasted_iota(jnp.int32, (1, LANE), 1)) < limit
        selj = (all_sel & admj) | (jnp.logical_not(all_sel) & picked)
        sel_scr[:, sl] = jnp.where(selj, 0.0, neg_inf)
        run = run + inc[:, LANE - 1:LANE]

    kv = akv_ref[0].astype(F32)
    k = _head_rms(kv[:, :A_DIM], kg_ref[...]).astype(BF16)
    v = akv_ref[0][:, A_DIM:2 * A_DIM]
    aq = aq_ref[0].astype(F32)
    scale = A_DIM ** -0.5
    for h in range(A_HEADS):
        qh = (_head_rms(aq[:, h * A_DIM:(h + 1) * A_DIM], qg_ref[...]) * scale).astype(BF16)
        logits = lax.dot_general(qh, k, (((1,), (1,)), ((), ())), preferred_element_type=F32) + sel_scr[...]
        m = jnp.max(logits, axis=-1, keepdims=True)
        p = jnp.exp(logits - m)
        denom = jnp.sum(p, axis=-1, keepdims=True)
        oh = jnp.dot(p.astype(BF16), v, preferred_element_type=F32) / denom
        o_ref[0, :, h * A_DIM:(h + 1) * A_DIM] = oh.astype(o_ref.dtype)


def _dsa(z3, qg, kg):
    bsz, t_len, _ = z3.shape
    n_sel = min(TOPK_MAX, t_len // 4)
    span = DSA_QB * DSA_GROUP
    outs = []
    for g in range(t_len // span):
        tk = span * (g + 1)
        q_off = g * DSA_GROUP
        outs.append(pl.pallas_call(
            functools.partial(_dsa_kernel, q_off=q_off, n_sel=n_sel),
            grid=(bsz, DSA_GROUP),
            in_specs=[pl.BlockSpec((1, DSA_QB, 512), lambda b, i, q_off=q_off: (b, q_off + i, _cb("aq"))),
                      pl.BlockSpec((1, tk, LANE), lambda b, i: (b, 0, _cb("akv"))),
                      pl.BlockSpec((1, DSA_QB, 256), lambda b, i, q_off=q_off: (b, q_off + i, _cb("iq"))),
                      pl.BlockSpec((1, DSA_QB, LANE), lambda b, i, q_off=q_off: (b, q_off + i, _cb("ikw"))),
                      pl.BlockSpec((1, tk, LANE), lambda b, i: (b, 0, _cb("ikw"))),
                      pl.BlockSpec((1, A_DIM), lambda b, i: (0, 0)),
                      pl.BlockSpec((1, A_DIM), lambda b, i: (0, 0))],
            out_specs=pl.BlockSpec((1, DSA_QB, 512), lambda b, i: (b, i, 0)),
            out_shape=jax.ShapeDtypeStruct((bsz, span, 512), BF16),
            scratch_shapes=[pltpu.VMEM((DSA_QB, tk), F32), pltpu.VMEM((DSA_QB, tk), F32)],
            compiler_params=pltpu.CompilerParams(
                dimension_semantics=("parallel", "arbitrary"), vmem_limit_bytes=48 * 1024 * 1024),
            name=f"dsa_attention_g{g}",
        )(z3, z3, z3, z3, z3, qg.reshape(1, A_DIM), kg.reshape(1, A_DIM)))
    return jnp.concatenate(outs, axis=1)


def _split_dot(x, m, x_is_lhs=True):
    hi = x.astype(BF16)
    lo = (x - hi.astype(F32)).astype(BF16)
    if x_is_lhs:
        return jnp.dot(hi, m, preferred_element_type=F32) + jnp.dot(lo, m, preferred_element_type=F32)
    return jnp.dot(m, hi, preferred_element_type=F32) + jnp.dot(m, lo, preferred_element_type=F32)


def _split_dot2(x, m2):
    hi = x.astype(BF16)
    lo = (x - hi.astype(F32)).astype(BF16)
    return jnp.dot(jnp.concatenate([hi, lo], axis=1), m2, preferred_element_type=F32)


SB_QB = 256
SB_KB = 128
SB_SWEEP = 4


def _sb_kernel(q_ref, k_ref, v_ref, qg_ref, kg_ref, o_ref, kn_scr):
    qi = pl.program_id(2)
    qb = q_ref.shape[1]
    hd = B_DIM

    @pl.when(qi == 0)
    def _():
        kf = k_ref[0].astype(F32)
        for h in range(2):
            kn_scr[h] = _head_rms(kf[:, h * hd:(h + 1) * hd], kg_ref[...]).astype(BF16)

    q = q_ref[0].astype(F32)
    scale = hd ** -0.5
    qn = [(_head_rms(q[:, h * hd:(h + 1) * hd], qg_ref[...]) * scale).astype(BF16) for h in range(2)]
    kb = SB_KB
    r_i = lax.broadcasted_iota(jnp.int32, (kb, kb), 0)
    c_i = lax.broadcasted_iota(jnp.int32, (kb, kb), 1)
    m_incl = jnp.where(r_i >= c_i, 1.0, 0.0).astype(BF16)
    m_incl2 = jnp.concatenate([m_incl, m_incl], axis=0)
    t_pos = qi * qb + lax.broadcasted_iota(jnp.int32, (qb, 1), 0)
    lane = lax.broadcasted_iota(jnp.int32, (1, 2 * hd), 1)
    last_sweep = ((qi + 1) * (qb // kb) - 1) // SB_SWEEP

    def body(i, carry):
        sb = last_sweep - i
        carry = list(carry)
        blocks = list(reversed(range(SB_SWEEP)))
        starts = [pl.multiple_of((sb * SB_SWEEP + j) * kb, kb) for j in blocks]
        stricts = [(ks + lax.broadcasted_iota(jnp.int32, (1, kb), 1)) < t_pos for ks in starts]
        zs = [[lax.dot_general(qn[h], kn_scr[h, pl.ds(ks, kb), :], (((1,), (1,)), ((), ())),
                               preferred_element_type=F32) for h in range(2)] for ks in starts]
        drops = [[jnp.where(st, jnp.maximum(z, 0.0) + jnp.log(1.0 + jnp.exp(-jnp.abs(z))), 0.0) for z in zb]
                 for st, zb in zip(stricts, zs)]
        sums = [[_split_dot2(d, m_incl2) for d in db] for db in drops]
        probs = []
        for st, zb, sb_ in zip(stricts, zs, sums):
            row = []
            for h in range(2):
                row.append(jnp.where(st, jnp.exp(zb[h] - sb_[h] - carry[2 * h]), 0.0).astype(BF16))
                carry[2 * h] = carry[2 * h] + sb_[h][:, 0:1]
            probs.append(row)
        for ks, row in zip(starts, probs):
            vblk = v_ref[0, pl.ds(ks, kb), :]
            for h in range(2):
                carry[2 * h + 1] = carry[2 * h + 1] + jnp.dot(row[h], vblk, preferred_element_type=F32)
        return tuple(carry)

    init = (jnp.zeros((qb, 1), F32), jnp.zeros((qb, 2 * hd), F32)) * 2
    res = lax.fori_loop(0, last_sweep + 1, body, init)
    o_ref[0] = jnp.where(lane < hd, res[1], res[3]).astype(o_ref.dtype)


def _stickbreak(z3, qg, kg):
    bsz, t_len, _ = z3.shape
    nq = t_len // SB_QB
    npair = B_HEADS // 2
    cq, ck, cv = COL["bq"][0] // LANE, COL["bk"][0] // LANE, COL["bv"][0] // LANE
    return pl.pallas_call(
        _sb_kernel,
        grid=(bsz, npair, nq),
        in_specs=[pl.BlockSpec((1, SB_QB, LANE), lambda b, p, i: (b, i, cq + p)),
                  pl.BlockSpec((1, t_len, LANE), lambda b, p, i: (b, 0, ck + p)),
                  pl.BlockSpec((1, t_len, LANE), lambda b, p, i: (b, 0, cv + p)),
                  pl.BlockSpec((1, B_DIM), lambda b, p, i: (0, 0)),
                  pl.BlockSpec((1, B_DIM), lambda b, p, i: (0, 0))],
        out_specs=pl.BlockSpec((1, SB_QB, LANE), lambda b, p, i: (b, i, p)),
        out_shape=jax.ShapeDtypeStruct((bsz, t_len, B_HEADS * B_DIM), BF16),
        scratch_shapes=[pltpu.VMEM((2, t_len, B_DIM), BF16)],
        compiler_params=pltpu.CompilerParams(
            dimension_semantics=("parallel", "parallel", "arbitrary"), vmem_limit_bytes=48 * 1024 * 1024),
        name="stickbreak_attention",
    )(z3, z3, z3, qg.reshape(1, B_DIM), kg.reshape(1, B_DIM))


GLA_SUB = 16
GLA_UNROLL = 4


def _gla_kernel(q_ref, k_ref, v_ref, r_ref, a_ref, wa_ref, ba_ref, og_ref, o_ref, st_scr):
    t_len = q_ref.shape[1]
    nc = t_len // CHUNK
    dk, dv, sub = C_KEY, C_VAL, GLA_SUB
    r_i = lax.broadcasted_iota(jnp.int32, (CHUNK, CHUNK), 0)
    c_i = lax.broadcasted_iota(jnp.int32, (CHUNK, CHUNK), 1)
    l_incl = jnp.where(r_i >= c_i, 1.0, 0.0).astype(BF16)
    st_scr[...] = jnp.zeros_like(st_scr)
    neg_inf = jnp.float32(-jnp.inf)
    pr = lax.broadcasted_iota(jnp.int32, (sub * sub, 1), 0)
    pair_ok = pr % sub <= pr // sub
    fr = lax.broadcasted_iota(jnp.int32, (sub, sub * sub), 0)
    fc = lax.broadcasted_iota(jnp.int32, (sub, sub * sub), 1)
    fold_t = jnp.where(fc // sub == fr, 1.0, 0.0).astype(BF16)
    hr = lax.broadcasted_iota(jnp.int32, (2 * dk, 2 * dv), 0)
    hc = lax.broadcasted_iota(jnp.int32, (2 * dk, 2 * dv), 1)
    head_ones = jnp.where(hr // dk == hc // dv, 1.0, 0.0).astype(BF16)

    def per_t(x):
        return jnp.broadcast_to(x[:, None, :], (sub, sub, x.shape[1])).reshape(sub * sub, x.shape[1])

    def per_s(x):
        return jnp.broadcast_to(x[None, :, :], (sub, sub, x.shape[1])).reshape(sub * sub, x.shape[1])

    def trip(i, _):
        rows = [pl.ds(pl.multiple_of((i * GLA_UNROLL + u) * CHUNK, CHUNK), CHUNK) for u in range(GLA_UNROLL)]
        pres = [jnp.dot(a_ref[0, r, :], wa_ref[...], preferred_element_type=F32) + ba_ref[...] for r in rows]
        la2s = [(jnp.minimum(p, 0.0) - jnp.log(1.0 + jnp.exp(-jnp.abs(p)))) * (1.0 / C_TAU) for p in pres]
        b2s = [_split_dot(la2, l_incl, x_is_lhs=False) for la2 in la2s]
        q2s = [q_ref[0, r, :].astype(F32) * (dk ** -0.5) for r in rows]
        k2s = [k_ref[0, r, :].astype(F32) for r in rows]
        v2fs = [v_ref[0, r, :].astype(F32) for r in rows]

        terms, vss = [], []
        for b2, q2, k2, v2f in zip(b2s, q2s, k2s, v2fs):
            for j in range(CHUNK // sub):
                sl = slice(j * sub, (j + 1) * sub)
                bt, qt = (per_t(x[sl]) for x in (b2, q2))
                bs, ks_, vs = (per_s(x[sl]) for x in (b2, k2, v2f))
                terms.append(qt * ks_ * jnp.exp(jnp.where(pair_ok, bt - bs, neg_inf)))
                vss.append(vs)
        w2s = [_split_dot(t, head_ones) for t in terms]
        diags = [jnp.dot(fold_t, (w2 * vs).astype(BF16), preferred_element_type=F32)
                 for w2, vs in zip(w2s, vss)]

        per_chunk = []
        for u in range(GLA_UNROLL):
            heads = []
            for h in range(2):
                b = b2s[u][:, h * dk:(h + 1) * dk]
                q = q2s[u][:, h * dk:(h + 1) * dk]
                k = k2s[u][:, h * dk:(h + 1) * dk]
                v = v_ref[0, rows[u], h * dv:(h + 1) * dv]
                atts = []
                for j in range(1, CHUNK // sub):
                    lo_r = j * sub
                    ref_row = b[lo_r - 1:lo_r, :]
                    qs = (q[lo_r:lo_r + sub] * jnp.exp(b[lo_r:lo_r + sub] - ref_row)).astype(BF16)
                    ks = (k[:lo_r] * jnp.exp(ref_row - b[:lo_r])).astype(BF16)
                    atts.append(lax.dot_general(qs, ks, (((1,), (1,)), ((), ())), preferred_element_type=F32))
                b_last = b[CHUNK - 1:CHUNK, :]
                heads.append(dict(
                    v=v, atts=atts, qe=(q * jnp.exp(b)).astype(BF16), decay=jnp.exp(b_last),
                    kd=(k * jnp.exp(b_last - b)).astype(BF16)))
            per_chunk.append(heads)
        for u in range(GLA_UNROLL):
            for h in range(2):
                hd = per_chunk[u][h]
                intra = [diags[u * (CHUNK // sub)][:, h * dv:(h + 1) * dv]]
                for j in range(1, CHUNK // sub):
                    lo_r = j * sub
                    intra.append(diags[u * (CHUNK // sub) + j][:, h * dv:(h + 1) * dv]
                                 + jnp.dot(hd["atts"][j - 1].astype(BF16), hd["v"][:lo_r], preferred_element_type=F32))
                hd["intra"] = jnp.concatenate(intra, axis=0)
                hd["upd"] = lax.dot_general(hd["v"], hd["kd"], (((0,), (0,)), ((), ())),
                                            preferred_element_type=F32)

        for h in range(2):
            st = st_scr[h]
            for u in range(GLA_UNROLL):
                hd = per_chunk[u][h]
                o = lax.dot_general(hd["qe"], st.astype(BF16), (((1,), (1,)), ((), ())),
                                    preferred_element_type=F32) + hd["intra"]
                st = st * hd["decay"] + hd["upd"]
                y = o * lax.rsqrt(jnp.mean(o * o, axis=-1, keepdims=True) + EPS) * og_ref[:, h * dv:(h + 1) * dv]
                r = r_ref[0, rows[u], h * dv:(h + 1) * dv].astype(F32)
                o_ref[0, rows[u], h * dv:(h + 1) * dv] = (y * (r * jax.nn.sigmoid(r))).astype(o_ref.dtype)
            st_scr[h] = st
        return 0

    lax.fori_loop(0, nc // GLA_UNROLL, trip, 0)


def _gla(z3, w_alpha, b_alpha, out_gain):
    bsz, t_len, _ = z3.shape
    npair = C_HEADS // 2
    wa = jnp.zeros((LANE, C_HEADS * C_KEY), F32).at[:C_RANK].set(w_alpha).astype(BF16)
    cq, ck = COL["cq"][0] // LANE, COL["ck"][0] // LANE
    cv, cr, ca = COL["cv"][0] // 256, COL["cr"][0] // 256, COL["ca"][0] // LANE
    return pl.pallas_call(
        _gla_kernel,
        grid=(bsz, npair),
        in_specs=[pl.BlockSpec((1, t_len, LANE), lambda b, p: (b, 0, cq + p)),
                  pl.BlockSpec((1, t_len, LANE), lambda b, p: (b, 0, ck + p)),
                  pl.BlockSpec((1, t_len, 256), lambda b, p: (b, 0, cv + p)),
                  pl.BlockSpec((1, t_len, 256), lambda b, p: (b, 0, cr + p)),
                  pl.BlockSpec((1, t_len, LANE), lambda b, p: (b, 0, ca)),
                  pl.BlockSpec((LANE, LANE), lambda b, p: (0, p)),
                  pl.BlockSpec((1, LANE), lambda b, p: (0, p)),
                  pl.BlockSpec((1, 256), lambda b, p: (0, p))],
        out_specs=pl.BlockSpec((1, t_len, 256), lambda b, p: (b, 0, p)),
        out_shape=jax.ShapeDtypeStruct((bsz, t_len, C_HEADS * C_VAL), BF16),
        scratch_shapes=[pltpu.VMEM((2, C_VAL, C_KEY), F32)],
        compiler_params=pltpu.CompilerParams(
            dimension_semantics=("parallel", "parallel"), vmem_limit_bytes=48 * 1024 * 1024),
        name="gla_attention",
    )(z3, z3, z3, z3, z3, wa, b_alpha.reshape(1, -1), out_gain.reshape(1, -1))


def _merge_kernel(h_ref, g0_ref, g1_ref, g2_ref, bg_ref, ya_ref, yb_ref, yc_ref, woa_ref, wob_ref, woc_ref,
                  wout_ref, gate_ref, ng_ref, sc_ref, sh_ref, wpq_ref, h1_ref, n2_ref, q_ref):
    d = h_ref.shape[1]
    merged = jnp.zeros(h_ref.shape, F32)
    for i, (g_ref, y_ref, w_ref) in enumerate(((g0_ref, ya_ref, woa_ref), (g1_ref, yb_ref, wob_ref),
                                               (g2_ref, yc_ref, woc_ref))):
        gate = jax.nn.sigmoid(g_ref[...].astype(F32) + bg_ref[:, i * d:(i + 1) * d])
        merged = merged + gate * jnp.dot(y_ref[...], w_ref[...], preferred_element_type=F32)
    y = jnp.dot(merged.astype(BF16), wout_ref[...], preferred_element_type=F32)
    h1 = h_ref[...] + gate_ref[0] * y
    h1_ref[...] = h1
    n = h1 * lax.rsqrt(jnp.mean(h1 * h1, axis=-1, keepdims=True) + EPS)
    n2 = (n * ng_ref[...] * (1.0 + sc_ref[0]) + sh_ref[0]).astype(BF16)
    n2_ref[...] = n2
    q_ref[...] = jnp.dot(n2, wpq_ref[...], preferred_element_type=F32).astype(BF16)


def _merge(h2, z, ya, yb, yc, b_gate, w_oa, w_ob, w_oc, w_out, gate1, norm_g, scale, shift, w_pq, t_len, tm=256):
    n_tok, d = h2.shape
    bpt = t_len // tm
    nq = w_pq.shape[1]
    row = lambda i: (i, 0)
    fixed = lambda i: (0, 0)
    per_b = lambda i: (i // bpt, 0, 0)
    bw = ya.shape[1]
    return pl.pallas_call(
        _merge_kernel,
        grid=(n_tok // tm,),
        in_specs=[pl.BlockSpec((tm, d), row),
                  pl.BlockSpec((tm, d), lambda i: (i, 0)),
                  pl.BlockSpec((tm, d), lambda i: (i, 1)),
                  pl.BlockSpec((tm, d), lambda i: (i, 2)),
                  pl.BlockSpec((1, N_BRANCH * d), fixed),
                  pl.BlockSpec((tm, bw), row), pl.BlockSpec((tm, bw), row), pl.BlockSpec((tm, bw), row),
                  pl.BlockSpec((bw, d), fixed), pl.BlockSpec((bw, d), fixed), pl.BlockSpec((bw, d), fixed),
                  pl.BlockSpec((d, d), fixed),
                  pl.BlockSpec((1, 1, d), per_b),
                  pl.BlockSpec((1, d), fixed),
                  pl.BlockSpec((1, 1, d), per_b), pl.BlockSpec((1, 1, d), per_b),
                  pl.BlockSpec((d, nq), fixed)],
        out_specs=[pl.BlockSpec((tm, d), row), pl.BlockSpec((tm, d), row), pl.BlockSpec((tm, nq), row)],
        out_shape=[jax.ShapeDtypeStruct((n_tok, d), F32), jax.ShapeDtypeStruct((n_tok, d), BF16),
                   jax.ShapeDtypeStruct((n_tok, nq), BF16)],
        compiler_params=pltpu.CompilerParams(
            dimension_semantics=("parallel",), vmem_limit_bytes=48 * 1024 * 1024),
        name="merge_outproj_peerq",
    )(h2, z, z, z, b_gate.reshape(1, -1), ya, yb, yc, w_oa.astype(BF16), w_ob.astype(BF16), w_oc.astype(BF16),
      w_out.astype(BF16), gate1, norm_g.reshape(1, d), scale, shift, w_pq.astype(BF16))


PEER_TB = 128


def _take_top(vals, ids, k):
    neg_inf = jnp.float32(-jnp.inf)
    big = jnp.float32(2.0 ** 30)
    out_v, out_i = [], []
    for _ in range(k):
        m = jnp.max(vals, axis=0, keepdims=True)
        first = jnp.min(jnp.where(vals == m, ids, big), axis=0, keepdims=True)
        out_v.append(m)
        out_i.append(first)
        vals = jnp.where(ids == first, neg_inf, vals)
    return out_v, out_i


def _peer_topk_kernel(q_ref, sk_ref, eid_ref, g_ref):
    tb = q_ref.shape[0]
    hq = PEER_QDIM // 2
    key_id = lax.broadcasted_iota(jnp.int32, (PEER_NKEYS, tb), 0).astype(F32)
    k = PEER_TOPK
    n_exp = PEER_NKEYS * PEER_NKEYS
    n_j = [k if i == 0 else 8 for i in range(k)]
    pos = jnp.concatenate([(i * k + lax.broadcasted_iota(jnp.int32, (n_j[i], tb), 0)) * n_exp
                           for i in range(k)], axis=0).astype(F32)
    for h in range(PEER_HEADS):
        tops, topi = [], []
        for p in range(2):
            qhp = q_ref[:, (2 * h + p) * hq:(2 * h + p + 1) * hq]
            s = lax.dot_general(sk_ref[h, p], qhp, (((1,), (1,)), ((), ())), preferred_element_type=F32)
            v, ids = _take_top(s, key_id, k)
            tops.append(v)
            topi.append(ids)
        b_s = jnp.concatenate(tops[1], axis=0)
        b_i = jnp.concatenate(topi[1], axis=0)
        cand = jnp.concatenate([tops[0][i] + b_s[:n_j[i]] for i in range(k)], axis=0)
        cid = jnp.concatenate([topi[0][i] * PEER_NKEYS + b_i[:n_j[i]] for i in range(k)], axis=0)
        best, tag = _take_top(cand, pos + cid, k)
        best = jnp.concatenate(best, axis=0)
        e = jnp.exp(best - best[0:1])
        g_ref[0, h * k:(h + 1) * k, :] = e / jnp.sum(e, axis=0, keepdims=True)
        eid_ref[0, h * k:(h + 1) * k, :] = jnp.concatenate(tag, axis=0).astype(jnp.int32) % n_exp


def _peer_topk(q, sub_keys):
    n_tok = q.shape[0]
    nblk = n_tok // PEER_TB
    slots = PEER_HEADS * PEER_TOPK
    return pl.pallas_call(
        _peer_topk_kernel,
        grid=(nblk,),
        in_specs=[pl.BlockSpec((PEER_TB, q.shape[1]), lambda i: (i, 0)),
                  pl.BlockSpec(sub_keys.shape, lambda i: (0, 0, 0, 0))],
        out_specs=[pl.BlockSpec((1, slots, PEER_TB), lambda i: (i, 0, 0)),
                   pl.BlockSpec((1, slots, PEER_TB), lambda i: (i, 0, 0))],
        out_shape=[jax.ShapeDtypeStruct((nblk, slots, PEER_TB), jnp.int32),
                   jax.ShapeDtypeStruct((nblk, slots, PEER_TB), F32)],
        compiler_params=pltpu.CompilerParams(dimension_semantics=("parallel",)),
        name="peer_topk",
    )(q, sub_keys.astype(BF16))


PEER_VMEM_LIMIT = 50 * 1024 * 1024
SUBLANE = 8
PACK = 2 * SUBLANE
PEER_GRP = LANE // PACK
PEER_TOK_UNROLL = 8


def _pair_tiles(tab):
    e, d = tab.shape
    bits = lax.bitcast_convert_type(tab.astype(BF16), jnp.uint16).astype(jnp.uint32)
    bits = bits.reshape(e // 2, 2, d // LANE, LANE)
    return (bits[:, 0] | (bits[:, 1] << 16)).reshape(e // 2 * SUBLANE, LANE)


def _tile(tab_ref, word_row):
    words = tab_ref[pl.ds(pl.multiple_of(word_row, SUBLANE), SUBLANE), :]
    return pltpu.bitcast(words, BF16)


def _dup_tiles(x2):
    n, d = x2.shape
    x4 = x2.reshape(n, d // LANE, 1, LANE)
    return jnp.broadcast_to(x4, (n, d // LANE, 2, LANE)).reshape(n, PACK, LANE)


def _expand_mats(slots):
    col = np.arange(slots * PACK)
    s_of = (col // LANE) * PEER_GRP + (col % LANE) // PACK
    h_of = col % 2
    m = np.zeros((2, slots, slots * PACK), np.float32)
    for h in range(2):
        m[h, s_of[h_of == h], col[h_of == h]] = 1.0
    return jnp.asarray(m, BF16)


def _table_spec(tab):
    return pl.BlockSpec(tab.shape, lambda i: (0, 0), pipeline_mode=pl.Buffered(1))


PEER_STAGE = 2


def _index_stage(slots):
    return [pltpu.SMEM((PEER_STAGE, PEER_TOK_UNROLL, slots), jnp.int32), pltpu.SemaphoreType.DMA((PEER_STAGE,))]


def _for_each_token(rows_ref, idx_smem, sems, tb, token_fn):
    u = PEER_TOK_UNROLL
    ntrip = tb // u

    def stage(trip, buf):
        src = rows_ref.at[0, pl.ds(pl.multiple_of(trip * u, u), u), :]
        return pltpu.make_async_copy(src, idx_smem.at[buf], sems.at[buf])

    for b in range(PEER_STAGE):
        stage(b, b).start()

    def outer(j, carry):
        for b in range(PEER_STAGE):
            trip = j * PEER_STAGE + b
            stage(trip, b).wait()
            for tt in range(u):
                token_fn(trip * u + tt, lambda s, b=b, tt=tt: idx_smem[b, tt, s])

            @pl.when(trip + PEER_STAGE < ntrip)
            def _():
                stage(trip + PEER_STAGE, b).start()
        return carry

    lax.fori_loop(0, ntrip // PEER_STAGE, outer, 0)


def _peer_u_kernel(rows_ref, xx_ref, eid_ref, g_ref, ex_ref, tab_ref, cw_ref, ae_scr, ao_scr, idx_smem, sems):
    tb = xx_ref.shape[0]
    slots = ae_scr.shape[0]
    lane = lax.broadcasted_iota(jnp.int32, (SUBLANE, tb), 1)
    r_i = lax.broadcasted_iota(jnp.int32, (PACK, LANE), 0)
    k_i = lax.broadcasted_iota(jnp.int32, (PACK, LANE), 1)
    sel = jnp.where((k_i // PACK == r_i % SUBLANE) & (k_i % 2 == r_i // SUBLANE), 1.0, 0.0).astype(BF16)

    def token(t, idx):
        xt = xx_ref[t]
        xg = jnp.concatenate([xt] * PEER_GRP, axis=0)
        here = lane == t
        for g in range(0, slots // PEER_GRP, 2):
            halves = []
            for gg in (g, g + 1):
                tiles = [_tile(tab_ref, idx(gg * PEER_GRP + p)) for p in range(PEER_GRP)]
                halves.append(jnp.concatenate(tiles, axis=0) * xg)
            part = jnp.dot(sel, jnp.concatenate(halves, axis=1), preferred_element_type=F32)
            for j, gg in enumerate((g, g + 1)):
                tot = jnp.sum(part[:, j * LANE:(j + 1) * LANE], axis=1, keepdims=True)
                rs = slice(gg * PEER_GRP, (gg + 1) * PEER_GRP)
                ae_scr[rs, :] = jnp.where(here, tot[:SUBLANE], ae_scr[rs, :])
                ao_scr[rs, :] = jnp.where(here, tot[SUBLANE:], ao_scr[rs, :])

    _for_each_token(rows_ref, idx_smem, sems, tb, token)
    odd = (eid_ref[0] & 1) == 1
    act = jnp.where(odd, ao_scr[...], ae_scr[...])
    coef = g_ref[0] * (0.5 * act * (1.0 + lax.erf(act * (2.0 ** -0.5))))
    zero = jnp.zeros_like(coef)
    tn = (((0,), (0,)), ((), ()))
    even_c = jnp.where(odd, zero, coef).astype(BF16)
    odd_c = jnp.where(odd, coef, zero).astype(BF16)
    for g in range(slots // PEER_GRP):
        cols = slice(g * LANE, (g + 1) * LANE)
        cw_ref[:, g, :] = (lax.dot_general(even_c, ex_ref[0][:, cols], tn, preferred_element_type=F32)
                           + lax.dot_general(odd_c, ex_ref[1][:, cols], tn, preferred_element_type=F32))


def _peer_u(rows, eid, g, xx, tab_u, expand):
    nblk, slots, tb = eid.shape
    n_tok = xx.shape[0]
    return pl.pallas_call(
        _peer_u_kernel,
        grid=(nblk,),
        in_specs=[pl.BlockSpec((1, tb, slots), lambda i: (i, 0, 0)),
                  pl.BlockSpec((tb,) + xx.shape[1:], lambda i: (i, 0, 0)),
                  pl.BlockSpec((1, slots, tb), lambda i: (i, 0, 0)),
                  pl.BlockSpec((1, slots, tb), lambda i: (i, 0, 0)),
                  pl.BlockSpec(expand.shape, lambda i: (0, 0, 0)),
                  _table_spec(tab_u)],
        out_specs=pl.BlockSpec((tb, slots // PEER_GRP, LANE), lambda i: (i, 0, 0)),
        out_shape=jax.ShapeDtypeStruct((n_tok, slots // PEER_GRP, LANE), F32),
        scratch_shapes=[pltpu.VMEM((slots, tb), F32), pltpu.VMEM((slots, tb), F32)] + _index_stage(slots),
        compiler_params=pltpu.CompilerParams(
            dimension_semantics=("arbitrary",), vmem_limit_bytes=PEER_VMEM_LIMIT),
        name="peer_u",
    )(rows, xx, eid, g, expand, tab_u)


def _peer_v_kernel(rows_ref, cw_ref, tab_ref, o_ref, idx_smem, sems, *, slots):
    tb = o_ref.shape[0]
    r_i = lax.broadcasted_iota(jnp.int32, (PACK, LANE), 0)
    k_i = lax.broadcasted_iota(jnp.int32, (PACK, LANE), 1)
    keep = (k_i % PACK) // 2 == r_i % SUBLANE
    first = r_i < SUBLANE

    def token(t, idx):
        acc = jnp.zeros((PACK, 2 * LANE), F32)
        cwt = cw_ref[t]
        for g in range(0, slots // PEER_GRP, 2):
            halves = [jnp.concatenate([_tile(tab_ref, idx(gg * PEER_GRP + p)) for p in range(PEER_GRP)],
                                      axis=0) for gg in (g, g + 1)]
            coef = jnp.where(first, jnp.broadcast_to(cwt[g:g + 1, :], (PACK, LANE)),
                             jnp.broadcast_to(cwt[g + 1:g + 2, :], (PACK, LANE)))
            lhs = jnp.where(keep, coef, 0.0).astype(BF16)
            acc = acc + jnp.dot(lhs, jnp.concatenate(halves, axis=1), preferred_element_type=F32)
        o_ref[t] = acc[:SUBLANE, :LANE] + acc[SUBLANE:, LANE:]

    _for_each_token(rows_ref, idx_smem, sems, tb, token)


def _peer_v(rows, cw, tab_v, tb):
    n_tok, groups, _ = cw.shape
    slots = groups * PEER_GRP
    return pl.pallas_call(
        functools.partial(_peer_v_kernel, slots=slots),
        grid=(n_tok // tb,),
        in_specs=[pl.BlockSpec((1, tb, slots), lambda i: (i, 0, 0)),
                  pl.BlockSpec((tb, groups, LANE), lambda i: (i, 0, 0)),
                  _table_spec(tab_v)],
        out_specs=pl.BlockSpec((tb, SUBLANE, LANE), lambda i: (i, 0, 0)),
        out_shape=jax.ShapeDtypeStruct((n_tok, SUBLANE, LANE), F32),
        scratch_shapes=_index_stage(slots),
        compiler_params=pltpu.CompilerParams(
            dimension_semantics=("arbitrary",), vmem_limit_bytes=PEER_VMEM_LIMIT),
        name="peer_v",
    )(rows, cw, tab_v)


def _residual_kernel(h_ref, p_ref, gate_ref, o_ref):
    o_ref[...] = h_ref[...] + gate_ref[0] * p_ref[...]


def _residual(h2, p2, gate, t_len, tm=512):
    n_tok, d = h2.shape
    bpt = t_len // tm
    return pl.pallas_call(
        _residual_kernel,
        grid=(n_tok // tm,),
        in_specs=[pl.BlockSpec((tm, d), lambda i: (i, 0)), pl.BlockSpec((tm, d), lambda i: (i, 0)),
                  pl.BlockSpec((1, 1, d), lambda i: (i // bpt, 0, 0))],
        out_specs=pl.BlockSpec((tm, d), lambda i: (i, 0)),
        out_shape=jax.ShapeDtypeStruct((n_tok, d), F32),
        compiler_params=pltpu.CompilerParams(dimension_semantics=("parallel",)),
        name="peer_residual",
    )(h2, p2, gate)


def _layer(h2, mod, t_len, p):
    n_tok, d = h2.shape
    bsz = n_tok // t_len
    z = _inproj(h2, p["norm1_g"], mod[:, 1], mod[:, 0], _reorder_w_in(p["w_in"]), t_len)
    z3 = z.reshape(bsz, t_len, Z_COLS)
    ya = _dsa(z3, p["a_qn_g"], p["a_kn_g"]).reshape(n_tok, -1)
    yb = _stickbreak(z3, p["b_qn_g"], p["b_kn_g"]).reshape(n_tok, -1)
    yc = _gla(z3, p["c_w_alpha"], p["c_b_alpha"], p["c_on_g"]).reshape(n_tok, -1)
    h1, n2, q = _merge(h2, z, ya, yb, yc, p["b_gate"], p["w_oa"], p["w_ob"], p["w_oc"], p["w_out"],
                       mod[:, 2], p["norm2_g"], mod[:, 4], mod[:, 3], p["peer_wq"], t_len)
    eid, g = _peer_topk(q, p["peer_subkeys"])
    nblk, slots, tb = eid.shape
    rows = jnp.swapaxes((eid >> 1) * SUBLANE, 1, 2)
    cw = _peer_u(rows, eid, g, _dup_tiles(n2), _pair_tiles(p["peer_u"]), _expand_mats(slots))
    out = _peer_v(rows, cw, _pair_tiles(p["peer_v"]), tb)
    return _residual(h1, out.reshape(n_tok, d), mod[:, 5], t_len)


def kernel(x, c, norm1_g, norm2_g, w_mod, b_mod, w_in, b_gate, a_qn_g, a_kn_g, b_qn_g, b_kn_g, c_w_alpha, c_b_alpha, c_on_g, w_oa, w_ob, w_oc, w_out, peer_wq, peer_subkeys, peer_u, peer_v):
    bsz, t_len, d = x.shape
    params = dict(norm1_g=norm1_g, norm2_g=norm2_g, w_in=w_in, b_gate=b_gate, a_qn_g=a_qn_g, a_kn_g=a_kn_g,
                  b_qn_g=b_qn_g, b_kn_g=b_kn_g, c_w_alpha=c_w_alpha, c_b_alpha=c_b_alpha, c_on_g=c_on_g,
                  w_oa=w_oa, w_ob=w_ob, w_oc=w_oc, w_out=w_out, peer_wq=peer_wq, peer_subkeys=peer_subkeys,
                  peer_u=peer_u, peer_v=peer_v)
    h2 = x.reshape(bsz * t_len, d)
    for l in range(w_in.shape[0]):
        mod = _mod(c, w_mod[l], b_mod[l]).reshape(bsz, 6, 1, d)
        h2 = _layer(h2, mod, t_len, {k: v[l] for k, v in params.items()})
    return h2.reshape(bsz, t_len, d)
```

```python
import functools

import jax
import jax.numpy as jnp
import numpy as np
from jax import lax
from jax.experimental import pallas as pl
from jax.experimental.pallas import tpu as pltpu

F32 = jnp.float32
BF16 = jnp.bfloat16

D_MODEL = 1024
EPS = 1e-6
CHUNK = 64
TOPK_MAX = 256
A_HEADS, A_DIM = 8, 64
IDX_HEADS, IDX_DIM = 4, 64
B_HEADS, B_DIM = 8, 64
C_HEADS, C_KEY, C_VAL, C_RANK, C_TAU = 4, 64, 128, 16, 16.0
PEER_HEADS, PEER_NKEYS, PEER_QDIM, PEER_TOPK = 8, 128, 256, 16
N_BRANCH = 3

LANE = 128

_SEGMENTS = (
    ("gl", 3072), ("aq", 512), ("bq", 512), ("bk", 512), ("bv", 512), ("cv", 512), ("cr", 512),
    ("iq", 256), ("cq", 256), ("ck", 256), ("akv", 128), ("ikw", 128), ("ca", 128),
)
COL = {}
_off = 0
for _name, _w in _SEGMENTS:
    COL[_name] = (_off, _w)
    _off += _w
Z_COLS = _off
Z_TILE = Z_COLS // 3

_REF_WIDTHS = (512, 64, 64, 256, 64, 4, 512, 512, 512, 256, 256, 512, 512, 16, 3072)


def _reorder_w_in(w_in):
    offs = np.cumsum((0,) + _REF_WIDTHS)
    p = [w_in[:, offs[i]:offs[i + 1]] for i in range(len(_REF_WIDTHS))]
    aq, ak, av, iq, ik, iw, bq, bk, bv, cq, ck, cv, cr, ca, gl = p
    d = w_in.shape[0]
    z = lambda n: jnp.zeros((d, n), w_in.dtype)
    cols = [gl, aq, bq, bk, bv, cv, cr, iq, cq, ck, ak, av, ik, iw, z(60), ca, z(112)]
    return jnp.concatenate(cols, axis=1).astype(BF16)


def _cb(name):
    off, w = COL[name]
    return off // w


def _mod_kernel(c_ref, w_ref, b_ref, o_ref):
    c = c_ref[...]
    ca = (c * jax.nn.sigmoid(c)).astype(BF16)
    o_ref[...] = jnp.dot(ca, w_ref[...].astype(BF16), preferred_element_type=F32) + b_ref[...]


def _mod(c, w_mod, b_mod):
    bsz, d = c.shape
    n = w_mod.shape[1]
    tn = 768
    return pl.pallas_call(
        _mod_kernel,
        grid=(n // tn,),
        in_specs=[pl.BlockSpec((bsz, d), lambda j: (0, 0)),
                  pl.BlockSpec((d, tn), lambda j: (0, j)),
                  pl.BlockSpec((1, tn), lambda j: (0, j))],
        out_specs=pl.BlockSpec((bsz, tn), lambda j: (0, j)),
        out_shape=jax.ShapeDtypeStruct((bsz, n), F32),
        name="adaln_mod",
    )(c, w_mod, b_mod.reshape(1, n))


def _inproj_kernel(h_ref, g_ref, sc_ref, sh_ref, w_ref, z_ref, n_scr):
    @pl.when(pl.program_id(1) == 0)
    def _():
        x = h_ref[...]
        y = x * lax.rsqrt(jnp.mean(x * x, axis=-1, keepdims=True) + EPS)
        n = y * g_ref[...] * (1.0 + sc_ref[0]) + sh_ref[0]
        n_scr[...] = n.astype(BF16)

    z_ref[...] = jnp.dot(n_scr[...], w_ref[...], preferred_element_type=F32).astype(BF16)


def _inproj(h2, norm_g, scale, shift, w_z, t_len, tm=512):
    n_tok, d = h2.shape
    tm = min(tm, t_len)
    bpt = t_len // tm
    return pl.pallas_call(
        _inproj_kernel,
        grid=(n_tok // tm, Z_COLS // Z_TILE),
        in_specs=[pl.BlockSpec((tm, d), lambda i, j: (i, 0)),
                  pl.BlockSpec((1, d), lambda i, j: (0, 0)),
                  pl.BlockSpec((1, 1, d), lambda i, j: (i // bpt, 0, 0)),
                  pl.BlockSpec((1, 1, d), lambda i, j: (i // bpt, 0, 0)),
                  pl.BlockSpec((d, Z_TILE), lambda i, j: (0, j))],
        out_specs=pl.BlockSpec((tm, Z_TILE), lambda i, j: (i, j)),
        out_shape=jax.ShapeDtypeStruct((n_tok, Z_COLS), BF16),
        scratch_shapes=[pltpu.VMEM((tm, d), BF16)],
        compiler_params=pltpu.CompilerParams(
            dimension_semantics=("parallel", "arbitrary"), vmem_limit_bytes=48 * 1024 * 1024),
        name="norm_inproj",
    )(h2, norm_g.reshape(1, d), scale, shift, w_z)


def _head_rms(x, gain):
    return x * lax.rsqrt(jnp.mean(x * x, axis=-1, keepdims=True) + EPS) * gain


DSA_QB = 256
DSA_GROUP = 1
DSA_BISECT = 30


def _dsa_kernel(aq_ref, akv_ref, iq_ref, ikwq_ref, ikwk_ref, qg_ref, kg_ref, o_ref, sm_scr, sel_scr,
                *, q_off, n_sel):
    qb = aq_ref.shape[1]
    tk = akv_ref.shape[1]
    q0 = (q_off + pl.program_id(1)) * qb
    t_pos = q0 + lax.broadcasted_iota(jnp.int32, (qb, 1), 0)
    limit = (t_pos // CHUNK + 1) * CHUNK
    s_pos = lax.broadcasted_iota(jnp.int32, (1, tk), 1)
    adm = s_pos < limit

    kidx = ikwk_ref[0][:, :IDX_DIM]
    qidx = iq_ref[0]
    w = ikwq_ref[0][:, IDX_DIM:IDX_DIM + IDX_HEADS].astype(F32)
    score = jnp.zeros((qb, tk), F32)
    for h in range(IDX_HEADS):
        rel = lax.dot_general(qidx[:, h * IDX_DIM:(h + 1) * IDX_DIM], kidx,
                              (((1,), (1,)), ((), ())), preferred_element_type=F32)
        score = score + w[:, h:h + 1] * jnp.maximum(rel, 0.0)
    neg_inf = jnp.float32(-jnp.inf)
    sm_scr[...] = jnp.where(adm, score, neg_inf)

    hi0 = jnp.max(sm_scr[...], axis=-1, keepdims=True)
    lo0 = jnp.min(jnp.where(adm, score, jnp.float32(jnp.inf)), axis=-1, keepdims=True)
    kf = jnp.float32(n_sel)

    def bisect(_, carry):
        lo, hi = carry
        mid = 0.5 * (lo + hi)
        cnt = jnp.sum(jnp.where(sm_scr[...] > mid, 1.0, 0.0), axis=-1, keepdims=True)
        below = cnt < kf
        return jnp.where(below, lo, mid), jnp.where(below, mid, hi)

    _, hi = lax.fori_loop(0, DSA_BISECT, bisect, (lo0, hi0))
    sm = sm_scr[...]
    v1 = jnp.max(jnp.where(sm <= hi, sm, neg_inf), axis=-1, keepdims=True)
    c_ge1 = jnp.sum(jnp.where(sm >= v1, 1.0, 0.0), axis=-1, keepdims=True)
    v2 = jnp.max(jnp.where(sm < v1, sm, neg_inf), axis=-1, keepdims=True)
    c_ge2 = jnp.sum(jnp.where(sm >= v2, 1.0, 0.0), axis=-1, keepdims=True)
    v3 = jnp.max(jnp.where(sm < v2, sm, neg_inf), axis=-1, keepdims=True)
    thr = jnp.where(c_ge1 >= kf, v1, jnp.where(c_ge2 >= kf, v2, v3))
    need = kf - jnp.sum(jnp.where(sm > thr, 1.0, 0.0), axis=-1, keepdims=True)

    r_i = lax.broadcasted_iota(jnp.int32, (LANE, LANE), 0)
    c_i = lax.broadcasted_iota(jnp.int32, (LANE, LANE), 1)
    incl = jnp.where(r_i <= c_i, 1.0, 0.0).astype(BF16)
    all_sel = limit <= n_sel
    run = jnp.zeros((qb, 1), F32)
    for j in range(tk // LANE):
        sl = slice(j * LANE, (j + 1) * LANE)
        smj = sm_scr[:, sl]
        eqj = jnp.where(smj == thr, 1.0, 0.0)
        inc = jnp.dot(eqj.astype(BF16), incl, preferred_element_type=F32)
        rank = inc - eqj + run
        picked = (smj > thr) | ((smj == thr) & (rank < need))
        admj = (j * LANE + lax.broadcasted_iota(jnp.int32, (1, LANE), 1)) < limit
        selj = (all_sel & admj) | (jnp.logical_not(all_sel) & picked)
        sel_scr[:, sl] = jnp.where(selj, 0.0, neg_inf)
        run = run + inc[:, LANE - 1:LANE]

    kv = akv_ref[0].astype(F32)
    k = _head_rms(kv[:, :A_DIM], kg_ref[...]).astype(BF16)
    v = akv_ref[0][:, A_DIM:2 * A_DIM]
    aq = aq_ref[0].astype(F32)
    scale = A_DIM ** -0.5
    for h in range(A_HEADS):
        qh = (_head_rms(aq[:, h * A_DIM:(h + 1) * A_DIM], qg_ref[...]) * scale).astype(BF16)
        logits = lax.dot_general(qh, k, (((1,), (1,)), ((), ())), preferred_element_type=F32) + sel_scr[...]
        m = jnp.max(logits, axis=-1, keepdims=True)
        p = jnp.exp(logits - m)
        denom = jnp.sum(p, axis=-1, keepdims=True)
        oh = jnp.dot(p.astype(BF16), v, preferred_element_type=F32) / denom
        o_ref[0, :, h * A_DIM:(h + 1) * A_DIM] = oh.astype(o_ref.dtype)


def _dsa(z3, qg, kg):
    bsz, t_len, _ = z3.shape
    n_sel = min(TOPK_MAX, t_len // 4)
    span = DSA_QB * DSA_GROUP
    outs = []
    for g in range(t_len // span):
        tk = span * (g + 1)
        q_off = g * DSA_GROUP
        outs.append(pl.pallas_call(
            functools.partial(_dsa_kernel, q_off=q_off, n_sel=n_sel),
            grid=(bsz, DSA_GROUP),
            in_specs=[pl.BlockSpec((1, DSA_QB, 512), lambda b, i, q_off=q_off: (b, q_off + i, _cb("aq"))),
                      pl.BlockSpec((1, tk, LANE), lambda b, i: (b, 0, _cb("akv"))),
                      pl.BlockSpec((1, DSA_QB, 256), lambda b, i, q_off=q_off: (b, q_off + i, _cb("iq"))),
                      pl.BlockSpec((1, DSA_QB, LANE), lambda b, i, q_off=q_off: (b, q_off + i, _cb("ikw"))),
                      pl.BlockSpec((1, tk, LANE), lambda b, i: (b, 0, _cb("ikw"))),
                      pl.BlockSpec((1, A_DIM), lambda b, i: (0, 0)),
                      pl.BlockSpec((1, A_DIM), lambda b, i: (0, 0))],
            out_specs=pl.BlockSpec((1, DSA_QB, 512), lambda b, i: (b, i, 0)),
            out_shape=jax.ShapeDtypeStruct((bsz, span, 512), BF16),
            scratch_shapes=[pltpu.VMEM((DSA_QB, tk), F32), pltpu.VMEM((DSA_QB, tk), F32)],
            compiler_params=pltpu.CompilerParams(
                dimension_semantics=("parallel", "arbitrary"), vmem_limit_bytes=48 * 1024 * 1024),
            name=f"dsa_attention_g{g}",
        )(z3, z3, z3, z3, z3, qg.reshape(1, A_DIM), kg.reshape(1, A_DIM)))
    return jnp.concatenate(outs, axis=1)


def _split_dot(x, m, x_is_lhs=True):
    hi = x.astype(BF16)
    lo = (x - hi.astype(F32)).astype(BF16)
    if x_is_lhs:
        return jnp.dot(hi, m, preferred_element_type=F32) + jnp.dot(lo, m, preferred_element_type=F32)
    return jnp.dot(m, hi, preferred_element_type=F32) + jnp.dot(m, lo, preferred_element_type=F32)


def _split_dot2(x, m2):
    hi = x.astype(BF16)
    lo = (x - hi.astype(F32)).astype(BF16)
    return jnp.dot(jnp.concatenate([hi, lo], axis=1), m2, preferred_element_type=F32)


SB_QB = 256
SB_KB = 128
SB_SWEEP = 4


def _sb_kernel(q_ref, k_ref, v_ref, qg_ref, kg_ref, o_ref, kn_scr):
    qi = pl.program_id(2)
    qb = q_ref.shape[1]
    hd = B_DIM

    @pl.when(qi == 0)
    def _():
        kf = k_ref[0].astype(F32)
        for h in range(2):
            kn_scr[h] = _head_rms(kf[:, h * hd:(h + 1) * hd], kg_ref[...]).astype(BF16)

    q = q_ref[0].astype(F32)
    scale = hd ** -0.5
    qn = [(_head_rms(q[:, h * hd:(h + 1) * hd], qg_ref[...]) * scale).astype(BF16) for h in range(2)]
    kb = SB_KB
    r_i = lax.broadcasted_iota(jnp.int32, (kb, kb), 0)
    c_i = lax.broadcasted_iota(jnp.int32, (kb, kb), 1)
    m_incl = jnp.where(r_i >= c_i, 1.0, 0.0).astype(BF16)
    m_incl2 = jnp.concatenate([m_incl, m_incl], axis=0)
    t_pos = qi * qb + lax.broadcasted_iota(jnp.int32, (qb, 1), 0)
    lane = lax.broadcasted_iota(jnp.int32, (1, 2 * hd), 1)
    last_sweep = ((qi + 1) * (qb // kb) - 1) // SB_SWEEP

    def body(i, carry):
        sb = last_sweep - i
        carry = list(carry)
        blocks = list(reversed(range(SB_SWEEP)))
        starts = [pl.multiple_of((sb * SB_SWEEP + j) * kb, kb) for j in blocks]
        stricts = [(ks + lax.broadcasted_iota(jnp.int32, (1, kb), 1)) < t_pos for ks in starts]
        zs = [[lax.dot_general(qn[h], kn_scr[h, pl.ds(ks, kb), :], (((1,), (1,)), ((), ())),
                               preferred_element_type=F32) for h in range(2)] for ks in starts]
        drops = [[jnp.where(st, jnp.maximum(z, 0.0) + jnp.log(1.0 + jnp.exp(-jnp.abs(z))), 0.0) for z in zb]
                 for st, zb in zip(stricts, zs)]
        sums = [[_split_dot2(d, m_incl2) for d in db] for db in drops]
        probs = []
        for st, zb, sb_ in zip(stricts, zs, sums):
            row = []
            for h in range(2):
                row.append(jnp.where(st, jnp.exp(zb[h] - sb_[h] - carry[2 * h]), 0.0).astype(BF16))
                carry[2 * h] = carry[2 * h] + sb_[h][:, 0:1]
            probs.append(row)
        for ks, row in zip(starts, probs):
            vblk = v_ref[0, pl.ds(ks, kb), :]
            for h in range(2):
                carry[2 * h + 1] = carry[2 * h + 1] + jnp.dot(row[h], vblk, preferred_element_type=F32)
        return tuple(carry)

    init = (jnp.zeros((qb, 1), F32), jnp.zeros((qb, 2 * hd), F32)) * 2
    res = lax.fori_loop(0, last_sweep + 1, body, init)
    o_ref[0] = jnp.where(lane < hd, res[1], res[3]).astype(o_ref.dtype)


def _stickbreak(z3, qg, kg):
    bsz, t_len, _ = z3.shape
    nq = t_len // SB_QB
    npair = B_HEADS // 2
    cq, ck, cv = COL["bq"][0] // LANE, COL["bk"][0] // LANE, COL["bv"][0] // LANE
    return pl.pallas_call(
        _sb_kernel,
        grid=(bsz, npair, nq),
        in_specs=[pl.BlockSpec((1, SB_QB, LANE), lambda b, p, i: (b, i, cq + p)),
                  pl.BlockSpec((1, t_len, LANE), lambda b, p, i: (b, 0, ck + p)),
                  pl.BlockSpec((1, t_len, LANE), lambda b, p, i: (b, 0, cv + p)),
                  pl.BlockSpec((1, B_DIM), lambda b, p, i: (0, 0)),
                  pl.BlockSpec((1, B_DIM), lambda b, p, i: (0, 0))],
        out_specs=pl.BlockSpec((1, SB_QB, LANE), lambda b, p, i: (b, i, p)),
        out_shape=jax.ShapeDtypeStruct((bsz, t_len, B_HEADS * B_DIM), BF16),
        scratch_shapes=[pltpu.VMEM((2, t_len, B_DIM), BF16)],
        compiler_params=pltpu.CompilerParams(
            dimension_semantics=("parallel", "parallel", "arbitrary"), vmem_limit_bytes=48 * 1024 * 1024),
        name="stickbreak_attention",
    )(z3, z3, z3, qg.reshape(1, B_DIM), kg.reshape(1, B_DIM))


GLA_SUB = 16
GLA_UNROLL = 4


def _gla_kernel(q_ref, k_ref, v_ref, r_ref, a_ref, wa_ref, ba_ref, og_ref, o_ref, st_scr):
    t_len = q_ref.shape[1]
    nc = t_len // CHUNK
    dk, dv, sub = C_KEY, C_VAL, GLA_SUB
    r_i = lax.broadcasted_iota(jnp.int32, (CHUNK, CHUNK), 0)
    c_i = lax.broadcasted_iota(jnp.int32, (CHUNK, CHUNK), 1)
    l_incl = jnp.where(r_i >= c_i, 1.0, 0.0).astype(BF16)
    st_scr[...] = jnp.zeros_like(st_scr)
    neg_inf = jnp.float32(-jnp.inf)
    pr = lax.broadcasted_iota(jnp.int32, (sub * sub, 1), 0)
    pair_ok = pr % sub <= pr // sub
    fr = lax.broadcasted_iota(jnp.int32, (sub, sub * sub), 0)
    fc = lax.broadcasted_iota(jnp.int32, (sub, sub * sub), 1)
    fold_t = jnp.where(fc // sub == fr, 1.0, 0.0).astype(BF16)
    hr = lax.broadcasted_iota(jnp.int32, (2 * dk, 2 * dv), 0)
    hc = lax.broadcasted_iota(jnp.int32, (2 * dk, 2 * dv), 1)
    head_ones = jnp.where(hr // dk == hc // dv, 1.0, 0.0).astype(BF16)

    def per_t(x):
        return jnp.broadcast_to(x[:, None, :], (sub, sub, x.shape[1])).reshape(sub * sub, x.shape[1])

    def per_s(x):
        return jnp.broadcast_to(x[None, :, :], (sub, sub, x.shape[1])).reshape(sub * sub, x.shape[1])

    def trip(i, _):
        rows = [pl.ds(pl.multiple_of((i * GLA_UNROLL + u) * CHUNK, CHUNK), CHUNK) for u in range(GLA_UNROLL)]
        pres = [jnp.dot(a_ref[0, r, :], wa_ref[...], preferred_element_type=F32) + ba_ref[...] for r in rows]
        la2s = [(jnp.minimum(p, 0.0) - jnp.log(1.0 + jnp.exp(-jnp.abs(p)))) * (1.0 / C_TAU) for p in pres]
        b2s = [_split_dot(la2, l_incl, x_is_lhs=False) for la2 in la2s]
        q2s = [q_ref[0, r, :].astype(F32) * (dk ** -0.5) for r in rows]
        k2s = [k_ref[0, r, :].astype(F32) for r in rows]
        v2fs = [v_ref[0, r, :].astype(F32) for r in rows]

        terms, vss = [], []
        for b2, q2, k2, v2f in zip(b2s, q2s, k2s, v2fs):
            for j in range(CHUNK // sub):
                sl = slice(j * sub, (j + 1) * sub)
                bt, qt = (per_t(x[sl]) for x in (b2, q2))
                bs, ks_, vs = (per_s(x[sl]) for x in (b2, k2, v2f))
                terms.append(qt * ks_ * jnp.exp(jnp.where(pair_ok, bt - bs, neg_inf)))
                vss.append(vs)
        w2s = [_split_dot(t, head_ones) for t in terms]
        diags = [jnp.dot(fold_t, (w2 * vs).astype(BF16), preferred_element_type=F32)
                 for w2, vs in zip(w2s, vss)]

        per_chunk = []
        for u in range(GLA_UNROLL):
            heads = []
            for h in range(2):
                b = b2s[u][:, h * dk:(h + 1) * dk]
                q = q2s[u][:, h * dk:(h + 1) * dk]
                k = k2s[u][:, h * dk:(h + 1) * dk]
                v = v_ref[0, rows[u], h * dv:(h + 1) * dv]
                atts = []
                for j in range(1, CHUNK // sub):
                    lo_r = j * sub
                    ref_row = b[lo_r - 1:lo_r, :]
                    qs = (q[lo_r:lo_r + sub] * jnp.exp(b[lo_r:lo_r + sub] - ref_row)).astype(BF16)
                    ks = (k[:lo_r] * jnp.exp(ref_row - b[:lo_r])).astype(BF16)
                    atts.append(lax.dot_general(qs, ks, (((1,), (1,)), ((), ())), preferred_element_type=F32))
                b_last = b[CHUNK - 1:CHUNK, :]
                heads.append(dict(
                    v=v, atts=atts, qe=(q * jnp.exp(b)).astype(BF16), decay=jnp.exp(b_last),
                    kd=(k * jnp.exp(b_last - b)).astype(BF16)))
            per_chunk.append(heads)
        for u in range(GLA_UNROLL):
            for h in range(2):
                hd = per_chunk[u][h]
                intra = [diags[u * (CHUNK // sub)][:, h * dv:(h + 1) * dv]]
                for j in range(1, CHUNK // sub):
                    lo_r = j * sub
                    intra.append(diags[u * (CHUNK // sub) + j][:, h * dv:(h + 1) * dv]
                                 + jnp.dot(hd["atts"][j - 1].astype(BF16), hd["v"][:lo_r], preferred_element_type=F32))
                hd["intra"] = jnp.concatenate(intra, axis=0)
                hd["upd"] = lax.dot_general(hd["v"], hd["kd"], (((0,), (0,)), ((), ())),
                                            preferred_element_type=F32)

        for h in range(2):
            st = st_scr[h]
            for u in range(GLA_UNROLL):
                hd = per_chunk[u][h]
                o = lax.dot_general(hd["qe"], st.astype(BF16), (((1,), (1,)), ((), ())),
                                    preferred_element_type=F32) + hd["intra"]
                st = st * hd["decay"] + hd["upd"]
                y = o * lax.rsqrt(jnp.mean(o * o, axis=-1, keepdims=True) + EPS) * og_ref[:, h * dv:(h + 1) * dv]
                r = r_ref[0, rows[u], h * dv:(h + 1) * dv].astype(F32)
                o_ref[0, rows[u], h * dv:(h + 1) * dv] = (y * (r * jax.nn.sigmoid(r))).astype(o_ref.dtype)
            st_scr[h] = st
        return 0

    lax.fori_loop(0, nc // GLA_UNROLL, trip, 0)


def _gla(z3, w_alpha, b_alpha, out_gain):
    bsz, t_len, _ = z3.shape
    npair = C_HEADS // 2
    wa = jnp.zeros((LANE, C_HEADS * C_KEY), F32).at[:C_RANK].set(w_alpha).astype(BF16)
    cq, ck = COL["cq"][0] // LANE, COL["ck"][0] // LANE
    cv, cr, ca = COL["cv"][0] // 256, COL["cr"][0] // 256, COL["ca"][0] // LANE
    return pl.pallas_call(
        _gla_kernel,
        grid=(bsz, npair),
        in_specs=[pl.BlockSpec((1, t_len, LANE), lambda b, p: (b, 0, cq + p)),
                  pl.BlockSpec((1, t_len, LANE), lambda b, p: (b, 0, ck + p)),
                  pl.BlockSpec((1, t_len, 256), lambda b, p: (b, 0, cv + p)),
                  pl.BlockSpec((1, t_len, 256), lambda b, p: (b, 0, cr + p)),
                  pl.BlockSpec((1, t_len, LANE), lambda b, p: (b, 0, ca)),
                  pl.BlockSpec((LANE, LANE), lambda b, p: (0, p)),
                  pl.BlockSpec((1, LANE), lambda b, p: (0, p)),
                  pl.BlockSpec((1, 256), lambda b, p: (0, p))],
        out_specs=pl.BlockSpec((1, t_len, 256), lambda b, p: (b, 0, p)),
        out_shape=jax.ShapeDtypeStruct((bsz, t_len, C_HEADS * C_VAL), BF16),
        scratch_shapes=[pltpu.VMEM((2, C_VAL, C_KEY), F32)],
        compiler_params=pltpu.CompilerParams(
            dimension_semantics=("parallel", "parallel"), vmem_limit_bytes=48 * 1024 * 1024),
        name="gla_attention",
    )(z3, z3, z3, z3, z3, wa, b_alpha.reshape(1, -1), out_gain.reshape(1, -1))


def _merge_kernel(h_ref, g0_ref, g1_ref, g2_ref, bg_ref, ya_ref, yb_ref, yc_ref, woa_ref, wob_ref, woc_ref,
                  wout_ref, gate_ref, ng_ref, sc_ref, sh_ref, wpq_ref, h1_ref, n2_ref, q_ref):
    d = h_ref.shape[1]
    merged = jnp.zeros(h_ref.shape, F32)
    for i, (g_ref, y_ref, w_ref) in enumerate(((g0_ref, ya_ref, woa_ref), (g1_ref, yb_ref, wob_ref),
                                               (g2_ref, yc_ref, woc_ref))):
        gate = jax.nn.sigmoid(g_ref[...].astype(F32) + bg_ref[:, i * d:(i + 1) * d])
        merged = merged + gate * jnp.dot(y_ref[...], w_ref[...], preferred_element_type=F32)
    y = jnp.dot(merged.astype(BF16), wout_ref[...], preferred_element_type=F32)
    h1 = h_ref[...] + gate_ref[0] * y
    h1_ref[...] = h1
    n = h1 * lax.rsqrt(jnp.mean(h1 * h1, axis=-1, keepdims=True) + EPS)
    n2 = (n * ng_ref[...] * (1.0 + sc_ref[0]) + sh_ref[0]).astype(BF16)
    n2_ref[...] = n2
    q_ref[...] = jnp.dot(n2, wpq_ref[...], preferred_element_type=F32).astype(BF16)


def _merge(h2, z, ya, yb, yc, b_gate, w_oa, w_ob, w_oc, w_out, gate1, norm_g, scale, shift, w_pq, t_len, tm=256):
    n_tok, d = h2.shape
    bpt = t_len // tm
    nq = w_pq.shape[1]
    row = lambda i: (i, 0)
    fixed = lambda i: (0, 0)
    per_b = lambda i: (i // bpt, 0, 0)
    bw = ya.shape[1]
    return pl.pallas_call(
        _merge_kernel,
        grid=(n_tok // tm,),
        in_specs=[pl.BlockSpec((tm, d), row),
                  pl.BlockSpec((tm, d), lambda i: (i, 0)),
                  pl.BlockSpec((tm, d), lambda i: (i, 1)),
                  pl.BlockSpec((tm, d), lambda i: (i, 2)),
                  pl.BlockSpec((1, N_BRANCH * d), fixed),
                  pl.BlockSpec((tm, bw), row), pl.BlockSpec((tm, bw), row), pl.BlockSpec((tm, bw), row),
                  pl.BlockSpec((bw, d), fixed), pl.BlockSpec((bw, d), fixed), pl.BlockSpec((bw, d), fixed),
                  pl.BlockSpec((d, d), fixed),
                  pl.BlockSpec((1, 1, d), per_b),
                  pl.BlockSpec((1, d), fixed),
                  pl.BlockSpec((1, 1, d), per_b), pl.BlockSpec((1, 1, d), per_b),
                  pl.BlockSpec((d, nq), fixed)],
        out_specs=[pl.BlockSpec((tm, d), row), pl.BlockSpec((tm, d), row), pl.BlockSpec((tm, nq), row)],
        out_shape=[jax.ShapeDtypeStruct((n_tok, d), F32), jax.ShapeDtypeStruct((n_tok, d), BF16),
                   jax.ShapeDtypeStruct((n_tok, nq), BF16)],
        compiler_params=pltpu.CompilerParams(
            dimension_semantics=("parallel",), vmem_limit_bytes=48 * 1024 * 1024),
        name="merge_outproj_peerq",
    )(h2, z, z, z, b_gate.reshape(1, -1), ya, yb, yc, w_oa.astype(BF16), w_ob.astype(BF16), w_oc.astype(BF16),
      w_out.astype(BF16), gate1, norm_g.reshape(1, d), scale, shift, w_pq.astype(BF16))


PEER_TB = 128


def _take_top_steps(vals, ids, k, out_v, out_i):
    neg_inf = jnp.float32(-jnp.inf)
    big = jnp.float32(2.0 ** 30)
    for _ in range(k):
        m = jnp.max(vals, axis=0, keepdims=True)
        first = jnp.min(jnp.where(vals == m, ids, big), axis=0, keepdims=True)
        out_v.append(m)
        out_i.append(first)
        vals = jnp.where(ids == first, neg_inf, vals)
        yield


PEER_ROUNDS = 3 * PEER_TOPK


def _retrieve_head_steps(q_ref, sk_ref, h, eid_dst, g_dst):
    tb = q_ref.shape[0]
    hq = PEER_QDIM // 2
    k = PEER_TOPK
    n_exp = PEER_NKEYS * PEER_NKEYS
    key_id = lax.broadcasted_iota(jnp.int32, (PEER_NKEYS, tb), 0).astype(F32)
    n_j = [k if i == 0 else 8 for i in range(k)]
    pos = jnp.concatenate([(i * k + lax.broadcasted_iota(jnp.int32, (n_j[i], tb), 0)) * n_exp
                           for i in range(k)], axis=0).astype(F32)
    tops, topi = [], []
    for p in range(2):
        col = pl.multiple_of((2 * h + p) * hq, hq)
        qhp = q_ref[:, pl.ds(col, hq)]
        s = lax.dot_general(sk_ref[h, p], qhp, (((1,), (1,)), ((), ())), preferred_element_type=F32)
        v, ids = [], []
        yield from _take_top_steps(s, key_id, k, v, ids)
        tops.append(v)
        topi.append(ids)
    b_s = jnp.concatenate(tops[1], axis=0)
    b_i = jnp.concatenate(topi[1], axis=0)
    cand = jnp.concatenate([tops[0][i] + b_s[:n_j[i]] for i in range(k)], axis=0)
    cid = jnp.concatenate([topi[0][i] * PEER_NKEYS + b_i[:n_j[i]] for i in range(k)], axis=0)
    best, tag = [], []
    yield from _take_top_steps(cand, pos + cid, k, best, tag)
    best = jnp.concatenate(best, axis=0)
    e = jnp.exp(best - best[0:1])
    rows = pl.ds(pl.multiple_of(h * k, k), k)
    g_dst[rows, :] = e / jnp.sum(e, axis=0, keepdims=True)
    eid_dst[rows, :] = jnp.concatenate(tag, axis=0).astype(jnp.int32) % n_exp


PEER_VMEM_LIMIT = 50 * 1024 * 1024
SUBLANE = 8
PACK = 2 * SUBLANE
PEER_GRP = LANE // PACK
PEER_TOK_UNROLL = 8


def _pair_tiles(tab):
    e, d = tab.shape
    bits = lax.bitcast_convert_type(tab.astype(BF16), jnp.uint16).astype(jnp.uint32)
    bits = bits.reshape(e // 2, 2, d // LANE, LANE)
    return (bits[:, 0] | (bits[:, 1] << 16)).reshape(e // 2 * SUBLANE, LANE)


def _tile(tab_ref, word_row):
    words = tab_ref[pl.ds(pl.multiple_of(word_row, SUBLANE), SUBLANE), :]
    return pltpu.bitcast(words, BF16)


def _dup_tiles(x2):
    n, d = x2.shape
    x4 = x2.reshape(n, d // LANE, 1, LANE)
    return jnp.broadcast_to(x4, (n, d // LANE, 2, LANE)).reshape(n, PACK, LANE)


def _expand_mats(slots):
    col = np.arange(slots * PACK)
    s_of = (col // LANE) * PEER_GRP + (col % LANE) // PACK
    h_of = col % 2
    m = np.zeros((2, slots, slots * PACK), np.float32)
    for h in range(2):
        m[h, s_of[h_of == h], col[h_of == h]] = 1.0
    return jnp.asarray(m, BF16)


def _table_spec(tab):
    return pl.BlockSpec(tab.shape, lambda i: (0, 0), pipeline_mode=pl.Buffered(1))


PEER_STAGE = 2


def _index_stage(slots):
    return [pltpu.SMEM((PEER_STAGE, PEER_TOK_UNROLL, slots), jnp.int32), pltpu.SemaphoreType.DMA((PEER_STAGE,))]


def _for_each_token(rows_ref, idx_smem, sems, tb, token_fn, side_steps=None, steps_per_token=0):
    u = PEER_TOK_UNROLL
    ntrip = tb // u

    def stage(trip, buf):
        src = rows_ref.at[pl.ds(pl.multiple_of(trip * u, u), u), :]
        return pltpu.make_async_copy(src, idx_smem.at[buf], sems.at[buf])

    for b in range(PEER_STAGE):
        stage(b, b).start()

    def outer(j, carry):
        side = side_steps(j) if side_steps is not None else iter(())
        for b in range(PEER_STAGE):
            trip = j * PEER_STAGE + b
            stage(trip, b).wait()
            for tt in range(u):
                token_fn(trip * u + tt, lambda s, b=b, tt=tt: idx_smem[b, tt, s])
                for _ in range(steps_per_token):
                    next(side, None)

            @pl.when(trip + PEER_STAGE < ntrip)
            def _():
                stage(trip + PEER_STAGE, b).start()
        for _ in side:
            pass
        return carry

    lax.fori_loop(0, ntrip // PEER_STAGE, outer, 0)


def _peer_u_kernel(q_ref, sk_ref, xx_ref, ex_ref, tab_ref, cw_ref, rows_ref, eid_scr, g_scr, rows_scr,
                   ae_scr, ao_scr, idx_smem, sems):
    step = pl.program_id(0)
    cur = step % 2
    prev = 1 - cur
    tb = xx_ref.shape[0]
    slots = ae_scr.shape[0]

    @pl.when(step == 0)
    def _():
        rows_scr[1] = jnp.zeros(rows_scr.shape[1:], rows_scr.dtype)
        eid_scr[1] = jnp.zeros(eid_scr.shape[1:], eid_scr.dtype)
        g_scr[1] = jnp.zeros(g_scr.shape[1:], g_scr.dtype)

    lane = lax.broadcasted_iota(jnp.int32, (SUBLANE, tb), 1)
    r_i = lax.broadcasted_iota(jnp.int32, (PACK, LANE), 0)
    k_i = lax.broadcasted_iota(jnp.int32, (PACK, LANE), 1)
    sel = jnp.where((k_i // PACK == r_i % SUBLANE) & (k_i % 2 == r_i // SUBLANE), 1.0, 0.0).astype(BF16)

    def token(t, idx):
        xt = xx_ref[t]
        xg = jnp.concatenate([xt] * PEER_GRP, axis=0)
        here = lane == t
        for g in range(0, slots // PEER_GRP, 2):
            halves = []
            for gg in (g, g + 1):
                tiles = [_tile(tab_ref, idx(gg * PEER_GRP + p)) for p in range(PEER_GRP)]
                halves.append(jnp.concatenate(tiles, axis=0) * xg)
            part = jnp.dot(sel, jnp.concatenate(halves, axis=1), preferred_element_type=F32)
            for j, gg in enumerate((g, g + 1)):
                tot = jnp.sum(part[:, j * LANE:(j + 1) * LANE], axis=1, keepdims=True)
                rs = slice(gg * PEER_GRP, (gg + 1) * PEER_GRP)
                ae_scr[rs, :] = jnp.where(here, tot[:SUBLANE], ae_scr[rs, :])
                ao_scr[rs, :] = jnp.where(here, tot[SUBLANE:], ao_scr[rs, :])

    tokens_per_trip = PEER_TOK_UNROLL * PEER_STAGE
    assert tb // tokens_per_trip == PEER_HEADS and PEER_ROUNDS % tokens_per_trip == 0
    _for_each_token(rows_scr.at[prev], idx_smem, sems, tb, token,
                    side_steps=lambda h: _retrieve_head_steps(q_ref, sk_ref, h, eid_scr.at[cur], g_scr.at[cur]),
                    steps_per_token=PEER_ROUNDS // tokens_per_trip)

    eid_p = eid_scr[prev]
    odd = (eid_p & 1) == 1
    act = jnp.where(odd, ao_scr[...], ae_scr[...])
    coef = g_scr[prev] * (0.5 * act * (1.0 + lax.erf(act * (2.0 ** -0.5))))
    zero = jnp.zeros_like(coef)
    tn = (((0,), (0,)), ((), ()))
    even_c = jnp.where(odd, zero, coef).astype(BF16)
    odd_c = jnp.where(odd, coef, zero).astype(BF16)
    for g in range(slots // PEER_GRP):
        cols = slice(g * LANE, (g + 1) * LANE)
        cw_ref[:, g, :] = (lax.dot_general(even_c, ex_ref[0][:, cols], tn, preferred_element_type=F32)
                           + lax.dot_general(odd_c, ex_ref[1][:, cols], tn, preferred_element_type=F32))

    rows_cur = jnp.transpose((eid_scr[cur] >> 1) * SUBLANE)
    rows_scr[cur] = rows_cur
    rows_ref[0] = rows_cur


def _peer_u(q, sub_keys, xx, tab_u, expand, tb):
    n_tok = xx.shape[0]
    nblk = n_tok // tb
    slots = PEER_HEADS * PEER_TOPK
    last = nblk - 1
    return pl.pallas_call(
        _peer_u_kernel,
        grid=(nblk + 1,),
        in_specs=[pl.BlockSpec((tb, q.shape[1]), lambda i: (jnp.minimum(i, last), 0)),
                  pl.BlockSpec(sub_keys.shape, lambda i: (0, 0, 0, 0)),
                  pl.BlockSpec((tb,) + xx.shape[1:], lambda i: (jnp.maximum(i - 1, 0), 0, 0)),
                  pl.BlockSpec(expand.shape, lambda i: (0, 0, 0)),
                  _table_spec(tab_u)],
        out_specs=[pl.BlockSpec((tb, slots // PEER_GRP, LANE), lambda i: (jnp.maximum(i - 1, 0), 0, 0)),
                   pl.BlockSpec((1, tb, slots), lambda i: (jnp.minimum(i, last), 0, 0))],
        out_shape=[jax.ShapeDtypeStruct((n_tok, slots // PEER_GRP, LANE), F32),
                   jax.ShapeDtypeStruct((nblk, tb, slots), jnp.int32)],
        scratch_shapes=[pltpu.VMEM((2, slots, tb), jnp.int32), pltpu.VMEM((2, slots, tb), F32),
                        pltpu.VMEM((2, tb, slots), jnp.int32),
                        pltpu.VMEM((slots, tb), F32), pltpu.VMEM((slots, tb), F32)] + _index_stage(slots),
        compiler_params=pltpu.CompilerParams(
            dimension_semantics=("arbitrary",), vmem_limit_bytes=PEER_VMEM_LIMIT),
        name="peer_retrieve_u",
    )(q, sub_keys.astype(BF16), xx, expand, tab_u)


def _peer_v_kernel(rows_ref, cw_ref, tab_ref, o_ref, idx_smem, sems, *, slots):
    tb = o_ref.shape[0]
    r_i = lax.broadcasted_iota(jnp.int32, (PACK, LANE), 0)
    k_i = lax.broadcasted_iota(jnp.int32, (PACK, LANE), 1)
    keep = (k_i % PACK) // 2 == r_i % SUBLANE
    first = r_i < SUBLANE

    def token(t, idx):
        acc = jnp.zeros((PACK, 2 * LANE), F32)
        cwt = cw_ref[t]
        for g in range(0, slots // PEER_GRP, 2):
            halves = [jnp.concatenate([_tile(tab_ref, idx(gg * PEER_GRP + p)) for p in range(PEER_GRP)],
                                      axis=0) for gg in (g, g + 1)]
            coef = jnp.where(first, jnp.broadcast_to(cwt[g:g + 1, :], (PACK, LANE)),
                             jnp.broadcast_to(cwt[g + 1:g + 2, :], (PACK, LANE)))
            lhs = jnp.where(keep, coef, 0.0).astype(BF16)
            acc = acc + jnp.dot(lhs, jnp.concatenate(halves, axis=1), preferred_element_type=F32)
        o_ref[t] = acc[:SUBLANE, :LANE] + acc[SUBLANE:, LANE:]

    _for_each_token(rows_ref.at[0], idx_smem, sems, tb, token)


def _peer_v(rows, cw, tab_v, tb):
    n_tok, groups, _ = cw.shape
    slots = groups * PEER_GRP
    return pl.pallas_call(
        functools.partial(_peer_v_kernel, slots=slots),
        grid=(n_tok // tb,),
        in_specs=[pl.BlockSpec((1, tb, slots), lambda i: (i, 0, 0)),
                  pl.BlockSpec((tb, groups, LANE), lambda i: (i, 0, 0)),
                  _table_spec(tab_v)],
        out_specs=pl.BlockSpec((tb, SUBLANE, LANE), lambda i: (i, 0, 0)),
        out_shape=jax.ShapeDtypeStruct((n_tok, SUBLANE, LANE), F32),
        scratch_shapes=_index_stage(slots),
        compiler_params=pltpu.CompilerParams(
            dimension_semantics=("arbitrary",), vmem_limit_bytes=PEER_VMEM_LIMIT),
        name="peer_v",
    )(rows, cw, tab_v)


def _residual_kernel(h_ref, p_ref, gate_ref, o_ref):
    o_ref[...] = h_ref[...] + gate_ref[0] * p_ref[...]


def _residual(h2, p2, gate, t_len, tm=512):
    n_tok, d = h2.shape
    bpt = t_len // tm
    return pl.pallas_call(
        _residual_kernel,
        grid=(n_tok // tm,),
        in_specs=[pl.BlockSpec((tm, d), lambda i: (i, 0)), pl.BlockSpec((tm, d), lambda i: (i, 0)),
                  pl.BlockSpec((1, 1, d), lambda i: (i // bpt, 0, 0))],
        out_specs=pl.BlockSpec((tm, d), lambda i: (i, 0)),
        out_shape=jax.ShapeDtypeStruct((n_tok, d), F32),
        compiler_params=pltpu.CompilerParams(dimension_semantics=("parallel",)),
        name="peer_residual",
    )(h2, p2, gate)


def _layer(h2, mod, t_len, p):
    n_tok, d = h2.shape
    bsz = n_tok // t_len
    z = _inproj(h2, p["norm1_g"], mod[:, 1], mod[:, 0], _reorder_w_in(p["w_in"]), t_len)
    z3 = z.reshape(bsz, t_len, Z_COLS)
    ya = _dsa(z3, p["a_qn_g"], p["a_kn_g"]).reshape(n_tok, -1)
    yb = _stickbreak(z3, p["b_qn_g"], p["b_kn_g"]).reshape(n_tok, -1)
    yc = _gla(z3, p["c_w_alpha"], p["c_b_alpha"], p["c_on_g"]).reshape(n_tok, -1)
    h1, n2, q = _merge(h2, z, ya, yb, yc, p["b_gate"], p["w_oa"], p["w_ob"], p["w_oc"], p["w_out"],
                       mod[:, 2], p["norm2_g"], mod[:, 4], mod[:, 3], p["peer_wq"], t_len)
    cw, rows = _peer_u(q, p["peer_subkeys"], _dup_tiles(n2), _pair_tiles(p["peer_u"]),
                       _expand_mats(PEER_HEADS * PEER_TOPK), PEER_TB)
    out = _peer_v(rows, cw, _pair_tiles(p["peer_v"]), PEER_TB)
    return _residual(h1, out.reshape(n_tok, d), mod[:, 5], t_len)


def kernel(x, c, norm1_g, norm2_g, w_mod, b_mod, w_in, b_gate, a_qn_g, a_kn_g, b_qn_g, b_kn_g, c_w_alpha, c_b_alpha, c_on_g, w_oa, w_ob, w_oc, w_out, peer_wq, peer_subkeys, peer_u, peer_v):
    bsz, t_len, d = x.shape
    params = dict(norm1_g=norm1_g, norm2_g=norm2_g, w_in=w_in, b_gate=b_gate, a_qn_g=a_qn_g, a_kn_g=a_kn_g,
                  b_qn_g=b_qn_g, b_kn_g=b_kn_g, c_w_alpha=c_w_alpha, c_b_alpha=c_b_alpha, c_on_g=c_on_g,
                  w_oa=w_oa, w_ob=w_ob, w_oc=w_oc, w_out=w_out, peer_wq=peer_wq, peer_subkeys=peer_subkeys,
                  peer_u=peer_u, peer_v=peer_v)
    h2 = x.reshape(bsz * t_len, d)
    for l in range(w_in.shape[0]):
        mod = _mod(c, w_mod[l], b_mod[l]).reshape(bsz, 6, 1, d)
        h2 = _layer(h2, mod, t_len, {k: v[l] for k, v in params.items()})
    return h2.reshape(bsz, t_len, d)
```

```python
import functools

import jax
import jax.numpy as jnp
import numpy as np
from jax import lax
from jax.experimental import pallas as pl
from jax.experimental.pallas import tpu as pltpu

F32 = jnp.float32
BF16 = jnp.bfloat16

D_MODEL = 1024
EPS = 1e-6
CHUNK = 64
TOPK_MAX = 256
A_HEADS, A_DIM = 8, 64
IDX_HEADS, IDX_DIM = 4, 64
B_HEADS, B_DIM = 8, 64
C_HEADS, C_KEY, C_VAL, C_RANK, C_TAU = 4, 64, 128, 16, 16.0
PEER_HEADS, PEER_NKEYS, PEER_QDIM, PEER_TOPK = 8, 128, 256, 16
N_BRANCH = 3

LANE = 128

_SEGMENTS = (
    ("gl", 3072), ("aq", 512), ("bq", 512), ("bk", 512), ("bv", 512), ("cv", 512), ("cr", 512),
    ("iq", 256), ("cq", 256), ("ck", 256), ("akv", 128), ("ikw", 128), ("ca", 128),
)
COL = {}
_off = 0
for _name, _w in _SEGMENTS:
    COL[_name] = (_off, _w)
    _off += _w
Z_COLS = _off
Z_TILE = Z_COLS // 3

_REF_WIDTHS = (512, 64, 64, 256, 64, 4, 512, 512, 512, 256, 256, 512, 512, 16, 3072)


def _reorder_w_in(w_in):
    offs = np.cumsum((0,) + _REF_WIDTHS)
    p = [w_in[:, offs[i]:offs[i + 1]] for i in range(len(_REF_WIDTHS))]
    aq, ak, av, iq, ik, iw, bq, bk, bv, cq, ck, cv, cr, ca, gl = p
    d = w_in.shape[0]
    z = lambda n: jnp.zeros((d, n), w_in.dtype)
    cols = [gl, aq, bq, bk, bv, cv, cr, iq, cq, ck, ak, av, ik, iw, z(60), ca, z(112)]
    return jnp.concatenate(cols, axis=1).astype(BF16)


def _cb(name):
    off, w = COL[name]
    return off // w


def _mod_kernel(c_ref, w_ref, b_ref, o_ref):
    c = c_ref[...]
    ca = (c * jax.nn.sigmoid(c)).astype(BF16)
    o_ref[...] = jnp.dot(ca, w_ref[...].astype(BF16), preferred_element_type=F32) + b_ref[...]


def _mod(c, w_mod, b_mod):
    bsz, d = c.shape
    n = w_mod.shape[1]
    tn = 768
    return pl.pallas_call(
        _mod_kernel,
        grid=(n // tn,),
        in_specs=[pl.BlockSpec((bsz, d), lambda j: (0, 0)),
                  pl.BlockSpec((d, tn), lambda j: (0, j)),
                  pl.BlockSpec((1, tn), lambda j: (0, j))],
        out_specs=pl.BlockSpec((bsz, tn), lambda j: (0, j)),
        out_shape=jax.ShapeDtypeStruct((bsz, n), F32),
        name="adaln_mod",
    )(c, w_mod, b_mod.reshape(1, n))


def _inproj_kernel(h_ref, g_ref, sc_ref, sh_ref, w_ref, z_ref, n_scr):
    @pl.when(pl.program_id(1) == 0)
    def _():
        x = h_ref[...]
        y = x * lax.rsqrt(jnp.mean(x * x, axis=-1, keepdims=True) + EPS)
        n = y * g_ref[...] * (1.0 + sc_ref[0]) + sh_ref[0]
        n_scr[...] = n.astype(BF16)

    z_ref[...] = jnp.dot(n_scr[...], w_ref[...], preferred_element_type=F32).astype(BF16)


def _inproj(h2, norm_g, scale, shift, w_z, t_len, tm=512):
    n_tok, d = h2.shape
    tm = min(tm, t_len)
    bpt = t_len // tm
    return pl.pallas_call(
        _inproj_kernel,
        grid=(n_tok // tm, Z_COLS // Z_TILE),
        in_specs=[pl.BlockSpec((tm, d), lambda i, j: (i, 0)),
                  pl.BlockSpec((1, d), lambda i, j: (0, 0)),
                  pl.BlockSpec((1, 1, d), lambda i, j: (i // bpt, 0, 0)),
                  pl.BlockSpec((1, 1, d), lambda i, j: (i // bpt, 0, 0)),
                  pl.BlockSpec((d, Z_TILE), lambda i, j: (0, j))],
        out_specs=pl.BlockSpec((tm, Z_TILE), lambda i, j: (i, j)),
        out_shape=jax.ShapeDtypeStruct((n_tok, Z_COLS), BF16),
        scratch_shapes=[pltpu.VMEM((tm, d), BF16)],
        compiler_params=pltpu.CompilerParams(
            dimension_semantics=("parallel", "arbitrary"), vmem_limit_bytes=48 * 1024 * 1024),
        name="norm_inproj",
    )(h2, norm_g.reshape(1, d), scale, shift, w_z)


def _head_rms(x, gain):
    return x * lax.rsqrt(jnp.mean(x * x, axis=-1, keepdims=True) + EPS) * gain


DSA_QB = 256
DSA_GROUP = 1
DSA_BISECT = 24


def _dsa_kernel(aq_ref, akv_ref, iq_ref, ikwq_ref, ikwk_ref, qg_ref, kg_ref, o_ref, sm_scr, sel_scr,
                *, q_off, n_sel):
    qb = aq_ref.shape[1]
    tk = akv_ref.shape[1]
    q0 = (q_off + pl.program_id(1)) * qb
    t_pos = q0 + lax.broadcasted_iota(jnp.int32, (qb, 1), 0)
    limit = (t_pos // CHUNK + 1) * CHUNK
    s_pos = lax.broadcasted_iota(jnp.int32, (1, tk), 1)
    adm = s_pos < limit

    kidx = ikwk_ref[0][:, :IDX_DIM]
    qidx = iq_ref[0]
    w = ikwq_ref[0][:, IDX_DIM:IDX_DIM + IDX_HEADS].astype(F32)
    score = jnp.zeros((qb, tk), F32)
    for h in range(IDX_HEADS):
        rel = lax.dot_general(qidx[:, h * IDX_DIM:(h + 1) * IDX_DIM], kidx,
                              (((1,), (1,)), ((), ())), preferred_element_type=F32)
        score = score + w[:, h:h + 1] * jnp.maximum(rel, 0.0)
    neg_inf = jnp.float32(-jnp.inf)
    sm_scr[...] = jnp.where(adm, score, neg_inf)

    hi0 = jnp.max(sm_scr[...], axis=-1, keepdims=True)
    lo0 = jnp.min(jnp.where(adm, score, jnp.float32(jnp.inf)), axis=-1, keepdims=True)
    kf = jnp.float32(n_sel)

    def bisect(_, carry):
        lo, hi = carry
        mid = 0.5 * (lo + hi)
        cnt = jnp.sum(jnp.where(sm_scr[...] > mid, 1.0, 0.0), axis=-1, keepdims=True)
        below = cnt < kf
        return jnp.where(below, lo, mid), jnp.where(below, mid, hi)

    _, hi = lax.fori_loop(0, DSA_BISECT, bisect, (lo0, hi0))
    sm = sm_scr[...]
    v1 = jnp.max(jnp.where(sm <= hi, sm, neg_inf), axis=-1, keepdims=True)
    c_ge1 = jnp.sum(jnp.where(sm >= v1, 1.0, 0.0), axis=-1, keepdims=True)
    v2 = jnp.max(jnp.where(sm < v1, sm, neg_inf), axis=-1, keepdims=True)
    c_ge2 = jnp.sum(jnp.where(sm >= v2, 1.0, 0.0), axis=-1, keepdims=True)
    v3 = jnp.max(jnp.where(sm < v2, sm, neg_inf), axis=-1, keepdims=True)
    thr = jnp.where(c_ge1 >= kf, v1, jnp.where(c_ge2 >= kf, v2, v3))
    need = kf - jnp.sum(jnp.where(sm > thr, 1.0, 0.0), axis=-1, keepdims=True)

    r_i = lax.broadcasted_iota(jnp.int32, (LANE, LANE), 0)
    c_i = lax.broadcasted_iota(jnp.int32, (LANE, LANE), 1)
    incl = jnp.where(r_i <= c_i, 1.0, 0.0).astype(BF16)
    all_sel = limit <= n_sel
    run = jnp.zeros((qb, 1), F32)
    for j in range(tk // LANE):
        sl = slice(j * LANE, (j + 1) * LANE)
        smj = sm_scr[:, sl]
        eqj = jnp.where(smj == thr, 1.0, 0.0)
        inc = jnp.dot(eqj.astype(BF16), incl, preferred_element_type=F32)
        rank = inc - eqj + run
        picked = (smj > thr) | ((smj == thr) & (rank < need))
        admj = (j * LANE + lax.broadcasted_iota(jnp.int32, (1, LANE), 1)) < limit
        selj = (all_sel & admj) | (jnp.logical_not(all_sel) & picked)
        sel_scr[:, sl] = jnp.where(selj, 0.0, neg_inf)
        run = run + inc[:, LANE - 1:LANE]

    kv = akv_ref[0].astype(F32)
    k = _head_rms(kv[:, :A_DIM], kg_ref[...]).astype(BF16)
    v = akv_ref[0][:, A_DIM:2 * A_DIM]
    aq = aq_ref[0].astype(F32)
    scale = A_DIM ** -0.5
    for h in range(A_HEADS):
        qh = (_head_rms(aq[:, h * A_DIM:(h + 1) * A_DIM], qg_ref[...]) * scale).astype(BF16)
        logits = lax.dot_general(qh, k, (((1,), (1,)), ((), ())), preferred_element_type=F32) + sel_scr[...]
        m = jnp.max(logits, axis=-1, keepdims=True)
        p = jnp.exp(logits - m)
        denom = jnp.sum(p, axis=-1, keepdims=True)
        oh = jnp.dot(p.astype(BF16), v, preferred_element_type=F32) / denom
        o_ref[0, :, h * A_DIM:(h + 1) * A_DIM] = oh.astype(o_ref.dtype)


def _dsa(z3, qg, kg):
    bsz, t_len, _ = z3.shape
    n_sel = min(TOPK_MAX, t_len // 4)
    span = DSA_QB * DSA_GROUP
    outs = []
    for g in range(t_len // span):
        tk = span * (g + 1)
        q_off = g * DSA_GROUP
        outs.append(pl.pallas_call(
            functools.partial(_dsa_kernel, q_off=q_off, n_sel=n_sel),
            grid=(bsz, DSA_GROUP),
            in_specs=[pl.BlockSpec((1, DSA_QB, 512), lambda b, i, q_off=q_off: (b, q_off + i, _cb("aq"))),
                      pl.BlockSpec((1, tk, LANE), lambda b, i: (b, 0, _cb("akv"))),
                      pl.BlockSpec((1, DSA_QB, 256), lambda b, i, q_off=q_off: (b, q_off + i, _cb("iq"))),
                      pl.BlockSpec((1, DSA_QB, LANE), lambda b, i, q_off=q_off: (b, q_off + i, _cb("ikw"))),
                      pl.BlockSpec((1, tk, LANE), lambda b, i: (b, 0, _cb("ikw"))),
                      pl.BlockSpec((1, A_DIM), lambda b, i: (0, 0)),
                      pl.BlockSpec((1, A_DIM), lambda b, i: (0, 0))],
            out_specs=pl.BlockSpec((1, DSA_QB, 512), lambda b, i: (b, i, 0)),
            out_shape=jax.ShapeDtypeStruct((bsz, span, 512), BF16),
            scratch_shapes=[pltpu.VMEM((DSA_QB, tk), F32), pltpu.VMEM((DSA_QB, tk), F32)],
            compiler_params=pltpu.CompilerParams(
                dimension_semantics=("parallel", "arbitrary"), vmem_limit_bytes=48 * 1024 * 1024),
            name=f"dsa_attention_g{g}",
        )(z3, z3, z3, z3, z3, qg.reshape(1, A_DIM), kg.reshape(1, A_DIM)))
    return jnp.concatenate(outs, axis=1)


def _split_dot(x, m, x_is_lhs=True):
    hi = x.astype(BF16)
    lo = (x - hi.astype(F32)).astype(BF16)
    if x_is_lhs:
        return jnp.dot(hi, m, preferred_element_type=F32) + jnp.dot(lo, m, preferred_element_type=F32)
    return jnp.dot(m, hi, preferred_element_type=F32) + jnp.dot(m, lo, preferred_element_type=F32)


def _split_dot2(x, m2):
    hi = x.astype(BF16)
    lo = (x - hi.astype(F32)).astype(BF16)
    return jnp.dot(jnp.concatenate([hi, lo], axis=1), m2, preferred_element_type=F32)


SB_QB = 256
SB_KB = 128
SB_SWEEP = 4


def _sb_kernel(q_ref, k_ref, v_ref, qg_ref, kg_ref, o_ref, kn_scr):
    qi = pl.program_id(2)
    qb = q_ref.shape[1]
    hd = B_DIM

    @pl.when(qi == 0)
    def _():
        kf = k_ref[0].astype(F32)
        for h in range(2):
            kn_scr[h] = _head_rms(kf[:, h * hd:(h + 1) * hd], kg_ref[...]).astype(BF16)

    q = q_ref[0].astype(F32)
    scale = hd ** -0.5
    qn = [(_head_rms(q[:, h * hd:(h + 1) * hd], qg_ref[...]) * scale).astype(BF16) for h in range(2)]
    kb = SB_KB
    r_i = lax.broadcasted_iota(jnp.int32, (kb, kb), 0)
    c_i = lax.broadcasted_iota(jnp.int32, (kb, kb), 1)
    m_incl = jnp.where(r_i >= c_i, 1.0, 0.0).astype(BF16)
    m_incl2 = jnp.concatenate([m_incl, m_incl], axis=0)
    t_pos = qi * qb + lax.broadcasted_iota(jnp.int32, (qb, 1), 0)
    lane = lax.broadcasted_iota(jnp.int32, (1, 2 * hd), 1)
    last_sweep = ((qi + 1) * (qb // kb) - 1) // SB_SWEEP

    def body(i, carry):
        sb = last_sweep - i
        carry = list(carry)
        blocks = list(reversed(range(SB_SWEEP)))
        starts = [pl.multiple_of((sb * SB_SWEEP + j) * kb, kb) for j in blocks]
        stricts = [(ks + lax.broadcasted_iota(jnp.int32, (1, kb), 1)) < t_pos for ks in starts]
        zs = [[lax.dot_general(qn[h], kn_scr[h, pl.ds(ks, kb), :], (((1,), (1,)), ((), ())),
                               preferred_element_type=F32) for h in range(2)] for ks in starts]
        drops = [[jnp.where(st, jnp.maximum(z, 0.0) + jnp.log(1.0 + jnp.exp(-jnp.abs(z))), 0.0) for z in zb]
                 for st, zb in zip(stricts, zs)]
        sums = [[_split_dot2(d, m_incl2) for d in db] for db in drops]
        probs = []
        for st, zb, sb_ in zip(stricts, zs, sums):
            row = []
            for h in range(2):
                row.append(jnp.where(st, jnp.exp(zb[h] - sb_[h] - carry[2 * h]), 0.0).astype(BF16))
                carry[2 * h] = carry[2 * h] + sb_[h][:, 0:1]
            probs.append(row)
        for ks, row in zip(starts, probs):
            vblk = v_ref[0, pl.ds(ks, kb), :]
            for h in range(2):
                carry[2 * h + 1] = carry[2 * h + 1] + jnp.dot(row[h], vblk, preferred_element_type=F32)
        return tuple(carry)

    init = (jnp.zeros((qb, 1), F32), jnp.zeros((qb, 2 * hd), F32)) * 2
    res = lax.fori_loop(0, last_sweep + 1, body, init)
    o_ref[0] = jnp.where(lane < hd, res[1], res[3]).astype(o_ref.dtype)


def _stickbreak(z3, qg, kg):
    bsz, t_len, _ = z3.shape
    nq = t_len // SB_QB
    npair = B_HEADS // 2
    cq, ck, cv = COL["bq"][0] // LANE, COL["bk"][0] // LANE, COL["bv"][0] // LANE
    return pl.pallas_call(
        _sb_kernel,
        grid=(bsz, npair, nq),
        in_specs=[pl.BlockSpec((1, SB_QB, LANE), lambda b, p, i: (b, i, cq + p)),
                  pl.BlockSpec((1, t_len, LANE), lambda b, p, i: (b, 0, ck + p)),
                  pl.BlockSpec((1, t_len, LANE), lambda b, p, i: (b, 0, cv + p)),
                  pl.BlockSpec((1, B_DIM), lambda b, p, i: (0, 0)),
                  pl.BlockSpec((1, B_DIM), lambda b, p, i: (0, 0))],
        out_specs=pl.BlockSpec((1, SB_QB, LANE), lambda b, p, i: (b, i, p)),
        out_shape=jax.ShapeDtypeStruct((bsz, t_len, B_HEADS * B_DIM), BF16),
        scratch_shapes=[pltpu.VMEM((2, t_len, B_DIM), BF16)],
        compiler_params=pltpu.CompilerParams(
            dimension_semantics=("parallel", "parallel", "arbitrary"), vmem_limit_bytes=48 * 1024 * 1024),
        name="stickbreak_attention",
    )(z3, z3, z3, qg.reshape(1, B_DIM), kg.reshape(1, B_DIM))


GLA_SUB = 16
GLA_UNROLL = 4


def _gla_kernel(q_ref, k_ref, v_ref, r_ref, a_ref, wa_ref, ba_ref, og_ref, o_ref, st_scr):
    t_len = q_ref.shape[1]
    nc = t_len // CHUNK
    dk, dv, sub = C_KEY, C_VAL, GLA_SUB
    r_i = lax.broadcasted_iota(jnp.int32, (CHUNK, CHUNK), 0)
    c_i = lax.broadcasted_iota(jnp.int32, (CHUNK, CHUNK), 1)
    l_incl = jnp.where(r_i >= c_i, 1.0, 0.0).astype(BF16)
    st_scr[...] = jnp.zeros_like(st_scr)
    neg_inf = jnp.float32(-jnp.inf)
    pr = lax.broadcasted_iota(jnp.int32, (sub * sub, 1), 0)
    pair_ok = pr % sub <= pr // sub
    fr = lax.broadcasted_iota(jnp.int32, (sub, sub * sub), 0)
    fc = lax.broadcasted_iota(jnp.int32, (sub, sub * sub), 1)
    fold_t = jnp.where(fc // sub == fr, 1.0, 0.0).astype(BF16)
    hr = lax.broadcasted_iota(jnp.int32, (2 * dk, 2 * dv), 0)
    hc = lax.broadcasted_iota(jnp.int32, (2 * dk, 2 * dv), 1)
    head_ones = jnp.where(hr // dk == hc // dv, 1.0, 0.0).astype(BF16)

    def per_t(x):
        return jnp.broadcast_to(x[:, None, :], (sub, sub, x.shape[1])).reshape(sub * sub, x.shape[1])

    def per_s(x):
        return jnp.broadcast_to(x[None, :, :], (sub, sub, x.shape[1])).reshape(sub * sub, x.shape[1])

    def trip(i, _):
        rows = [pl.ds(pl.multiple_of((i * GLA_UNROLL + u) * CHUNK, CHUNK), CHUNK) for u in range(GLA_UNROLL)]
        pres = [jnp.dot(a_ref[0, r, :], wa_ref[...], preferred_element_type=F32) + ba_ref[...] for r in rows]
        la2s = [(jnp.minimum(p, 0.0) - jnp.log(1.0 + jnp.exp(-jnp.abs(p)))) * (1.0 / C_TAU) for p in pres]
        b2s = [_split_dot(la2, l_incl, x_is_lhs=False) for la2 in la2s]
        q2s = [q_ref[0, r, :].astype(F32) * (dk ** -0.5) for r in rows]
        k2s = [k_ref[0, r, :].astype(F32) for r in rows]
        v2fs = [v_ref[0, r, :].astype(F32) for r in rows]

        terms, vss = [], []
        for b2, q2, k2, v2f in zip(b2s, q2s, k2s, v2fs):
            for j in range(CHUNK // sub):
                sl = slice(j * sub, (j + 1) * sub)
                bt, qt = (per_t(x[sl]) for x in (b2, q2))
                bs, ks_, vs = (per_s(x[sl]) for x in (b2, k2, v2f))
                terms.append(qt * ks_ * jnp.exp(jnp.where(pair_ok, bt - bs, neg_inf)))
                vss.append(vs)
        w2s = [_split_dot(t, head_ones) for t in terms]
        diags = [jnp.dot(fold_t, (w2 * vs).astype(BF16), preferred_element_type=F32)
                 for w2, vs in zip(w2s, vss)]

        per_chunk = []
        for u in range(GLA_UNROLL):
            heads = []
            for h in range(2):
                b = b2s[u][:, h * dk:(h + 1) * dk]
                q = q2s[u][:, h * dk:(h + 1) * dk]
                k = k2s[u][:, h * dk:(h + 1) * dk]
                v = v_ref[0, rows[u], h * dv:(h + 1) * dv]
                atts = []
                for j in range(1, CHUNK // sub):
                    lo_r = j * sub
                    ref_row = b[lo_r - 1:lo_r, :]
                    qs = (q[lo_r:lo_r + sub] * jnp.exp(b[lo_r:lo_r + sub] - ref_row)).astype(BF16)
                    ks = (k[:lo_r] * jnp.exp(ref_row - b[:lo_r])).astype(BF16)
                    atts.append(lax.dot_general(qs, ks, (((1,), (1,)), ((), ())), preferred_element_type=F32))
                b_last = b[CHUNK - 1:CHUNK, :]
                heads.append(dict(
                    v=v, atts=atts, qe=(q * jnp.exp(b)).astype(BF16), decay=jnp.exp(b_last),
                    kd=(k * jnp.exp(b_last - b)).astype(BF16)))
            per_chunk.append(heads)
        for u in range(GLA_UNROLL):
            for h in range(2):
                hd = per_chunk[u][h]
                intra = [diags[u * (CHUNK // sub)][:, h * dv:(h + 1) * dv]]
                for j in range(1, CHUNK // sub):
                    lo_r = j * sub
                    intra.append(diags[u * (CHUNK // sub) + j][:, h * dv:(h + 1) * dv]
                                 + jnp.dot(hd["atts"][j - 1].astype(BF16), hd["v"][:lo_r], preferred_element_type=F32))
                hd["intra"] = jnp.concatenate(intra, axis=0)
                hd["upd"] = lax.dot_general(hd["v"], hd["kd"], (((0,), (0,)), ((), ())),
                                            preferred_element_type=F32)

        for h in range(2):
            st = st_scr[h]
            for u in range(GLA_UNROLL):
                hd = per_chunk[u][h]
                o = lax.dot_general(hd["qe"], st.astype(BF16), (((1,), (1,)), ((), ())),
                                    preferred_element_type=F32) + hd["intra"]
                st = st * hd["decay"] + hd["upd"]
                y = o * lax.rsqrt(jnp.mean(o * o, axis=-1, keepdims=True) + EPS) * og_ref[:, h * dv:(h + 1) * dv]
                r = r_ref[0, rows[u], h * dv:(h + 1) * dv].astype(F32)
                o_ref[0, rows[u], h * dv:(h + 1) * dv] = (y * (r * jax.nn.sigmoid(r))).astype(o_ref.dtype)
            st_scr[h] = st
        return 0

    lax.fori_loop(0, nc // GLA_UNROLL, trip, 0)


def _gla(z3, w_alpha, b_alpha, out_gain):
    bsz, t_len, _ = z3.shape
    npair = C_HEADS // 2
    wa = jnp.zeros((LANE, C_HEADS * C_KEY), F32).at[:C_RANK].set(w_alpha).astype(BF16)
    cq, ck = COL["cq"][0] // LANE, COL["ck"][0] // LANE
    cv, cr, ca = COL["cv"][0] // 256, COL["cr"][0] // 256, COL["ca"][0] // LANE
    return pl.pallas_call(
        _gla_kernel,
        grid=(bsz, npair),
        in_specs=[pl.BlockSpec((1, t_len, LANE), lambda b, p: (b, 0, cq + p)),
                  pl.BlockSpec((1, t_len, LANE), lambda b, p: (b, 0, ck + p)),
                  pl.BlockSpec((1, t_len, 256), lambda b, p: (b, 0, cv + p)),
                  pl.BlockSpec((1, t_len, 256), lambda b, p: (b, 0, cr + p)),
                  pl.BlockSpec((1, t_len, LANE), lambda b, p: (b, 0, ca)),
                  pl.BlockSpec((LANE, LANE), lambda b, p: (0, p)),
                  pl.BlockSpec((1, LANE), lambda b, p: (0, p)),
                  pl.BlockSpec((1, 256), lambda b, p: (0, p))],
        out_specs=pl.BlockSpec((1, t_len, 256), lambda b, p: (b, 0, p)),
        out_shape=jax.ShapeDtypeStruct((bsz, t_len, C_HEADS * C_VAL), BF16),
        scratch_shapes=[pltpu.VMEM((2, C_VAL, C_KEY), F32)],
        compiler_params=pltpu.CompilerParams(
            dimension_semantics=("parallel", "parallel"), vmem_limit_bytes=48 * 1024 * 1024),
        name="gla_attention",
    )(z3, z3, z3, z3, z3, wa, b_alpha.reshape(1, -1), out_gain.reshape(1, -1))


def _merge_kernel(h_ref, g0_ref, g1_ref, g2_ref, bg_ref, ya_ref, yb_ref, yc_ref, woa_ref, wob_ref, woc_ref,
                  wout_ref, gate_ref, ng_ref, sc_ref, sh_ref, wpq_ref, h1_ref, n2_ref, q_ref):
    d = h_ref.shape[1]
    merged = jnp.zeros(h_ref.shape, F32)
    for i, (g_ref, y_ref, w_ref) in enumerate(((g0_ref, ya_ref, woa_ref), (g1_ref, yb_ref, wob_ref),
                                               (g2_ref, yc_ref, woc_ref))):
        gate = jax.nn.sigmoid(g_ref[...].astype(F32) + bg_ref[:, i * d:(i + 1) * d])
        merged = merged + gate * jnp.dot(y_ref[...], w_ref[...], preferred_element_type=F32)
    y = jnp.dot(merged.astype(BF16), wout_ref[...], preferred_element_type=F32)
    h1 = h_ref[...] + gate_ref[0] * y
    h1_ref[...] = h1
    n = h1 * lax.rsqrt(jnp.mean(h1 * h1, axis=-1, keepdims=True) + EPS)
    n2 = (n * ng_ref[...] * (1.0 + sc_ref[0]) + sh_ref[0]).astype(BF16)
    n2_ref[...] = n2
    q_ref[...] = jnp.dot(n2, wpq_ref[...], preferred_element_type=F32).astype(BF16)


def _merge(h2, z, ya, yb, yc, b_gate, w_oa, w_ob, w_oc, w_out, gate1, norm_g, scale, shift, w_pq, t_len, tm=256):
    n_tok, d = h2.shape
    bpt = t_len // tm
    nq = w_pq.shape[1]
    row = lambda i: (i, 0)
    fixed = lambda i: (0, 0)
    per_b = lambda i: (i // bpt, 0, 0)
    bw = ya.shape[1]
    return pl.pallas_call(
        _merge_kernel,
        grid=(n_tok // tm,),
        in_specs=[pl.BlockSpec((tm, d), row),
                  pl.BlockSpec((tm, d), lambda i: (i, 0)),
                  pl.BlockSpec((tm, d), lambda i: (i, 1)),
                  pl.BlockSpec((tm, d), lambda i: (i, 2)),
                  pl.BlockSpec((1, N_BRANCH * d), fixed),
                  pl.BlockSpec((tm, bw), row), pl.BlockSpec((tm, bw), row), pl.BlockSpec((tm, bw), row),
                  pl.BlockSpec((bw, d), fixed), pl.BlockSpec((bw, d), fixed), pl.BlockSpec((bw, d), fixed),
                  pl.BlockSpec((d, d), fixed),
                  pl.BlockSpec((1, 1, d), per_b),
                  pl.BlockSpec((1, d), fixed),
                  pl.BlockSpec((1, 1, d), per_b), pl.BlockSpec((1, 1, d), per_b),
                  pl.BlockSpec((d, nq), fixed)],
        out_specs=[pl.BlockSpec((tm, d), row), pl.BlockSpec((tm, d), row), pl.BlockSpec((tm, nq), row)],
        out_shape=[jax.ShapeDtypeStruct((n_tok, d), F32), jax.ShapeDtypeStruct((n_tok, d), BF16),
                   jax.ShapeDtypeStruct((n_tok, nq), BF16)],
        compiler_params=pltpu.CompilerParams(
            dimension_semantics=("parallel",), vmem_limit_bytes=48 * 1024 * 1024),
        name="merge_outproj_peerq",
    )(h2, z, z, z, b_gate.reshape(1, -1), ya, yb, yc, w_oa.astype(BF16), w_ob.astype(BF16), w_oc.astype(BF16),
      w_out.astype(BF16), gate1, norm_g.reshape(1, d), scale, shift, w_pq.astype(BF16))


PEER_TB = 128


def _take_top_steps(vals, ids, k, out_v, out_i):
    neg_inf = jnp.float32(-jnp.inf)
    big = jnp.float32(2.0 ** 30)
    for _ in range(k):
        m = jnp.max(vals, axis=0, keepdims=True)
        first = jnp.min(jnp.where(vals == m, ids, big), axis=0, keepdims=True)
        out_v.append(m)
        out_i.append(first)
        vals = jnp.where(ids == first, neg_inf, vals)
        yield


PEER_ROUNDS = 3 * PEER_TOPK


def _retrieve_head_steps(q_ref, sk_ref, h, eid_dst, g_dst):
    tb = q_ref.shape[0]
    hq = PEER_QDIM // 2
    k = PEER_TOPK
    n_exp = PEER_NKEYS * PEER_NKEYS
    key_id = lax.broadcasted_iota(jnp.int32, (PEER_NKEYS, tb), 0).astype(F32)
    n_j = [k if i == 0 else 8 for i in range(k)]
    pos = jnp.concatenate([(i * k + lax.broadcasted_iota(jnp.int32, (n_j[i], tb), 0)) * n_exp
                           for i in range(k)], axis=0).astype(F32)
    tops, topi = [], []
    for p in range(2):
        col = pl.multiple_of((2 * h + p) * hq, hq)
        qhp = q_ref[:, pl.ds(col, hq)]
        s = lax.dot_general(sk_ref[h, p], qhp, (((1,), (1,)), ((), ())), preferred_element_type=F32)
        v, ids = [], []
        yield from _take_top_steps(s, key_id, k, v, ids)
        tops.append(v)
        topi.append(ids)
    b_s = jnp.concatenate(tops[1], axis=0)
    b_i = jnp.concatenate(topi[1], axis=0)
    cand = jnp.concatenate([tops[0][i] + b_s[:n_j[i]] for i in range(k)], axis=0)
    cid = jnp.concatenate([topi[0][i] * PEER_NKEYS + b_i[:n_j[i]] for i in range(k)], axis=0)
    best, tag = [], []
    yield from _take_top_steps(cand, pos + cid, k, best, tag)
    best = jnp.concatenate(best, axis=0)
    e = jnp.exp(best - best[0:1])
    rows = pl.ds(pl.multiple_of(h * k, k), k)
    g_dst[rows, :] = e / jnp.sum(e, axis=0, keepdims=True)
    eid_dst[rows, :] = jnp.concatenate(tag, axis=0).astype(jnp.int32) % n_exp


PEER_VMEM_LIMIT = 50 * 1024 * 1024
SUBLANE = 8
PACK = 2 * SUBLANE
PEER_GRP = LANE // PACK
PEER_TOK_UNROLL = 8


def _pair_tiles(tab):
    e, d = tab.shape
    bits = lax.bitcast_convert_type(tab.astype(BF16), jnp.uint16).astype(jnp.uint32)
    bits = bits.reshape(e // 2, 2, d // LANE, LANE)
    return (bits[:, 0] | (bits[:, 1] << 16)).reshape(e // 2 * SUBLANE, LANE)


def _tile(tab_ref, word_row):
    words = tab_ref[pl.ds(pl.multiple_of(word_row, SUBLANE), SUBLANE), :]
    return pltpu.bitcast(words, BF16)


def _dup_tiles(x2):
    n, d = x2.shape
    x4 = x2.reshape(n, d // LANE, 1, LANE)
    return jnp.broadcast_to(x4, (n, d // LANE, 2, LANE)).reshape(n, PACK, LANE)


def _expand_mats(slots):
    col = np.arange(slots * PACK)
    s_of = (col // LANE) * PEER_GRP + (col % LANE) // PACK
    h_of = col % 2
    m = np.zeros((2, slots, slots * PACK), np.float32)
    for h in range(2):
        m[h, s_of[h_of == h], col[h_of == h]] = 1.0
    return jnp.asarray(m, BF16)


def _table_spec(tab):
    return pl.BlockSpec(tab.shape, lambda i: (0, 0), pipeline_mode=pl.Buffered(1))


PEER_STAGE = 2


def _index_stage(slots):
    return [pltpu.SMEM((PEER_STAGE, PEER_TOK_UNROLL, slots), jnp.int32), pltpu.SemaphoreType.DMA((PEER_STAGE,))]


def _for_each_token(rows_ref, idx_smem, sems, tb, token_fn, side_steps=None, steps_per_token=0):
    u = PEER_TOK_UNROLL
    ntrip = tb // u

    def stage(trip, buf):
        src = rows_ref.at[pl.ds(pl.multiple_of(trip * u, u), u), :]
        return pltpu.make_async_copy(src, idx_smem.at[buf], sems.at[buf])

    for b in range(PEER_STAGE):
        stage(b, b).start()

    def outer(j, carry):
        side = side_steps(j) if side_steps is not None else iter(())
        for b in range(PEER_STAGE):
            trip = j * PEER_STAGE + b
            stage(trip, b).wait()
            for tt in range(u):
                token_fn(trip * u + tt, lambda s, b=b, tt=tt: idx_smem[b, tt, s], trip, tt)
                for _ in range(steps_per_token):
                    next(side, None)

            @pl.when(trip + PEER_STAGE < ntrip)
            def _():
                stage(trip + PEER_STAGE, b).start()
        for _ in side:
            pass
        return carry

    lax.fori_loop(0, ntrip // PEER_STAGE, outer, 0)


def _peer_u_kernel(q_ref, sk_ref, xx_ref, ex_ref, tab_ref, cw_ref, rows_ref, eid_scr, g_scr, rows_scr,
                   ae_scr, ao_scr, idx_smem, sems):
    step = pl.program_id(0)
    cur = step % 2
    prev = 1 - cur
    tb = xx_ref.shape[0]
    slots = ae_scr.shape[0]

    @pl.when(step == 0)
    def _():
        rows_scr[1] = jnp.zeros(rows_scr.shape[1:], rows_scr.dtype)
        eid_scr[1] = jnp.zeros(eid_scr.shape[1:], eid_scr.dtype)
        g_scr[1] = jnp.zeros(g_scr.shape[1:], g_scr.dtype)

    lane = lax.broadcasted_iota(jnp.int32, (SUBLANE, tb), 1)
    r_i = lax.broadcasted_iota(jnp.int32, (PACK, LANE), 0)
    k_i = lax.broadcasted_iota(jnp.int32, (PACK, LANE), 1)
    sel = jnp.where((k_i // PACK == r_i % SUBLANE) & (k_i % 2 == r_i // SUBLANE), 1.0, 0.0).astype(BF16)

    def token(t, idx, trip, tt):
        xt = xx_ref[t]
        xg = jnp.concatenate([xt] * PEER_GRP, axis=0)
        here = lane == t
        for g in range(0, slots // PEER_GRP, 2):
            halves = []
            for gg in (g, g + 1):
                tiles = [_tile(tab_ref, idx(gg * PEER_GRP + p)) for p in range(PEER_GRP)]
                halves.append(jnp.concatenate(tiles, axis=0) * xg)
            part = jnp.dot(sel, jnp.concatenate(halves, axis=1), preferred_element_type=F32)
            for j, gg in enumerate((g, g + 1)):
                tot = jnp.sum(part[:, j * LANE:(j + 1) * LANE], axis=1, keepdims=True)
                rs = slice(gg * PEER_GRP, (gg + 1) * PEER_GRP)
                ae_scr[rs, :] = jnp.where(here, tot[:SUBLANE], ae_scr[rs, :])
                ao_scr[rs, :] = jnp.where(here, tot[SUBLANE:], ao_scr[rs, :])

    tokens_per_trip = PEER_TOK_UNROLL * PEER_STAGE
    assert tb // tokens_per_trip == PEER_HEADS and PEER_ROUNDS % tokens_per_trip == 0
    _for_each_token(rows_scr.at[prev], idx_smem, sems, tb, token,
                    side_steps=lambda h: _retrieve_head_steps(q_ref, sk_ref, h, eid_scr.at[cur], g_scr.at[cur]),
                    steps_per_token=PEER_ROUNDS // tokens_per_trip)

    eid_p = eid_scr[prev]
    odd = (eid_p & 1) == 1
    act = jnp.where(odd, ao_scr[...], ae_scr[...])
    coef = g_scr[prev] * (0.5 * act * (1.0 + lax.erf(act * (2.0 ** -0.5))))
    zero = jnp.zeros_like(coef)
    tn = (((0,), (0,)), ((), ()))
    cw = lax.dot_general(jnp.where(odd, zero, coef).astype(BF16), ex_ref[0], tn, preferred_element_type=F32)
    cw = cw + lax.dot_general(jnp.where(odd, coef, zero).astype(BF16), ex_ref[1], tn, preferred_element_type=F32)
    cw_ref[...] = cw

    rows_cur = jnp.transpose((eid_scr[cur] >> 1) * SUBLANE)
    rows_scr[cur] = rows_cur
    rows_ref[0] = rows_cur


def _peer_u(q, sub_keys, xx, tab_u, expand, tb):
    n_tok = xx.shape[0]
    nblk = n_tok // tb
    slots = PEER_HEADS * PEER_TOPK
    last = nblk - 1
    return pl.pallas_call(
        _peer_u_kernel,
        grid=(nblk + 1,),
        in_specs=[pl.BlockSpec((tb, q.shape[1]), lambda i: (jnp.minimum(i, last), 0)),
                  pl.BlockSpec(sub_keys.shape, lambda i: (0, 0, 0, 0)),
                  pl.BlockSpec((tb,) + xx.shape[1:], lambda i: (jnp.maximum(i - 1, 0), 0, 0)),
                  pl.BlockSpec(expand.shape, lambda i: (0, 0, 0)),
                  _table_spec(tab_u)],
        out_specs=[pl.BlockSpec((tb, slots * PACK), lambda i: (jnp.maximum(i - 1, 0), 0)),
                   pl.BlockSpec((1, tb, slots), lambda i: (jnp.minimum(i, last), 0, 0))],
        out_shape=[jax.ShapeDtypeStruct((n_tok, slots * PACK), F32),
                   jax.ShapeDtypeStruct((nblk, tb, slots), jnp.int32)],
        scratch_shapes=[pltpu.VMEM((2, slots, tb), jnp.int32), pltpu.VMEM((2, slots, tb), F32),
                        pltpu.VMEM((2, tb, slots), jnp.int32),
                        pltpu.VMEM((slots, tb), F32), pltpu.VMEM((slots, tb), F32)] + _index_stage(slots),
        compiler_params=pltpu.CompilerParams(
            dimension_semantics=("arbitrary",), vmem_limit_bytes=PEER_VMEM_LIMIT),
        name="peer_retrieve_u",
    )(q, sub_keys.astype(BF16), xx, expand, tab_u)


def _peer_v_kernel(rows_ref, cw_ref, tab_ref, o_ref, idx_smem, sems, *, slots):
    tb = o_ref.shape[0]
    r_i = lax.broadcasted_iota(jnp.int32, (PACK, LANE), 0)
    k_i = lax.broadcasted_iota(jnp.int32, (PACK, LANE), 1)
    keep = (k_i % PACK) // 2 == r_i % SUBLANE
    first = r_i < SUBLANE

    def token(t, idx, trip, tt):
        acc = jnp.zeros((PACK, 2 * LANE), F32)
        first_row = pl.multiple_of(trip * PEER_TOK_UNROLL, PEER_TOK_UNROLL)
        cwt = cw_ref[pl.ds(first_row, PEER_TOK_UNROLL), :][tt:tt + 1, :]
        for g in range(0, slots // PEER_GRP, 2):
            halves = [jnp.concatenate([_tile(tab_ref, idx(gg * PEER_GRP + p)) for p in range(PEER_GRP)],
                                      axis=0) for gg in (g, g + 1)]
            coef = jnp.where(first, jnp.broadcast_to(cwt[:, g * LANE:(g + 1) * LANE], (PACK, LANE)),
                             jnp.broadcast_to(cwt[:, (g + 1) * LANE:(g + 2) * LANE], (PACK, LANE)))
            lhs = jnp.where(keep, coef, 0.0).astype(BF16)
            acc = acc + jnp.dot(lhs, jnp.concatenate(halves, axis=1), preferred_element_type=F32)
        o_ref[t] = acc[:SUBLANE, :LANE] + acc[SUBLANE:, LANE:]

    _for_each_token(rows_ref.at[0], idx_smem, sems, tb, token)


def _peer_v(rows, cw, tab_v, tb):
    n_tok = cw.shape[0]
    slots = cw.shape[1] // PACK
    return pl.pallas_call(
        functools.partial(_peer_v_kernel, slots=slots),
        grid=(n_tok // tb,),
        in_specs=[pl.BlockSpec((1, tb, slots), lambda i: (i, 0, 0)),
                  pl.BlockSpec((tb, slots * PACK), lambda i: (i, 0)),
                  _table_spec(tab_v)],
        out_specs=pl.BlockSpec((tb, SUBLANE, LANE), lambda i: (i, 0, 0)),
        out_shape=jax.ShapeDtypeStruct((n_tok, SUBLANE, LANE), F32),
        scratch_shapes=_index_stage(slots),
        compiler_params=pltpu.CompilerParams(
            dimension_semantics=("arbitrary",), vmem_limit_bytes=PEER_VMEM_LIMIT),
        name="peer_v",
    )(rows, cw, tab_v)


def _residual_kernel(h_ref, p_ref, gate_ref, o_ref):
    o_ref[...] = h_ref[...] + gate_ref[0] * p_ref[...]


def _residual(h2, p2, gate, t_len, tm=512):
    n_tok, d = h2.shape
    bpt = t_len // tm
    return pl.pallas_call(
        _residual_kernel,
        grid=(n_tok // tm,),
        in_specs=[pl.BlockSpec((tm, d), lambda i: (i, 0)), pl.BlockSpec((tm, d), lambda i: (i, 0)),
                  pl.BlockSpec((1, 1, d), lambda i: (i // bpt, 0, 0))],
        out_specs=pl.BlockSpec((tm, d), lambda i: (i, 0)),
        out_shape=jax.ShapeDtypeStruct((n_tok, d), F32),
        compiler_params=pltpu.CompilerParams(dimension_semantics=("parallel",)),
        name="peer_residual",
    )(h2, p2, gate)


def _layer(h2, mod, t_len, p):
    n_tok, d = h2.shape
    bsz = n_tok // t_len
    z = _inproj(h2, p["norm1_g"], mod[:, 1], mod[:, 0], _reorder_w_in(p["w_in"]), t_len)
    z3 = z.reshape(bsz, t_len, Z_COLS)
    ya = _dsa(z3, p["a_qn_g"], p["a_kn_g"]).reshape(n_tok, -1)
    yb = _stickbreak(z3, p["b_qn_g"], p["b_kn_g"]).reshape(n_tok, -1)
    yc = _gla(z3, p["c_w_alpha"], p["c_b_alpha"], p["c_on_g"]).reshape(n_tok, -1)
    h1, n2, q = _merge(h2, z, ya, yb, yc, p["b_gate"], p["w_oa"], p["w_ob"], p["w_oc"], p["w_out"],
                       mod[:, 2], p["norm2_g"], mod[:, 4], mod[:, 3], p["peer_wq"], t_len)
    cw, rows = _peer_u(q, p["peer_subkeys"], _dup_tiles(n2), _pair_tiles(p["peer_u"]),
                       _expand_mats(PEER_HEADS * PEER_TOPK), PEER_TB)
    out = _peer_v(rows, cw, _pair_tiles(p["peer_v"]), PEER_TB)
    return _residual(h1, out.reshape(n_tok, d), mod[:, 5], t_len)


def kernel(x, c, norm1_g, norm2_g, w_mod, b_mod, w_in, b_gate, a_qn_g, a_kn_g, b_qn_g, b_kn_g, c_w_alpha, c_b_alpha, c_on_g, w_oa, w_ob, w_oc, w_out, peer_wq, peer_subkeys, peer_u, peer_v):
    bsz, t_len, d = x.shape
    params = dict(norm1_g=norm1_g, norm2_g=norm2_g, w_in=w_in, b_gate=b_gate, a_qn_g=a_qn_g, a_kn_g=a_kn_g,
                  b_qn_g=b_qn_g, b_kn_g=b_kn_g, c_w_alpha=c_w_alpha, c_b_alpha=c_b_alpha, c_on_g=c_on_g,
                  w_oa=w_oa, w_ob=w_ob, w_oc=w_oc, w_out=w_out, peer_wq=peer_wq, peer_subkeys=peer_subkeys,
                  peer_u=peer_u, peer_v=peer_v)
    h2 = x.reshape(bsz * t_len, d)
    for l in range(w_in.shape[0]):
        mod = _mod(c, w_mod[l], b_mod[l]).reshape(bsz, 6, 1, d)
        h2 = _layer(h2, mod, t_len, {k: v[l] for k, v in params.items()})
    return h2.reshape(bsz, t_len, d)
```

```python
import functools

import jax
import jax.numpy as jnp
import numpy as np
from jax import lax
from jax.experimental import pallas as pl
from jax.experimental.pallas import tpu as pltpu

F32 = jnp.float32
BF16 = jnp.bfloat16

D_MODEL = 1024
EPS = 1e-6
CHUNK = 64
TOPK_MAX = 256
A_HEADS, A_DIM = 8, 64
IDX_HEADS, IDX_DIM = 4, 64
B_HEADS, B_DIM = 8, 64
C_HEADS, C_KEY, C_VAL, C_RANK, C_TAU = 4, 64, 128, 16, 16.0
PEER_HEADS, PEER_NKEYS, PEER_QDIM, PEER_TOPK = 8, 128, 256, 16
N_BRANCH = 3

LANE = 128
MIXER_VMEM_LIMIT = 48 * 1024 * 1024

_SEGMENTS = (
    ("gl", 3072), ("aq", 512), ("bq", 512), ("bk", 512), ("bv", 512), ("cv", 512), ("cr", 512),
    ("iq", 256), ("cq", 256), ("ck", 256), ("akv", 128), ("ikw", 128), ("ca", 128),
)
COL = {}
_off = 0
for _name, _w in _SEGMENTS:
    COL[_name] = (_off, _w)
    _off += _w
Z_COLS = _off
Z_TILE = Z_COLS // 3

_REF_WIDTHS = (512, 64, 64, 256, 64, 4, 512, 512, 512, 256, 256, 512, 512, 16, 3072)


def _reorder_w_in(w_in):
    offs = np.cumsum((0,) + _REF_WIDTHS)
    p = [w_in[:, offs[i]:offs[i + 1]] for i in range(len(_REF_WIDTHS))]
    aq, ak, av, iq, ik, iw, bq, bk, bv, cq, ck, cv, cr, ca, gl = p
    d = w_in.shape[0]
    z = lambda n: jnp.zeros((d, n), w_in.dtype)
    cols = [gl, aq, bq, bk, bv, cv, cr, iq, cq, ck, ak, av, ik, iw, z(60), ca, z(112)]
    return jnp.concatenate(cols, axis=1).astype(BF16)


def _cb(name):
    off, w = COL[name]
    return off // w


def _mod_kernel(c_ref, w_ref, b_ref, o_ref):
    c = c_ref[...]
    ca = (c * jax.nn.sigmoid(c)).astype(BF16)
    o_ref[...] = jnp.dot(ca, w_ref[...].astype(BF16), preferred_element_type=F32) + b_ref[...]


def _mod(c, w_mod, b_mod):
    bsz, d = c.shape
    n = w_mod.shape[1]
    tn = 768
    return pl.pallas_call(
        _mod_kernel,
        grid=(n // tn,),
        in_specs=[pl.BlockSpec((bsz, d), lambda j: (0, 0)),
                  pl.BlockSpec((d, tn), lambda j: (0, j)),
                  pl.BlockSpec((1, tn), lambda j: (0, j))],
        out_specs=pl.BlockSpec((bsz, tn), lambda j: (0, j)),
        out_shape=jax.ShapeDtypeStruct((bsz, n), F32),
        name="adaln_mod",
    )(c, w_mod, b_mod.reshape(1, n))


def _inproj_kernel(h_ref, g_ref, sc_ref, sh_ref, w_ref, z_ref, n_scr):
    @pl.when(pl.program_id(1) == 0)
    def _():
        x = h_ref[...]
        y = x * lax.rsqrt(jnp.mean(x * x, axis=-1, keepdims=True) + EPS)
        n = y * g_ref[...] * (1.0 + sc_ref[0]) + sh_ref[0]
        n_scr[...] = n.astype(BF16)

    z_ref[...] = jnp.dot(n_scr[...], w_ref[...], preferred_element_type=F32).astype(BF16)


def _inproj(h2, norm_g, scale, shift, w_z, t_len, tm=512):
    n_tok, d = h2.shape
    tm = min(tm, t_len)
    bpt = t_len // tm
    return pl.pallas_call(
        _inproj_kernel,
        grid=(n_tok // tm, Z_COLS // Z_TILE),
        in_specs=[pl.BlockSpec((tm, d), lambda i, j: (i, 0)),
                  pl.BlockSpec((1, d), lambda i, j: (0, 0)),
                  pl.BlockSpec((1, 1, d), lambda i, j: (i // bpt, 0, 0)),
                  pl.BlockSpec((1, 1, d), lambda i, j: (i // bpt, 0, 0)),
                  pl.BlockSpec((d, Z_TILE), lambda i, j: (0, j))],
        out_specs=pl.BlockSpec((tm, Z_TILE), lambda i, j: (i, j)),
        out_shape=jax.ShapeDtypeStruct((n_tok, Z_COLS), BF16),
        scratch_shapes=[pltpu.VMEM((tm, d), BF16)],
        compiler_params=pltpu.CompilerParams(
            dimension_semantics=("parallel", "arbitrary"), vmem_limit_bytes=MIXER_VMEM_LIMIT),
        name="norm_inproj",
    )(h2, norm_g.reshape(1, d), scale, shift, w_z)


def _head_rms(x, gain):
    return x * lax.rsqrt(jnp.mean(x * x, axis=-1, keepdims=True) + EPS) * gain


DSA_QB = 256
DSA_GROUP = 1
DSA_BISECT = 24


def _dsa_kernel(aq_ref, akv_ref, iq_ref, ikwq_ref, ikwk_ref, qg_ref, kg_ref, o_ref, sm_scr, sel_scr,
                *, q_off, n_sel):
    qb = aq_ref.shape[1]
    tk = akv_ref.shape[1]
    q0 = (q_off + pl.program_id(1)) * qb
    t_pos = q0 + lax.broadcasted_iota(jnp.int32, (qb, 1), 0)
    limit = (t_pos // CHUNK + 1) * CHUNK
    s_pos = lax.broadcasted_iota(jnp.int32, (1, tk), 1)
    adm = s_pos < limit

    kidx = ikwk_ref[0][:, :IDX_DIM]
    qidx = iq_ref[0]
    w = ikwq_ref[0][:, IDX_DIM:IDX_DIM + IDX_HEADS].astype(F32)
    score = jnp.zeros((qb, tk), F32)
    for h in range(IDX_HEADS):
        rel = lax.dot_general(qidx[:, h * IDX_DIM:(h + 1) * IDX_DIM], kidx,
                              (((1,), (1,)), ((), ())), preferred_element_type=F32)
        score = score + w[:, h:h + 1] * jnp.maximum(rel, 0.0)
    neg_inf = jnp.float32(-jnp.inf)
    sm_scr[...] = jnp.where(adm, score, neg_inf)

    hi0 = jnp.max(sm_scr[...], axis=-1, keepdims=True)
    lo0 = jnp.min(jnp.where(adm, score, jnp.float32(jnp.inf)), axis=-1, keepdims=True)
    kf = jnp.float32(n_sel)

    def bisect(_, carry):
        lo, hi = carry
        mid = 0.5 * (lo + hi)
        cnt = jnp.sum(jnp.where(sm_scr[...] > mid, 1.0, 0.0), axis=-1, keepdims=True)
        below = cnt < kf
        return jnp.where(below, lo, mid), jnp.where(below, mid, hi)

    _, hi = lax.fori_loop(0, DSA_BISECT, bisect, (lo0, hi0))
    sm = sm_scr[...]
    v1 = jnp.max(jnp.where(sm <= hi, sm, neg_inf), axis=-1, keepdims=True)
    c_ge1 = jnp.sum(jnp.where(sm >= v1, 1.0, 0.0), axis=-1, keepdims=True)
    v2 = jnp.max(jnp.where(sm < v1, sm, neg_inf), axis=-1, keepdims=True)
    c_ge2 = jnp.sum(jnp.where(sm >= v2, 1.0, 0.0), axis=-1, keepdims=True)
    v3 = jnp.max(jnp.where(sm < v2, sm, neg_inf), axis=-1, keepdims=True)
    thr = jnp.where(c_ge1 >= kf, v1, jnp.where(c_ge2 >= kf, v2, v3))
    need = kf - jnp.sum(jnp.where(sm > thr, 1.0, 0.0), axis=-1, keepdims=True)

    r_i = lax.broadcasted_iota(jnp.int32, (LANE, LANE), 0)
    c_i = lax.broadcasted_iota(jnp.int32, (LANE, LANE), 1)
    incl = jnp.where(r_i <= c_i, 1.0, 0.0).astype(BF16)
    all_sel = limit <= n_sel
    run = jnp.zeros((qb, 1), F32)
    for j in range(tk // LANE):
        sl = slice(j * LANE, (j + 1) * LANE)
        smj = sm_scr[:, sl]
        eqj = jnp.where(smj == thr, 1.0, 0.0)
        inc = jnp.dot(eqj.astype(BF16), incl, preferred_element_type=F32)
        rank = inc - eqj + run
        picked = (smj > thr) | ((smj == thr) & (rank < need))
        admj = (j * LANE + lax.broadcasted_iota(jnp.int32, (1, LANE), 1)) < limit
        selj = (all_sel & admj) | (jnp.logical_not(all_sel) & picked)
        sel_scr[:, sl] = jnp.where(selj, 0.0, neg_inf)
        run = run + inc[:, LANE - 1:LANE]

    kv = akv_ref[0].astype(F32)
    k = _head_rms(kv[:, :A_DIM], kg_ref[...]).astype(BF16)
    v = akv_ref[0][:, A_DIM:2 * A_DIM]
    aq = aq_ref[0].astype(F32)
    scale = A_DIM ** -0.5
    for h in range(A_HEADS):
        qh = (_head_rms(aq[:, h * A_DIM:(h + 1) * A_DIM], qg_ref[...]) * scale).astype(BF16)
        logits = lax.dot_general(qh, k, (((1,), (1,)), ((), ())), preferred_element_type=F32) + sel_scr[...]
        m = jnp.max(logits, axis=-1, keepdims=True)
        p = jnp.exp(logits - m)
        denom = jnp.sum(p, axis=-1, keepdims=True)
        oh = jnp.dot(p.astype(BF16), v, preferred_element_type=F32) / denom
        o_ref[0, :, h * A_DIM:(h + 1) * A_DIM] = oh.astype(o_ref.dtype)


def _dsa(z3, qg, kg):
    bsz, t_len, _ = z3.shape
    n_sel = min(TOPK_MAX, t_len // 4)
    span = DSA_QB * DSA_GROUP
    outs = []
    for g in range(t_len // span):
        tk = span * (g + 1)
        q_off = g * DSA_GROUP
        outs.append(pl.pallas_call(
            functools.partial(_dsa_kernel, q_off=q_off, n_sel=n_sel),
            grid=(bsz, DSA_GROUP),
            in_specs=[pl.BlockSpec((1, DSA_QB, 512), lambda b, i, q_off=q_off: (b, q_off + i, _cb("aq"))),
                      pl.BlockSpec((1, tk, LANE), lambda b, i: (b, 0, _cb("akv"))),
                      pl.BlockSpec((1, DSA_QB, 256), lambda b, i, q_off=q_off: (b, q_off + i, _cb("iq"))),
                      pl.BlockSpec((1, DSA_QB, LANE), lambda b, i, q_off=q_off: (b, q_off + i, _cb("ikw"))),
                      pl.BlockSpec((1, tk, LANE), lambda b, i: (b, 0, _cb("ikw"))),
                      pl.BlockSpec((1, A_DIM), lambda b, i: (0, 0)),
                      pl.BlockSpec((1, A_DIM), lambda b, i: (0, 0))],
            out_specs=pl.BlockSpec((1, DSA_QB, 512), lambda b, i: (b, i, 0)),
            out_shape=jax.ShapeDtypeStruct((bsz, span, 512), BF16),
            scratch_shapes=[pltpu.VMEM((DSA_QB, tk), F32), pltpu.VMEM((DSA_QB, tk), F32)],
            compiler_params=pltpu.CompilerParams(
                dimension_semantics=("parallel", "arbitrary"), vmem_limit_bytes=MIXER_VMEM_LIMIT),
            name=f"dsa_attention_g{g}",
        )(z3, z3, z3, z3, z3, qg.reshape(1, A_DIM), kg.reshape(1, A_DIM)))
    return jnp.concatenate(outs, axis=1)


def _split_dot(x, m, x_is_lhs=True):
    hi = x.astype(BF16)
    lo = (x - hi.astype(F32)).astype(BF16)
    if x_is_lhs:
        return jnp.dot(hi, m, preferred_element_type=F32) + jnp.dot(lo, m, preferred_element_type=F32)
    return jnp.dot(m, hi, preferred_element_type=F32) + jnp.dot(m, lo, preferred_element_type=F32)


def _split_dot2(x, m2):
    hi = x.astype(BF16)
    lo = (x - hi.astype(F32)).astype(BF16)
    return jnp.dot(jnp.concatenate([hi, lo], axis=1), m2, preferred_element_type=F32)


SB_QB = 256
SB_KB = 128
SB_SWEEP = 4


def _sb_kernel(q_ref, k_ref, v_ref, qg_ref, kg_ref, o_ref, kn_scr):
    qi = pl.program_id(2)
    qb = q_ref.shape[1]
    hd = B_DIM

    @pl.when(qi == 0)
    def _():
        kf = k_ref[0].astype(F32)
        for h in range(2):
            kn_scr[h] = _head_rms(kf[:, h * hd:(h + 1) * hd], kg_ref[...]).astype(BF16)

    q = q_ref[0].astype(F32)
    scale = hd ** -0.5
    qn = [(_head_rms(q[:, h * hd:(h + 1) * hd], qg_ref[...]) * scale).astype(BF16) for h in range(2)]
    kb = SB_KB
    r_i = lax.broadcasted_iota(jnp.int32, (kb, kb), 0)
    c_i = lax.broadcasted_iota(jnp.int32, (kb, kb), 1)
    m_incl = jnp.where(r_i >= c_i, 1.0, 0.0).astype(BF16)
    m_incl2 = jnp.concatenate([m_incl, m_incl], axis=0)
    t_pos = qi * qb + lax.broadcasted_iota(jnp.int32, (qb, 1), 0)
    lane = lax.broadcasted_iota(jnp.int32, (1, 2 * hd), 1)
    last_sweep = ((qi + 1) * (qb // kb) - 1) // SB_SWEEP

    def body(i, carry):
        sb = last_sweep - i
        carry = list(carry)
        blocks = list(reversed(range(SB_SWEEP)))
        starts = [pl.multiple_of((sb * SB_SWEEP + j) * kb, kb) for j in blocks]
        stricts = [(ks + lax.broadcasted_iota(jnp.int32, (1, kb), 1)) < t_pos for ks in starts]
        zs = [[lax.dot_general(qn[h], kn_scr[h, pl.ds(ks, kb), :], (((1,), (1,)), ((), ())),
                               preferred_element_type=F32) for h in range(2)] for ks in starts]
        drops = [[jnp.where(st, jnp.maximum(z, 0.0) + jnp.log(1.0 + jnp.exp(-jnp.abs(z))), 0.0) for z in zb]
                 for st, zb in zip(stricts, zs)]
        sums = [[_split_dot2(d, m_incl2) for d in db] for db in drops]
        probs = []
        for st, zb, sb_ in zip(stricts, zs, sums):
            row = []
            for h in range(2):
                row.append(jnp.where(st, jnp.exp(zb[h] - sb_[h] - carry[2 * h]), 0.0).astype(BF16))
                carry[2 * h] = carry[2 * h] + sb_[h][:, 0:1]
            probs.append(row)
        for ks, row in zip(starts, probs):
            vblk = v_ref[0, pl.ds(ks, kb), :]
            for h in range(2):
                carry[2 * h + 1] = carry[2 * h + 1] + jnp.dot(row[h], vblk, preferred_element_type=F32)
        return tuple(carry)

    init = (jnp.zeros((qb, 1), F32), jnp.zeros((qb, 2 * hd), F32)) * 2
    res = lax.fori_loop(0, last_sweep + 1, body, init)
    o_ref[0] = jnp.where(lane < hd, res[1], res[3]).astype(o_ref.dtype)


def _stickbreak(z3, qg, kg):
    bsz, t_len, _ = z3.shape
    nq = t_len // SB_QB
    npair = B_HEADS // 2
    cq, ck, cv = COL["bq"][0] // LANE, COL["bk"][0] // LANE, COL["bv"][0] // LANE
    return pl.pallas_call(
        _sb_kernel,
        grid=(bsz, npair, nq),
        in_specs=[pl.BlockSpec((1, SB_QB, LANE), lambda b, p, i: (b, i, cq + p)),
                  pl.BlockSpec((1, t_len, LANE), lambda b, p, i: (b, 0, ck + p)),
                  pl.BlockSpec((1, t_len, LANE), lambda b, p, i: (b, 0, cv + p)),
                  pl.BlockSpec((1, B_DIM), lambda b, p, i: (0, 0)),
                  pl.BlockSpec((1, B_DIM), lambda b, p, i: (0, 0))],
        out_specs=pl.BlockSpec((1, SB_QB, LANE), lambda b, p, i: (b, i, p)),
        out_shape=jax.ShapeDtypeStruct((bsz, t_len, B_HEADS * B_DIM), BF16),
        scratch_shapes=[pltpu.VMEM((2, t_len, B_DIM), BF16)],
        compiler_params=pltpu.CompilerParams(
            dimension_semantics=("parallel", "parallel", "arbitrary"), vmem_limit_bytes=MIXER_VMEM_LIMIT),
        name="stickbreak_attention",
    )(z3, z3, z3, qg.reshape(1, B_DIM), kg.reshape(1, B_DIM))


GLA_SUB = 16
GLA_UNROLL = 4


def _gla_kernel(q_ref, k_ref, v_ref, r_ref, a_ref, wa_ref, ba_ref, og_ref, o_ref, st_scr):
    t_len = q_ref.shape[1]
    nc = t_len // CHUNK
    dk, dv, sub = C_KEY, C_VAL, GLA_SUB
    r_i = lax.broadcasted_iota(jnp.int32, (CHUNK, CHUNK), 0)
    c_i = lax.broadcasted_iota(jnp.int32, (CHUNK, CHUNK), 1)
    l_incl = jnp.where(r_i >= c_i, 1.0, 0.0).astype(BF16)
    st_scr[...] = jnp.zeros_like(st_scr)
    neg_inf = jnp.float32(-jnp.inf)
    pr = lax.broadcasted_iota(jnp.int32, (sub * sub, 1), 0)
    pair_ok = pr % sub <= pr // sub
    fr = lax.broadcasted_iota(jnp.int32, (sub, sub * sub), 0)
    fc = lax.broadcasted_iota(jnp.int32, (sub, sub * sub), 1)
    fold_t = jnp.where(fc // sub == fr, 1.0, 0.0).astype(BF16)
    hr = lax.broadcasted_iota(jnp.int32, (2 * dk, 2 * dv), 0)
    hc = lax.broadcasted_iota(jnp.int32, (2 * dk, 2 * dv), 1)
    head_ones = jnp.where(hr // dk == hc // dv, 1.0, 0.0).astype(BF16)

    def per_t(x):
        return jnp.broadcast_to(x[:, None, :], (sub, sub, x.shape[1])).reshape(sub * sub, x.shape[1])

    def per_s(x):
        return jnp.broadcast_to(x[None, :, :], (sub, sub, x.shape[1])).reshape(sub * sub, x.shape[1])

    def trip(i, _):
        rows = [pl.ds(pl.multiple_of((i * GLA_UNROLL + u) * CHUNK, CHUNK), CHUNK) for u in range(GLA_UNROLL)]
        pres = [jnp.dot(a_ref[0, r, :], wa_ref[...], preferred_element_type=F32) + ba_ref[...] for r in rows]
        la2s = [(jnp.minimum(p, 0.0) - jnp.log(1.0 + jnp.exp(-jnp.abs(p)))) * (1.0 / C_TAU) for p in pres]
        b2s = [_split_dot(la2, l_incl, x_is_lhs=False) for la2 in la2s]
        q2s = [q_ref[0, r, :].astype(F32) * (dk ** -0.5) for r in rows]
        k2s = [k_ref[0, r, :].astype(F32) for r in rows]
        v2fs = [v_ref[0, r, :].astype(F32) for r in rows]

        terms, vss = [], []
        for b2, q2, k2, v2f in zip(b2s, q2s, k2s, v2fs):
            for j in range(CHUNK // sub):
                sl = slice(j * sub, (j + 1) * sub)
                bt, qt = (per_t(x[sl]) for x in (b2, q2))
                bs, ks_, vs = (per_s(x[sl]) for x in (b2, k2, v2f))
                terms.append(qt * ks_ * jnp.exp(jnp.where(pair_ok, bt - bs, neg_inf)))
                vss.append(vs)
        w2s = [_split_dot(t, head_ones) for t in terms]
        diags = [jnp.dot(fold_t, (w2 * vs).astype(BF16), preferred_element_type=F32)
                 for w2, vs in zip(w2s, vss)]

        per_chunk = []
        for u in range(GLA_UNROLL):
            heads = []
            for h in range(2):
                b = b2s[u][:, h * dk:(h + 1) * dk]
                q = q2s[u][:, h * dk:(h + 1) * dk]
                k = k2s[u][:, h * dk:(h + 1) * dk]
                v = v_ref[0, rows[u], h * dv:(h + 1) * dv]
                atts = []
                for j in range(1, CHUNK // sub):
                    lo_r = j * sub
                    ref_row = b[lo_r - 1:lo_r, :]
                    qs = (q[lo_r:lo_r + sub] * jnp.exp(b[lo_r:lo_r + sub] - ref_row)).astype(BF16)
                    ks = (k[:lo_r] * jnp.exp(ref_row - b[:lo_r])).astype(BF16)
                    atts.append(lax.dot_general(qs, ks, (((1,), (1,)), ((), ())), preferred_element_type=F32))
                b_last = b[CHUNK - 1:CHUNK, :]
                heads.append(dict(
                    v=v, atts=atts, qe=(q * jnp.exp(b)).astype(BF16), decay=jnp.exp(b_last),
                    kd=(k * jnp.exp(b_last - b)).astype(BF16)))
            per_chunk.append(heads)
        for u in range(GLA_UNROLL):
            for h in range(2):
                hd = per_chunk[u][h]
                intra = [diags[u * (CHUNK // sub)][:, h * dv:(h + 1) * dv]]
                for j in range(1, CHUNK // sub):
                    lo_r = j * sub
                    intra.append(diags[u * (CHUNK // sub) + j][:, h * dv:(h + 1) * dv]
                                 + jnp.dot(hd["atts"][j - 1].astype(BF16), hd["v"][:lo_r], preferred_element_type=F32))
                hd["intra"] = jnp.concatenate(intra, axis=0)
                hd["upd"] = lax.dot_general(hd["v"], hd["kd"], (((0,), (0,)), ((), ())),
                                            preferred_element_type=F32)

        for h in range(2):
            st = st_scr[h]
            for u in range(GLA_UNROLL):
                hd = per_chunk[u][h]
                o = lax.dot_general(hd["qe"], st.astype(BF16), (((1,), (1,)), ((), ())),
                                    preferred_element_type=F32) + hd["intra"]
                st = st * hd["decay"] + hd["upd"]
                y = o * lax.rsqrt(jnp.mean(o * o, axis=-1, keepdims=True) + EPS) * og_ref[:, h * dv:(h + 1) * dv]
                r = r_ref[0, rows[u], h * dv:(h + 1) * dv].astype(F32)
                o_ref[0, rows[u], h * dv:(h + 1) * dv] = (y * (r * jax.nn.sigmoid(r))).astype(o_ref.dtype)
            st_scr[h] = st
        return 0

    lax.fori_loop(0, nc // GLA_UNROLL, trip, 0)


def _gla(z3, w_alpha, b_alpha, out_gain):
    bsz, t_len, _ = z3.shape
    npair = C_HEADS // 2
    wa = jnp.zeros((LANE, C_HEADS * C_KEY), F32).at[:C_RANK].set(w_alpha).astype(BF16)
    cq, ck = COL["cq"][0] // LANE, COL["ck"][0] // LANE
    cv, cr, ca = COL["cv"][0] // 256, COL["cr"][0] // 256, COL["ca"][0] // LANE
    return pl.pallas_call(
        _gla_kernel,
        grid=(bsz, npair),
        in_specs=[pl.BlockSpec((1, t_len, LANE), lambda b, p: (b, 0, cq + p)),
                  pl.BlockSpec((1, t_len, LANE), lambda b, p: (b, 0, ck + p)),
                  pl.BlockSpec((1, t_len, 256), lambda b, p: (b, 0, cv + p)),
                  pl.BlockSpec((1, t_len, 256), lambda b, p: (b, 0, cr + p)),
                  pl.BlockSpec((1, t_len, LANE), lambda b, p: (b, 0, ca)),
                  pl.BlockSpec((LANE, LANE), lambda b, p: (0, p)),
                  pl.BlockSpec((1, LANE), lambda b, p: (0, p)),
                  pl.BlockSpec((1, 256), lambda b, p: (0, p))],
        out_specs=pl.BlockSpec((1, t_len, 256), lambda b, p: (b, 0, p)),
        out_shape=jax.ShapeDtypeStruct((bsz, t_len, C_HEADS * C_VAL), BF16),
        scratch_shapes=[pltpu.VMEM((2, C_VAL, C_KEY), F32)],
        compiler_params=pltpu.CompilerParams(
            dimension_semantics=("parallel", "parallel"), vmem_limit_bytes=MIXER_VMEM_LIMIT),
        name="gla_attention",
    )(z3, z3, z3, z3, z3, wa, b_alpha.reshape(1, -1), out_gain.reshape(1, -1))


def _merge_kernel(h_ref, g0_ref, g1_ref, g2_ref, bg_ref, ya_ref, yb_ref, yc_ref, woa_ref, wob_ref, woc_ref,
                  wout_ref, gate_ref, ng_ref, sc_ref, sh_ref, wpq_ref, h1_ref, n2_ref, q_ref):
    d = h_ref.shape[1]
    merged = jnp.zeros(h_ref.shape, F32)
    for i, (g_ref, y_ref, w_ref) in enumerate(((g0_ref, ya_ref, woa_ref), (g1_ref, yb_ref, wob_ref),
                                               (g2_ref, yc_ref, woc_ref))):
        gate = jax.nn.sigmoid(g_ref[...].astype(F32) + bg_ref[:, i * d:(i + 1) * d])
        merged = merged + gate * jnp.dot(y_ref[...], w_ref[...], preferred_element_type=F32)
    y = jnp.dot(merged.astype(BF16), wout_ref[...], preferred_element_type=F32)
    h1 = h_ref[...] + gate_ref[0] * y
    h1_ref[...] = h1
    n = h1 * lax.rsqrt(jnp.mean(h1 * h1, axis=-1, keepdims=True) + EPS)
    n2 = (n * ng_ref[...] * (1.0 + sc_ref[0]) + sh_ref[0]).astype(BF16)
    n2_ref[...] = n2
    q_ref[...] = jnp.dot(n2, wpq_ref[...], preferred_element_type=F32).astype(BF16)


def _merge(h2, z, ya, yb, yc, b_gate, w_oa, w_ob, w_oc, w_out, gate1, norm_g, scale, shift, w_pq, t_len, tm=256):
    n_tok, d = h2.shape
    bpt = t_len // tm
    nq = w_pq.shape[1]
    row = lambda i: (i, 0)
    fixed = lambda i: (0, 0)
    per_b = lambda i: (i // bpt, 0, 0)
    bw = ya.shape[1]
    return pl.pallas_call(
        _merge_kernel,
        grid=(n_tok // tm,),
        in_specs=[pl.BlockSpec((tm, d), row),
                  pl.BlockSpec((tm, d), lambda i: (i, 0)),
                  pl.BlockSpec((tm, d), lambda i: (i, 1)),
                  pl.BlockSpec((tm, d), lambda i: (i, 2)),
                  pl.BlockSpec((1, N_BRANCH * d), fixed),
                  pl.BlockSpec((tm, bw), row), pl.BlockSpec((tm, bw), row), pl.BlockSpec((tm, bw), row),
                  pl.BlockSpec((bw, d), fixed), pl.BlockSpec((bw, d), fixed), pl.BlockSpec((bw, d), fixed),
                  pl.BlockSpec((d, d), fixed),
                  pl.BlockSpec((1, 1, d), per_b),
                  pl.BlockSpec((1, d), fixed),
                  pl.BlockSpec((1, 1, d), per_b), pl.BlockSpec((1, 1, d), per_b),
                  pl.BlockSpec((d, nq), fixed)],
        out_specs=[pl.BlockSpec((tm, d), row), pl.BlockSpec((tm, d), row), pl.BlockSpec((tm, nq), row)],
        out_shape=[jax.ShapeDtypeStruct((n_tok, d), F32), jax.ShapeDtypeStruct((n_tok, d), BF16),
                   jax.ShapeDtypeStruct((n_tok, nq), BF16)],
        compiler_params=pltpu.CompilerParams(
            dimension_semantics=("parallel",), vmem_limit_bytes=MIXER_VMEM_LIMIT),
        name="merge_outproj_peerq",
    )(h2, z, z, z, b_gate.reshape(1, -1), ya, yb, yc, w_oa.astype(BF16), w_ob.astype(BF16), w_oc.astype(BF16),
      w_out.astype(BF16), gate1, norm_g.reshape(1, d), scale, shift, w_pq.astype(BF16))


PEER_TB = 128


def _take_top_steps(vals, ids, k, out_v, out_i):
    neg_inf = jnp.float32(-jnp.inf)
    big = jnp.float32(2.0 ** 30)
    for _ in range(k):
        m = jnp.max(vals, axis=0, keepdims=True)
        first = jnp.min(jnp.where(vals == m, ids, big), axis=0, keepdims=True)
        out_v.append(m)
        out_i.append(first)
        vals = jnp.where(ids == first, neg_inf, vals)
        yield


PEER_ROUNDS = 3 * PEER_TOPK


def _retrieve_head_steps(q_ref, sk_ref, h, eid_dst, g_dst):
    tb = q_ref.shape[0]
    hq = PEER_QDIM // 2
    k = PEER_TOPK
    n_exp = PEER_NKEYS * PEER_NKEYS
    key_id = lax.broadcasted_iota(jnp.int32, (PEER_NKEYS, tb), 0).astype(F32)
    n_j = [k if i == 0 else 8 for i in range(k)]
    pos = jnp.concatenate([(i * k + lax.broadcasted_iota(jnp.int32, (n_j[i], tb), 0)) * n_exp
                           for i in range(k)], axis=0).astype(F32)
    tops, topi = [], []
    for p in range(2):
        col = pl.multiple_of((2 * h + p) * hq, hq)
        qhp = q_ref[:, pl.ds(col, hq)]
        s = lax.dot_general(sk_ref[h, p], qhp, (((1,), (1,)), ((), ())), preferred_element_type=F32)
        v, ids = [], []
        yield from _take_top_steps(s, key_id, k, v, ids)
        tops.append(v)
        topi.append(ids)
    b_s = jnp.concatenate(tops[1], axis=0)
    b_i = jnp.concatenate(topi[1], axis=0)
    cand = jnp.concatenate([tops[0][i] + b_s[:n_j[i]] for i in range(k)], axis=0)
    cid = jnp.concatenate([topi[0][i] * PEER_NKEYS + b_i[:n_j[i]] for i in range(k)], axis=0)
    best, tag = [], []
    yield from _take_top_steps(cand, pos + cid, k, best, tag)
    best = jnp.concatenate(best, axis=0)
    e = jnp.exp(best - best[0:1])
    rows = pl.ds(pl.multiple_of(h * k, k), k)
    g_dst[rows, :] = e / jnp.sum(e, axis=0, keepdims=True)
    eid_dst[rows, :] = jnp.concatenate(tag, axis=0).astype(jnp.int32) % n_exp


PEER_VMEM_LIMIT = 50 * 1024 * 1024
SUBLANE = 8
PACK = 2 * SUBLANE
PEER_GRP = LANE // PACK
PEER_TOK_UNROLL = 8


def _pair_tiles(tab):
    e, d = tab.shape
    bits = lax.bitcast_convert_type(tab.astype(BF16), jnp.uint16).astype(jnp.uint32)
    bits = bits.reshape(e // 2, 2, d // LANE, LANE)
    return (bits[:, 0] | (bits[:, 1] << 16)).reshape(e // 2 * SUBLANE, LANE)


def _tile(tab_ref, word_row):
    words = tab_ref[pl.ds(pl.multiple_of(word_row, SUBLANE), SUBLANE), :]
    return pltpu.bitcast(words, BF16)


def _dup_tiles(x2):
    n, d = x2.shape
    x4 = x2.reshape(n, d // LANE, 1, LANE)
    return jnp.broadcast_to(x4, (n, d // LANE, 2, LANE)).reshape(n, PACK, LANE)


def _expand_mats(slots):
    col = np.arange(slots * PACK)
    s_of = (col // LANE) * PEER_GRP + (col % LANE) // PACK
    h_of = col % 2
    m = np.zeros((2, slots, slots * PACK), np.float32)
    for h in range(2):
        m[h, s_of[h_of == h], col[h_of == h]] = 1.0
    return jnp.asarray(m, BF16)


def _table_spec(tab):
    return pl.BlockSpec(tab.shape, lambda i: (0, 0), pipeline_mode=pl.Buffered(1))


PEER_STAGE = 2


def _index_stage(slots):
    return [pltpu.SMEM((PEER_STAGE, PEER_TOK_UNROLL, slots), jnp.int32), pltpu.SemaphoreType.DMA((PEER_STAGE,))]


def _for_each_token(rows_ref, idx_smem, sems, tb, token_fn, side_steps=None, steps_per_token=0):
    u = PEER_TOK_UNROLL
    ntrip = tb // u

    def stage(trip, buf):
        src = rows_ref.at[pl.ds(pl.multiple_of(trip * u, u), u), :]
        return pltpu.make_async_copy(src, idx_smem.at[buf], sems.at[buf])

    for b in range(PEER_STAGE):
        stage(b, b).start()

    def outer(j, carry):
        side = side_steps(j) if side_steps is not None else iter(())
        for b in range(PEER_STAGE):
            trip = j * PEER_STAGE + b
            stage(trip, b).wait()
            for tt in range(u):
                token_fn(trip * u + tt, lambda s, b=b, tt=tt: idx_smem[b, tt, s], trip, tt)
                for _ in range(steps_per_token):
                    next(side, None)

            @pl.when(trip + PEER_STAGE < ntrip)
            def _():
                stage(trip + PEER_STAGE, b).start()
        for _ in side:
            pass
        return carry

    lax.fori_loop(0, ntrip // PEER_STAGE, outer, 0)


def _peer_u_kernel(q_ref, sk_ref, xx_ref, ex_ref, tab_ref, cw_ref, rows_ref, eid_scr, g_scr, rows_scr,
                   ae_scr, ao_scr, idx_smem, sems):
    step = pl.program_id(0)
    cur = step % 2
    prev = 1 - cur
    tb = xx_ref.shape[0]
    slots = ae_scr.shape[0]

    @pl.when(step == 0)
    def _():
        rows_scr[1] = jnp.zeros(rows_scr.shape[1:], rows_scr.dtype)
        eid_scr[1] = jnp.zeros(eid_scr.shape[1:], eid_scr.dtype)
        g_scr[1] = jnp.zeros(g_scr.shape[1:], g_scr.dtype)

    lane = lax.broadcasted_iota(jnp.int32, (SUBLANE, tb), 1)
    r_i = lax.broadcasted_iota(jnp.int32, (PACK, LANE), 0)
    k_i = lax.broadcasted_iota(jnp.int32, (PACK, LANE), 1)
    sel = jnp.where((k_i // PACK == r_i % SUBLANE) & (k_i % 2 == r_i // SUBLANE), 1.0, 0.0).astype(BF16)

    def token(t, idx, trip, tt):
        xt = xx_ref[t]
        xg = jnp.concatenate([xt] * PEER_GRP, axis=0)
        here = lane == t
        for g in range(0, slots // PEER_GRP, 2):
            halves = []
            for gg in (g, g + 1):
                tiles = [_tile(tab_ref, idx(gg * PEER_GRP + p)) for p in range(PEER_GRP)]
                halves.append(jnp.concatenate(tiles, axis=0) * xg)
            part = jnp.dot(sel, jnp.concatenate(halves, axis=1), preferred_element_type=F32)
            for j, gg in enumerate((g, g + 1)):
                tot = jnp.sum(part[:, j * LANE:(j + 1) * LANE], axis=1, keepdims=True)
                rs = slice(gg * PEER_GRP, (gg + 1) * PEER_GRP)
                ae_scr[rs, :] = jnp.where(here, tot[:SUBLANE], ae_scr[rs, :])
                ao_scr[rs, :] = jnp.where(here, tot[SUBLANE:], ao_scr[rs, :])

    tokens_per_trip = PEER_TOK_UNROLL * PEER_STAGE
    assert tb // tokens_per_trip == PEER_HEADS and PEER_ROUNDS % tokens_per_trip == 0
    _for_each_token(rows_scr.at[prev], idx_smem, sems, tb, token,
                    side_steps=lambda h: _retrieve_head_steps(q_ref, sk_ref, h, eid_scr.at[cur], g_scr.at[cur]),
                    steps_per_token=PEER_ROUNDS // tokens_per_trip)

    eid_p = eid_scr[prev]
    odd = (eid_p & 1) == 1
    act = jnp.where(odd, ao_scr[...], ae_scr[...])
    coef = g_scr[prev] * (0.5 * act * (1.0 + lax.erf(act * (2.0 ** -0.5))))
    zero = jnp.zeros_like(coef)
    tn = (((0,), (0,)), ((), ()))
    cw = lax.dot_general(jnp.where(odd, zero, coef).astype(BF16), ex_ref[0], tn, preferred_element_type=F32)
    cw = cw + lax.dot_general(jnp.where(odd, coef, zero).astype(BF16), ex_ref[1], tn, preferred_element_type=F32)
    cw_ref[...] = cw

    rows_cur = jnp.transpose((eid_scr[cur] >> 1) * SUBLANE)
    rows_scr[cur] = rows_cur
    rows_ref[0] = rows_cur


def _peer_u(q, sub_keys, xx, tab_u, expand, tb):
    n_tok = xx.shape[0]
    nblk = n_tok // tb
    slots = PEER_HEADS * PEER_TOPK
    last = nblk - 1
    return pl.pallas_call(
        _peer_u_kernel,
        grid=(nblk + 1,),
        in_specs=[pl.BlockSpec((tb, q.shape[1]), lambda i: (jnp.minimum(i, last), 0)),
                  pl.BlockSpec(sub_keys.shape, lambda i: (0, 0, 0, 0)),
                  pl.BlockSpec((tb,) + xx.shape[1:], lambda i: (jnp.maximum(i - 1, 0), 0, 0)),
                  pl.BlockSpec(expand.shape, lambda i: (0, 0, 0)),
                  _table_spec(tab_u)],
        out_specs=[pl.BlockSpec((tb, slots * PACK), lambda i: (jnp.maximum(i - 1, 0), 0)),
                   pl.BlockSpec((1, tb, slots), lambda i: (jnp.minimum(i, last), 0, 0))],
        out_shape=[jax.ShapeDtypeStruct((n_tok, slots * PACK), F32),
                   jax.ShapeDtypeStruct((nblk, tb, slots), jnp.int32)],
        scratch_shapes=[pltpu.VMEM((2, slots, tb), jnp.int32), pltpu.VMEM((2, slots, tb), F32),
                        pltpu.VMEM((2, tb, slots), jnp.int32),
                        pltpu.VMEM((slots, tb), F32), pltpu.VMEM((slots, tb), F32)] + _index_stage(slots),
        compiler_params=pltpu.CompilerParams(
            dimension_semantics=("arbitrary",), vmem_limit_bytes=PEER_VMEM_LIMIT),
        name="peer_retrieve_u",
    )(q, sub_keys.astype(BF16), xx, expand, tab_u)


def _peer_v_kernel(rows_ref, cw_ref, tab_ref, o_ref, idx_smem, sems, *, slots):
    tb = o_ref.shape[0]
    r_i = lax.broadcasted_iota(jnp.int32, (PACK, LANE), 0)
    k_i = lax.broadcasted_iota(jnp.int32, (PACK, LANE), 1)
    keep = (k_i % PACK) // 2 == r_i % SUBLANE
    first = r_i < SUBLANE

    def token(t, idx, trip, tt):
        acc = jnp.zeros((PACK, 2 * LANE), F32)
        first_row = pl.multiple_of(trip * PEER_TOK_UNROLL, PEER_TOK_UNROLL)
        cwt = cw_ref[pl.ds(first_row, PEER_TOK_UNROLL), :][tt:tt + 1, :]
        for g in range(0, slots // PEER_GRP, 2):
            halves = [jnp.concatenate([_tile(tab_ref, idx(gg * PEER_GRP + p)) for p in range(PEER_GRP)],
                                      axis=0) for gg in (g, g + 1)]
            coef = jnp.where(first, jnp.broadcast_to(cwt[:, g * LANE:(g + 1) * LANE], (PACK, LANE)),
                             jnp.broadcast_to(cwt[:, (g + 1) * LANE:(g + 2) * LANE], (PACK, LANE)))
            lhs = jnp.where(keep, coef, 0.0).astype(BF16)
            acc = acc + jnp.dot(lhs, jnp.concatenate(halves, axis=1), preferred_element_type=F32)
        o_ref[t] = acc[:SUBLANE, :LANE] + acc[SUBLANE:, LANE:]

    _for_each_token(rows_ref.at[0], idx_smem, sems, tb, token)


def _peer_v(rows, cw, tab_v, tb):
    n_tok = cw.shape[0]
    slots = cw.shape[1] // PACK
    return pl.pallas_call(
        functools.partial(_peer_v_kernel, slots=slots),
        grid=(n_tok // tb,),
        in_specs=[pl.BlockSpec((1, tb, slots), lambda i: (i, 0, 0)),
                  pl.BlockSpec((tb, slots * PACK), lambda i: (i, 0)),
                  _table_spec(tab_v)],
        out_specs=pl.BlockSpec((tb, SUBLANE, LANE), lambda i: (i, 0, 0)),
        out_shape=jax.ShapeDtypeStruct((n_tok, SUBLANE, LANE), F32),
        scratch_shapes=_index_stage(slots),
        compiler_params=pltpu.CompilerParams(
            dimension_semantics=("arbitrary",), vmem_limit_bytes=PEER_VMEM_LIMIT),
        name="peer_v",
    )(rows, cw, tab_v)


def _residual_kernel(h_ref, p_ref, gate_ref, o_ref):
    o_ref[...] = h_ref[...] + gate_ref[0] * p_ref[...]


def _residual(h2, p2, gate, t_len, tm=512):
    n_tok, d = h2.shape
    bpt = t_len // tm
    return pl.pallas_call(
        _residual_kernel,
        grid=(n_tok // tm,),
        in_specs=[pl.BlockSpec((tm, d), lambda i: (i, 0)), pl.BlockSpec((tm, d), lambda i: (i, 0)),
                  pl.BlockSpec((1, 1, d), lambda i: (i // bpt, 0, 0))],
        out_specs=pl.BlockSpec((tm, d), lambda i: (i, 0)),
        out_shape=jax.ShapeDtypeStruct((n_tok, d), F32),
        compiler_params=pltpu.CompilerParams(dimension_semantics=("parallel",)),
        name="peer_residual",
    )(h2, p2, gate)


def _layer(h2, mod, t_len, p):
    n_tok, d = h2.shape
    bsz = n_tok // t_len
    z = _inproj(h2, p["norm1_g"], mod[:, 1], mod[:, 0], _reorder_w_in(p["w_in"]), t_len)
    z3 = z.reshape(bsz, t_len, Z_COLS)
    ya = _dsa(z3, p["a_qn_g"], p["a_kn_g"]).reshape(n_tok, -1)
    yb = _stickbreak(z3, p["b_qn_g"], p["b_kn_g"]).reshape(n_tok, -1)
    yc = _gla(z3, p["c_w_alpha"], p["c_b_alpha"], p["c_on_g"]).reshape(n_tok, -1)
    h1, n2, q = _merge(h2, z, ya, yb, yc, p["b_gate"], p["w_oa"], p["w_ob"], p["w_oc"], p["w_out"],
                       mod[:, 2], p["norm2_g"], mod[:, 4], mod[:, 3], p["peer_wq"], t_len)
    cw, rows = _peer_u(q, p["peer_subkeys"], _dup_tiles(n2), _pair_tiles(p["peer_u"]),
                       _expand_mats(PEER_HEADS * PEER_TOPK), PEER_TB)
    out = _peer_v(rows, cw, _pair_tiles(p["peer_v"]), PEER_TB)
    return _residual(h1, out.reshape(n_tok, d), mod[:, 5], t_len)


def kernel(x, c, norm1_g, norm2_g, w_mod, b_mod, w_in, b_gate, a_qn_g, a_kn_g, b_qn_g, b_kn_g, c_w_alpha, c_b_alpha, c_on_g, w_oa, w_ob, w_oc, w_out, peer_wq, peer_subkeys, peer_u, peer_v):
    bsz, t_len, d = x.shape
    params = dict(norm1_g=norm1_g, norm2_g=norm2_g, w_in=w_in, b_gate=b_gate, a_qn_g=a_qn_g, a_kn_g=a_kn_g,
                  b_qn_g=b_qn_g, b_kn_g=b_kn_g, c_w_alpha=c_w_alpha, c_b_alpha=c_b_alpha, c_on_g=c_on_g,
                  w_oa=w_oa, w_ob=w_ob, w_oc=w_oc, w_out=w_out, peer_wq=peer_wq, peer_subkeys=peer_subkeys,
                  peer_u=peer_u, peer_v=peer_v)
    h2 = x.reshape(bsz * t_len, d)
    for l in range(w_in.shape[0]):
        mod = _mod(c, w_mod[l], b_mod[l]).reshape(bsz, 6, 1, d)
        h2 = _layer(h2, mod, t_len, {k: v[l] for k, v in params.items()})
    return h2.reshape(bsz, t_len, d)
```
